```python
import math
import jax, jax.numpy as jnp
from jax import lax
import numpy as np

D_MODEL = 2048
BATCH = 4
SEQ = 2048
DEPTH = 1

HEAD_DIM = 128
MEM_LEN = 256
ATT_GROUPS = ((128, 1), (512, 4), (2048, 16))
ATT_HEADS_PER_GROUP = 4
N_ATT_GROUPS = len(ATT_GROUPS)
ATT_WIDTH = N_ATT_GROUPS * ATT_HEADS_PER_GROUP * HEAD_DIM
ATT_OUT_WIDTH = ATT_HEADS_PER_GROUP * HEAD_DIM
ATT_BLOCK = 128
HG_HEADS = 8
HG_KEY = 128
HG_VAL = 128
HG_WIDTH = HG_HEADS * HG_KEY
HG_CHUNK = 64
N_BRANCH = 2
IN_WIDTH = 3 * ATT_WIDTH + 4 * HG_WIDTH + N_BRANCH * D_MODEL
CROSS_HEADS = 4
CROSS_WIDTH = CROSS_HEADS * HEAD_DIM
D_FF = int(math.ceil(8 * D_MODEL / 3 / 256) * 256)
RMS_EPS = 1e-6

kernel_name = "hybrid_dilated_attn_hgrn2_gated_block"


def rms_norm(x, w):
    xf = x.astype(jnp.float32)
    y = xf * lax.rsqrt(jnp.mean(xf * xf, axis=-1, keepdims=True) + RMS_EPS)
    return (y * w.astype(jnp.float32)).astype(x.dtype)


def banded_window_attention(q, k, v, window):
    assert window <= ATT_BLOCK
    lead = q.shape[:-2]
    L, dh = q.shape[-2], q.shape[-1]
    nb = -(-L // ATT_BLOCK)
    Lp = nb * ATT_BLOCK
    if Lp != L:
        pad = [(0, 0)] * len(lead) + [(0, Lp - L), (0, 0)]
        q, k, v = jnp.pad(q, pad), jnp.pad(k, pad), jnp.pad(v, pad)
    qb = q.reshape(*lead, nb, ATT_BLOCK, dh)
    kb = k.reshape(*lead, nb, ATT_BLOCK, dh)
    vb = v.reshape(*lead, nb, ATT_BLOCK, dh)
    kk = jnp.concatenate([jnp.concatenate([jnp.zeros_like(kb[..., :1, :, :]), kb[..., :-1, :, :]], axis=-3), kb], axis=-2)
    vv = jnp.concatenate([jnp.concatenate([jnp.zeros_like(vb[..., :1, :, :]), vb[..., :-1, :, :]], axis=-3), vb], axis=-2)
    blk = jnp.arange(nb)[:, None, None] * ATT_BLOCK
    qpos = blk + jnp.arange(ATT_BLOCK)[None, :, None]
    kpos = blk - ATT_BLOCK + jnp.arange(2 * ATT_BLOCK)[None, None, :]
    dist = qpos - kpos
    mask = (dist >= 0) & (dist <= window) & (kpos >= 0)
    s = jnp.einsum('...nqd,...nkd->...nqk', qb, kk).astype(jnp.float32) * (dh ** -0.5)
    s = jnp.where(mask, s, -jnp.inf)
    m = jnp.max(s, axis=-1, keepdims=True)
    p = jnp.exp(s - m)
    l = jnp.sum(p, axis=-1, keepdims=True)
    o = jnp.einsum('...nqk,...nkd->...nqd', (p / l).astype(v.dtype), vv)
    lse = (m + jnp.log(l))[..., 0]
    o = o.reshape(*lead, Lp, dh)[..., :L, :]
    lse = lse.reshape(*lead, Lp)[..., :L]
    return o, lse


def dilated_window_attention(q, k, v, window, dilation):
    B, H, S, dh = q.shape
    L = S // dilation

    def to_residue(t):
        return t.reshape(B, H, L, dilation, dh).transpose(0, 1, 3, 2, 4)

    o, lse = banded_window_attention(to_residue(q), to_residue(k), to_residue(v), window // dilation)
    o = o.transpose(0, 1, 3, 2, 4).reshape(B, H, S, dh)
    lse = lse.transpose(0, 1, 3, 2).reshape(B, H, S)
    return o, lse


def hgrn2_chunked(q, log_f, k, v):
    B, H, S, K = q.shape
    V = v.shape[-1]
    C = HG_CHUNK
    N = S // C
    q, log_f, k = (t.reshape(B, H, N, C, K) for t in (q, log_f, k))
    v = v.reshape(B, H, N, C, V)
    b = jnp.cumsum(log_f, axis=-2)
    b_last = b[..., -1:, :]
    chunk_kv = jnp.einsum('bhnck,bhncv->bhnkv', k * jnp.exp(b_last - b), v)
    decay = jnp.exp(b_last[..., 0, :])

    def step(state, inp):
        dec, kv = inp
        return dec[..., None] * state + kv, state

    _, states = lax.scan(step, jnp.zeros((B, H, K, V), jnp.float32),
                         (jnp.moveaxis(decay, 2, 0), jnp.moveaxis(chunk_kv, 2, 0)))
    states = jnp.moveaxis(states, 0, 2)
    inter = jnp.einsum('bhnck,bhnkv->bhncv', q * jnp.exp(b), states)
    b_ref = b[..., C // 2:C // 2 + 1, :]
    a = jnp.einsum('bhnck,bhnsk->bhncs', q * jnp.exp(b - b_ref), k * jnp.exp(b_ref - b))
    causal = jnp.arange(C)[:, None] >= jnp.arange(C)[None, :]
    a = jnp.where(causal, a, 0.0)
    intra = jnp.einsum('bhncs,bhnsv->bhncv', a, v)
    return (inter + intra).reshape(B, H, S, V)


def setup_inputs(seed: int = 0) -> dict:
    key = jax.random.key(seed)
    ks = jax.random.split(key, 20)
    nrm = lambda k, shape, fan_in: jax.random.normal(k, shape, jnp.float32) * (fan_in ** -0.5)
    gain = lambda k, shape: 1.0 + 0.02 * jax.random.normal(k, shape, jnp.float32)
    return {
        "x": jax.random.normal(ks[0], (BATCH, SEQ, D_MODEL), jnp.float32),
        "mem": jax.random.normal(ks[1], (BATCH, MEM_LEN, D_MODEL), jnp.float32),
        "ln_mix_w": gain(ks[2], (DEPTH, D_MODEL)),
        "w_in": nrm(ks[3], (DEPTH, D_MODEL, IN_WIDTH), D_MODEL),
        "hg_norm_w": gain(ks[4], (DEPTH, HG_VAL)),
        "hg_lower_bounds": 0.1 * jax.random.normal(ks[5], (DEPTH + 1, HG_WIDTH), jnp.float32),
        "w_branch_a": nrm(ks[6], (DEPTH, ATT_OUT_WIDTH, D_MODEL), ATT_OUT_WIDTH),
        "w_branch_b": nrm(ks[7], (DEPTH, HG_WIDTH, D_MODEL), HG_WIDTH),
        "w_out": nrm(ks[8], (DEPTH, D_MODEL, D_MODEL), D_MODEL),
        "ln_cross_w": gain(ks[9], (DEPTH, D_MODEL)),
        "ln_mem_w": gain(ks[10], (DEPTH, D_MODEL)),
        "wq_cross": nrm(ks[11], (DEPTH, D_MODEL, CROSS_WIDTH), D_MODEL),
        "wkv_cross": nrm(ks[12], (DEPTH, D_MODEL, 2 * CROSS_WIDTH), D_MODEL),
        "wo_cross": nrm(ks[13], (DEPTH, CROSS_WIDTH, D_MODEL), CROSS_WIDTH),
        "ln_ffn_w": gain(ks[14], (DEPTH, D_MODEL)),
        "w1": nrm(ks[15], (DEPTH, D_MODEL, D_FF), D_MODEL),
        "w3": nrm(ks[16], (DEPTH, D_MODEL, D_FF), D_MODEL),
        "w2": nrm(ks[17], (DEPTH, D_FF, D_MODEL), D_FF),
        "ln_final_w": gain(ks[18], (D_MODEL,)),
    }


def reference(x, mem, ln_mix_w, w_in, hg_norm_w, hg_lower_bounds, w_branch_a, w_branch_b, w_out,
              ln_cross_w, ln_mem_w, wq_cross, wkv_cross, wo_cross, ln_ffn_w, w1, w3, w2, ln_final_w):
    B, S, D = x.shape
    M = mem.shape[1]
    lower_bounds = jnp.cumsum(jax.nn.softmax(hg_lower_bounds.astype(jnp.float32), axis=0), axis=0)
    for l in range(DEPTH):
        h = rms_norm(x, ln_mix_w[l])
        proj = h @ w_in[l]
        q_a, k_a, v_a, q_h, f_h, i_h, g_h, gates = jnp.split(
            proj, np.cumsum([ATT_WIDTH] * 3 + [HG_WIDTH] * 4).tolist(), axis=-1)

        def att_heads(t):
            return t.reshape(B, S, N_ATT_GROUPS, ATT_HEADS_PER_GROUP, HEAD_DIM).transpose(0, 2, 3, 1, 4)
        qa, ka, va = att_heads(q_a), att_heads(k_a), att_heads(v_a)
        outs, lses = [], []
        for g, (window, dilation) in enumerate(ATT_GROUPS):
            o_g, lse_g = dilated_window_attention(qa[:, g], ka[:, g], va[:, g], window, dilation)
            outs.append(o_g)
            lses.append(lse_g)
        outs = jnp.stack(outs, axis=1)
        alpha = jax.nn.softmax(jnp.stack(lses, axis=1), axis=1)
        o_att = jnp.sum(alpha[..., None].astype(outs.dtype) * outs, axis=1)
        o_att = o_att.transpose(0, 2, 1, 3).reshape(B, S, ATT_OUT_WIDTH)

        def hg_heads(t):
            return t.reshape(B, S, HG_HEADS, HG_KEY).transpose(0, 2, 1, 3).astype(jnp.float32)
        lb = lower_bounds[l].reshape(HG_HEADS, HG_KEY)[None, :, None, :]
        f = lb + (1.0 - lb) * jax.nn.sigmoid(hg_heads(f_h))
        o_hg = hgrn2_chunked(jax.nn.silu(hg_heads(q_h)), jnp.log(f), 1.0 - f, hg_heads(i_h))
        o_hg = o_hg * lax.rsqrt(jnp.mean(o_hg * o_hg, axis=-1, keepdims=True) + RMS_EPS) * hg_norm_w[l].astype(jnp.float32)
        o_hg = (o_hg * jax.nn.silu(hg_heads(g_h))).transpose(0, 2, 1, 3).reshape(B, S, HG_WIDTH).astype(x.dtype)

        gate_a, gate_b = jnp.split(jax.nn.sigmoid(gates), N_BRANCH, axis=-1)
        merged = gate_a * (o_att @ w_branch_a[l]) + gate_b * (o_hg @ w_branch_b[l])
        x = x + merged @ w_out[l]

        hc = rms_norm(x, ln_cross_w[l])
        mn = rms_norm(mem, ln_mem_w[l])
        qc = (hc @ wq_cross[l]).reshape(B, S, CROSS_HEADS, HEAD_DIM)
        kvc = (mn @ wkv_cross[l]).reshape(B, M, 2, CROSS_HEADS, HEAD_DIM)
        sc = jnp.einsum('bshd,bmhd->bhsm', qc, kvc[:, :, 0]).astype(jnp.float32) * (HEAD_DIM ** -0.5)
        pc = jax.nn.softmax(sc, axis=-1).astype(x.dtype)
        oc = jnp.einsum('bhsm,bmhd->bshd', pc, kvc[:, :, 1]).reshape(B, S, CROSS_WIDTH)
        x = x + oc @ wo_cross[l]

        hf = rms_norm(x, ln_ffn_w[l])
        x = x + (jax.nn.silu(hf @ w1[l]) * (hf @ w3[l])) @ w2[l]
    return rms_norm(x, ln_final_w)
```

```python
import functools

import jax
import jax.numpy as jnp
from jax import lax
from jax.experimental import pallas as pl
from jax.experimental.pallas import tpu as pltpu

F32 = jnp.float32
BF16 = jnp.bfloat16

HEAD_DIM = 128
ATT_GROUPS = ((128, 1), (512, 4), (2048, 16))
ATT_HEADS_PER_GROUP = 4
N_ATT_GROUPS = len(ATT_GROUPS)
ATT_WIDTH = N_ATT_GROUPS * ATT_HEADS_PER_GROUP * HEAD_DIM
ATT_OUT_WIDTH = ATT_HEADS_PER_GROUP * HEAD_DIM
ATT_BLOCK = 128
HG_HEADS = 8
HG_KEY = 128
HG_WIDTH = HG_HEADS * HG_KEY
HG_CHUNK = 64
CROSS_HEADS = 4
CROSS_WIDTH = CROSS_HEADS * HEAD_DIM
RMS_EPS = 1e-6

V7X_VMEM_LIMIT_BYTES = 56 * 1024 * 1024


def _params(n_axes):
    return pltpu.CompilerParams(
        dimension_semantics=("arbitrary",) * n_axes,
        vmem_limit_bytes=V7X_VMEM_LIMIT_BYTES,
    )


def _rms_kernel(x_ref, w_ref, o_ref):
    x = x_ref[...]
    ms = jnp.mean(x * x, axis=-1, keepdims=True)
    o_ref[...] = (x * lax.rsqrt(ms + RMS_EPS) * w_ref[...]).astype(o_ref.dtype)


def _rmsnorm(x, w, out_dtype, name, bm=512):
    m, d = x.shape
    return pl.pallas_call(
        _rms_kernel,
        out_shape=jax.ShapeDtypeStruct((m, d), out_dtype),
        grid=(m // bm,),
        in_specs=[pl.BlockSpec((bm, d), lambda i: (i, 0)), pl.BlockSpec((1, d), lambda i: (0, 0))],
        out_specs=pl.BlockSpec((bm, d), lambda i: (i, 0)),
        compiler_params=_params(1),
        name=name,
    )(x, w.reshape(1, d))


def _mm_kernel(a_ref, w_ref, o_ref):
    o_ref[...] = jnp.dot(a_ref[...], w_ref[...], preferred_element_type=F32).astype(o_ref.dtype)


def _matmul(a, w, col0, n, out_dtype, name, bm, bn):
    m, k = a.shape
    cb = col0 // bn
    return pl.pallas_call(
        _mm_kernel,
        out_shape=jax.ShapeDtypeStruct((m, n), out_dtype),
        grid=(m // bm, n // bn),
        in_specs=[
            pl.BlockSpec((bm, k), lambda i, j: (i, 0)),
            pl.BlockSpec((k, bn), lambda i, j: (0, j + cb)),
        ],
        out_specs=pl.BlockSpec((bm, bn), lambda i, j: (i, j)),
        compiler_params=_params(2),
        name=name,
    )(a, w)


def _mm_res_kernel(a_ref, w_ref, r_ref, o_ref):
    o_ref[...] = r_ref[...] + jnp.dot(a_ref[...], w_ref[...], preferred_element_type=F32)


def _matmul_residual(a, w, res, name, bm, bn):
    m, k = a.shape
    n = w.shape[1]
    return pl.pallas_call(
        _mm_res_kernel,
        out_shape=jax.ShapeDtypeStruct((m, n), F32),
        grid=(m // bm, n // bn),
        in_specs=[
            pl.BlockSpec((bm, k), lambda i, j: (i, 0)),
            pl.BlockSpec((k, bn), lambda i, j: (0, j)),
            pl.BlockSpec((bm, bn), lambda i, j: (i, j)),
        ],
        out_specs=pl.BlockSpec((bm, bn), lambda i, j: (i, j)),
        compiler_params=_params(2),
        name=name,
    )(a, w, res)


def _swiglu_kernel(a_ref, w1_ref, w3_ref, o_ref):
    a = a_ref[...]
    u = jnp.dot(a, w1_ref[...], preferred_element_type=F32)
    g = jnp.dot(a, w3_ref[...], preferred_element_type=F32)
    o_ref[...] = (u * jax.nn.sigmoid(u) * g).astype(o_ref.dtype)


def _swiglu(a, w1, w3, name, bm, bn):
    m, k = a.shape
    n = w1.shape[1]
    return pl.pallas_call(
        _swiglu_kernel,
        out_shape=jax.ShapeDtypeStruct((m, n), BF16),
        grid=(m // bm, n // bn),
        in_specs=[
            pl.BlockSpec((bm, k), lambda i, j: (i, 0)),
            pl.BlockSpec((k, bn), lambda i, j: (0, j)),
            pl.BlockSpec((k, bn), lambda i, j: (0, j)),
        ],
        out_specs=pl.BlockSpec((bm, bn), lambda i, j: (i, j)),
        compiler_params=_params(2),
        name=name,
    )(a, w1, w3)


def _merge_kernel(oa_ref, oh_ref, ga_ref, gb_ref, wa_ref, wb_ref, o_ref):
    pa = jnp.dot(oa_ref[...], wa_ref[...], preferred_element_type=F32)
    pb = jnp.dot(oh_ref[...], wb_ref[...], preferred_element_type=F32)
    merged = jax.nn.sigmoid(ga_ref[...]) * pa + jax.nn.sigmoid(gb_ref[...]) * pb
    o_ref[...] = merged.astype(o_ref.dtype)


def _merge(o_att, o_hg, gates, gate_col0, wa, wb, name, bm, bn):
    m = o_att.shape[0]
    d = wa.shape[1]
    ga0 = gate_col0 // bn
    gb0 = (gate_col0 + d) // bn
    return pl.pallas_call(
        _merge_kernel,
        out_shape=jax.ShapeDtypeStruct((m, d), BF16),
        grid=(m // bm, d // bn),
        in_specs=[
            pl.BlockSpec((bm, o_att.shape[1]), lambda i, j: (i, 0)),
            pl.BlockSpec((bm, o_hg.shape[1]), lambda i, j: (i, 0)),
            pl.BlockSpec((bm, bn), lambda i, j: (i, j + ga0)),
            pl.BlockSpec((bm, bn), lambda i, j: (i, j + gb0)),
            pl.BlockSpec((wa.shape[0], bn), lambda i, j: (0, j)),
            pl.BlockSpec((wb.shape[0], bn), lambda i, j: (0, j)),
        ],
        out_specs=pl.BlockSpec((bm, bn), lambda i, j: (i, j)),
        compiler_params=_params(2),
        name=name,
    )(o_att, o_hg, gates, gates, wa, wb)


def _softmax_block(q, kk, vv, mask):
    s = lax.dot_general(q, kk, (((1,), (1,)), ((), ())), preferred_element_type=F32) * (HEAD_DIM ** -0.5)
    s = jnp.where(mask, s, -jnp.inf)
    m = jnp.max(s, axis=-1, keepdims=True)
    p = jnp.exp(s - m)
    l = jnp.sum(p, axis=-1, keepdims=True)
    pn = (p * (1.0 / l)).astype(BF16)
    o = jnp.dot(pn, vv, preferred_element_type=F32)
    return o, m + jnp.log(l)


def _attn_kernel(q0, k0, v0, q1, k1, v1, q2, k2, v2, o_ref, st, og, ls):
    seq = q0.shape[0]
    blk = ATT_BLOCK
    for idx, ref in enumerate((q1, k1, v1, q2, k2, v2)):
        st[idx] = ref[...].astype(F32)

    qi = lax.broadcasted_iota(jnp.int32, (blk, 2 * blk), 0)
    kj = lax.broadcasted_iota(jnp.int32, (blk, 2 * blk), 1)
    dist = qi - kj + blk
    first_mask = (lax.broadcasted_iota(jnp.int32, (blk, blk), 0) >= lax.broadcasted_iota(jnp.int32, (blk, blk), 1))

    srcs = ((q0, k0, v0), (0, 1, 2), (3, 4, 5))
    for g, (window, dil) in enumerate(ATT_GROUPS):
        w = window // dil
        band_mask = (dist >= 0) & (dist <= w)
        sub_len = seq // dil
        nb = sub_len // blk
        for r in range(dil):
            for n in range(nb):
                def rows(first_blk, count):
                    start = first_blk * blk * dil + r
                    if dil == 1:
                        return pl.ds(start, count)
                    return pl.ds(start, count, stride=dil)

                def load(which, first_blk, count):
                    if g == 0:
                        return srcs[0][which][rows(first_blk, count), :]
                    return st[srcs[g][which], rows(first_blk, count), :].astype(BF16)

                q = load(0, n, blk)
                if n == 0:
                    o, lse = _softmax_block(q, load(1, 0, blk), load(2, 0, blk), first_mask)
                else:
                    o, lse = _softmax_block(q, load(1, n - 1, 2 * blk), load(2, n - 1, 2 * blk), band_mask)
                og[g, rows(n, blk), :] = o
                ls[g, rows(n, blk), :] = jnp.broadcast_to(lse, (blk, HEAD_DIM))

    for n in range(seq // blk):
        sl = pl.ds(n * blk, blk)
        l0, l1, l2 = ls[0, sl, :], ls[1, sl, :], ls[2, sl, :]
        mx = jnp.maximum(jnp.maximum(l0, l1), l2)
        e0, e1, e2 = jnp.exp(l0 - mx), jnp.exp(l1 - mx), jnp.exp(l2 - mx)
        inv = 1.0 / (e0 + e1 + e2)
        out = (e0 * inv) * og[0, sl, :] + (e1 * inv) * og[1, sl, :] + (e2 * inv) * og[2, sl, :]
        o_ref[sl, :] = out.astype(o_ref.dtype)


def _dilated_attention(qkv, name):
    b, s, _ = qkv.shape
    hpg = ATT_HEADS_PER_GROUP
    n_cols = ATT_WIDTH // HEAD_DIM

    def spec(which, g):
        return pl.BlockSpec((None, s, HEAD_DIM), lambda bi, h: (bi, 0, which * n_cols + g * hpg + h))

    in_specs = [spec(which, g) for g in range(N_ATT_GROUPS) for which in range(3)]
    return pl.pallas_call(
        _attn_kernel,
        out_shape=jax.ShapeDtypeStruct((b, s, ATT_OUT_WIDTH), BF16),
        grid=(b, hpg),
        in_specs=in_specs,
        out_specs=pl.BlockSpec((None, s, HEAD_DIM), lambda bi, h: (bi, 0, h)),
        scratch_shapes=[
            pltpu.VMEM((6, s, HEAD_DIM), F32),
            pltpu.VMEM((3, s, HEAD_DIM), F32),
            pltpu.VMEM((3, s, HEAD_DIM), F32),
        ],
        compiler_params=_params(2),
        name=name,
    )(*([qkv] * 9))


def _hgrn_kernel(layer, q_ref, f_ref, i_ref, g_ref, lbp_ref, nw_ref, o_ref, state):
    ts = q_ref.shape[0]
    c = HG_CHUNK

    @pl.when(pl.program_id(1) == 0)
    def _():
        state[...] = jnp.zeros_like(state)

    a = lbp_ref[...]
    e = jnp.exp(a - jnp.max(a, axis=0, keepdims=True))
    sm = e / jnp.sum(e, axis=0, keepdims=True)
    lb_all = jnp.sum(sm[: layer + 1], axis=0, keepdims=True)
    nw = nw_ref[...]

    ri = lax.broadcasted_iota(jnp.int32, (c, c), 0)
    ci = lax.broadcasted_iota(jnp.int32, (c, c), 1)
    causal = ri >= ci
    tri = causal.astype(F32)

    def chunk(ci_, carry):
        rows = pl.ds(pl.multiple_of(ci_ * c, c), c)
        for h in range(HG_HEADS):
            cols = pl.ds(h * HG_KEY, HG_KEY)
            lb = lb_all[:, h * HG_KEY:(h + 1) * HG_KEY]
            f = lb + (1.0 - lb) * jax.nn.sigmoid(f_ref[rows, cols])
            k = 1.0 - f
            bcum = jnp.dot(tri, jnp.log(f), preferred_element_type=F32, precision=lax.Precision.HIGHEST)
            b_last = bcum[c - 1:c, :]
            b_mid = bcum[c // 2:c // 2 + 1, :]
            qh = q_ref[rows, cols]
            qs = qh * jax.nn.sigmoid(qh)
            v = i_ref[rows, cols].astype(BF16)
            st = state[h]
            inter = lax.dot_general((qs * jnp.exp(bcum)).astype(BF16), st.astype(BF16),
                                    (((1,), (1,)), ((), ())), preferred_element_type=F32)
            att = lax.dot_general((qs * jnp.exp(bcum - b_mid)).astype(BF16),
                                  (k * jnp.exp(b_mid - bcum)).astype(BF16),
                                  (((1,), (1,)), ((), ())), preferred_element_type=F32)
            att = jnp.where(causal, att, 0.0)
            intra = jnp.dot(att.astype(BF16), v, preferred_element_type=F32)
            kv_t = lax.dot_general(v, (k * jnp.exp(b_last - bcum)).astype(BF16),
                                   (((0,), (0,)), ((), ())), preferred_element_type=F32)
            state[h] = jnp.exp(b_last) * st + kv_t
            o = inter + intra
            o = o * lax.rsqrt(jnp.mean(o * o, axis=-1, keepdims=True) + RMS_EPS) * nw
            gh = g_ref[rows, cols]
            o_ref[rows, cols] = (o * (gh * jax.nn.sigmoid(gh))).astype(o_ref.dtype)
        return carry

    lax.fori_loop(0, ts // c, chunk, 0)


def _hgrn2(proj, col0, lower_bounds_param, norm_w, layer, name, ts=512):
    b, s, _ = proj.shape
    cb = col0 // HG_WIDTH

    def spec(which):
        return pl.BlockSpec((None, ts, HG_WIDTH), lambda bi, t: (bi, t, cb + which))

    n_layers = lower_bounds_param.shape[0]
    return pl.pallas_call(
        functools.partial(_hgrn_kernel, layer),
        out_shape=jax.ShapeDtypeStruct((b, s, HG_WIDTH), BF16),
        grid=(b, s // ts),
        in_specs=[spec(0), spec(1), spec(2), spec(3),
                  pl.BlockSpec((n_layers, HG_WIDTH), lambda bi, t: (0, 0)),
                  pl.BlockSpec((1, HG_KEY), lambda bi, t: (0, 0))],
        out_specs=pl.BlockSpec((None, ts, HG_WIDTH), lambda bi, t: (bi, t, 0)),
        scratch_shapes=[pltpu.VMEM((HG_HEADS, HG_KEY, HG_KEY), F32)],
        compiler_params=_params(2),
        name=name,
    )(proj, proj, proj, proj, lower_bounds_param, norm_w.reshape(1, HG_KEY))


def _cross_kernel(q_ref, k_ref, v_ref, o_ref):
    rb = 256
    k = k_ref[...]
    v = v_ref[...]
    for n in range(q_ref.shape[0] // rb):
        sl = pl.ds(n * rb, rb)
        s = lax.dot_general(q_ref[sl, :], k, (((1,), (1,)), ((), ())), preferred_element_type=F32) * (HEAD_DIM ** -0.5)
        m = jnp.max(s, axis=-1, keepdims=True)
        p = jnp.exp(s - m)
        l = jnp.sum(p, axis=-1, keepdims=True)
        pn = (p * (1.0 / l)).astype(BF16)
        o_ref[sl, :] = jnp.dot(pn, v, preferred_element_type=F32).astype(o_ref.dtype)


def _cross_attention(qc, kvc, name):
    b, s, _ = qc.shape
    mlen = kvc.shape[1]
    return pl.pallas_call(
        _cross_kernel,
        out_shape=jax.ShapeDtypeStruct((b, s, CROSS_WIDTH), BF16),
        grid=(b, CROSS_HEADS),
        in_specs=[
            pl.BlockSpec((None, s, HEAD_DIM), lambda bi, h: (bi, 0, h)),
            pl.BlockSpec((None, mlen, HEAD_DIM), lambda bi, h: (bi, 0, h)),
            pl.BlockSpec((None, mlen, HEAD_DIM), lambda bi, h: (bi, 0, CROSS_HEADS + h)),
        ],
        out_specs=pl.BlockSpec((None, s, HEAD_DIM), lambda bi, h: (bi, 0, h)),
        compiler_params=_params(2),
        name=name,
    )(qc, kvc, kvc)


def kernel(x, mem, ln_mix_w, w_in, hg_norm_w, hg_lower_bounds, w_branch_a, w_branch_b, w_out, ln_cross_w, ln_mem_w, wq_cross, wkv_cross, wo_cross, ln_ffn_w, w1, w3, w2, ln_final_w):
    b, s, d = x.shape
    mlen = mem.shape[1]
    t = b * s
    depth = w_in.shape[0]
    qkv_w = 3 * ATT_WIDTH
    hg_w = 4 * HG_WIDTH

    xf = x.reshape(t, d)
    memf = mem.reshape(b * mlen, d)
    for l in range(depth):
        w_in_l = w_in[l].astype(BF16)
        h = _rmsnorm(xf, ln_mix_w[l], BF16, "rms_mix")
        qkv = _matmul(h, w_in_l, 0, qkv_w, BF16, "proj_qkv", bm=1024, bn=512)
        hgg = _matmul(h, w_in_l, qkv_w, hg_w + 2 * d, F32, "proj_hg_gates", bm=1024, bn=512)
        o_att = _dilated_attention(qkv.reshape(b, s, qkv_w), "dilated_attn")
        o_hg = _hgrn2(hgg.reshape(b, s, hg_w + 2 * d), 0, hg_lower_bounds, hg_norm_w[l], l, "hgrn2")
        merged = _merge(o_att.reshape(t, ATT_OUT_WIDTH), o_hg.reshape(t, HG_WIDTH), hgg, hg_w,
                        w_branch_a[l].astype(BF16), w_branch_b[l].astype(BF16), "merge", bm=1024, bn=512)
        xf = _matmul_residual(merged, w_out[l].astype(BF16), xf, "out_proj", bm=1024, bn=1024)
        hc = _rmsnorm(xf, ln_cross_w[l], BF16, "rms_cross")
        mn = _rmsnorm(memf, ln_mem_w[l], BF16, "rms_mem")
        qc = _matmul(hc, wq_cross[l].astype(BF16), 0, CROSS_WIDTH, BF16, "cross_q", bm=1024, bn=512)
        kvc = _matmul(mn, wkv_cross[l].astype(BF16), 0, 2 * CROSS_WIDTH, BF16, "cross_kv", bm=512, bn=512)
        oc = _cross_attention(qc.reshape(b, s, CROSS_WIDTH), kvc.reshape(b, mlen, 2 * CROSS_WIDTH), "cross_attn")
        xf = _matmul_residual(oc.reshape(t, CROSS_WIDTH), wo_cross[l].astype(BF16), xf, "cross_o", bm=1024, bn=1024)
        hf = _rmsnorm(xf, ln_ffn_w[l], BF16, "rms_ffn")
        act = _swiglu(hf, w1[l].astype(BF16), w3[l].astype(BF16), "ffn_up", bm=1024, bn=512)
        xf = _matmul_residual(act, w2[l].astype(BF16), xf, "ffn_down", bm=512, bn=512)
    out = _rmsnorm(xf, ln_final_w, F32, "rms_final")
    return out.reshape(b, s, d)
```

```python
import functools

import jax
import jax.numpy as jnp
from jax import lax
from jax.experimental import pallas as pl
from jax.experimental.pallas import tpu as pltpu

F32 = jnp.float32
BF16 = jnp.bfloat16

HEAD_DIM = 128
ATT_GROUPS = ((128, 1), (512, 4), (2048, 16))
ATT_HEADS_PER_GROUP = 4
N_ATT_GROUPS = len(ATT_GROUPS)
ATT_WIDTH = N_ATT_GROUPS * ATT_HEADS_PER_GROUP * HEAD_DIM
ATT_OUT_WIDTH = ATT_HEADS_PER_GROUP * HEAD_DIM
ATT_BLOCK = 128
HG_HEADS = 8
HG_KEY = 128
HG_WIDTH = HG_HEADS * HG_KEY
HG_CHUNK = 64
CROSS_HEADS = 4
CROSS_WIDTH = CROSS_HEADS * HEAD_DIM
RMS_EPS = 1e-6

V7X_VMEM_LIMIT_BYTES = 56 * 1024 * 1024


def _params(n_axes):
    return pltpu.CompilerParams(
        dimension_semantics=("arbitrary",) * n_axes,
        vmem_limit_bytes=V7X_VMEM_LIMIT_BYTES,
    )


def _rms_kernel(x_ref, w_ref, o_ref):
    x = x_ref[...]
    ms = jnp.mean(x * x, axis=-1, keepdims=True)
    o_ref[...] = (x * lax.rsqrt(ms + RMS_EPS) * w_ref[...]).astype(o_ref.dtype)


def _rmsnorm(x, w, out_dtype, name, bm=512):
    m, d = x.shape
    return pl.pallas_call(
        _rms_kernel,
        out_shape=jax.ShapeDtypeStruct((m, d), out_dtype),
        grid=(m // bm,),
        in_specs=[pl.BlockSpec((bm, d), lambda i: (i, 0)), pl.BlockSpec((1, d), lambda i: (0, 0))],
        out_specs=pl.BlockSpec((bm, d), lambda i: (i, 0)),
        compiler_params=_params(1),
        name=name,
    )(x, w.reshape(1, d))


def _mm_kernel(a_ref, w_ref, o_ref):
    o_ref[...] = jnp.dot(a_ref[...], w_ref[...], preferred_element_type=F32).astype(o_ref.dtype)


def _matmul(a, w, col0, n, out_dtype, name, bm, bn):
    m, k = a.shape
    cb = col0 // bn
    return pl.pallas_call(
        _mm_kernel,
        out_shape=jax.ShapeDtypeStruct((m, n), out_dtype),
        grid=(m // bm, n // bn),
        in_specs=[
            pl.BlockSpec((bm, k), lambda i, j: (i, 0)),
            pl.BlockSpec((k, bn), lambda i, j: (0, j + cb)),
        ],
        out_specs=pl.BlockSpec((bm, bn), lambda i, j: (i, j)),
        compiler_params=_params(2),
        name=name,
    )(a, w)


def _norm_proj_kernel(x_ref, lnw_ref, w_ref, o_ref, h_ref):
    @pl.when(pl.program_id(1) == 0)
    def _():
        rb = 256
        for n in range(x_ref.shape[0] // rb):
            sl = pl.ds(n * rb, rb)
            x = x_ref[sl, :]
            ms = jnp.mean(x * x, axis=-1, keepdims=True)
            h_ref[sl, :] = (x * lax.rsqrt(ms + RMS_EPS) * lnw_ref[...]).astype(h_ref.dtype)

    o_ref[...] = jnp.dot(h_ref[...], w_ref[...].astype(BF16), preferred_element_type=F32).astype(o_ref.dtype)


def _norm_proj(x, lnw, w, col0, n, out_dtype, name, bm, bn):
    m, k = x.shape
    cb = col0 // bn
    return pl.pallas_call(
        _norm_proj_kernel,
        out_shape=jax.ShapeDtypeStruct((m, n), out_dtype),
        grid=(m // bm, n // bn),
        in_specs=[
            pl.BlockSpec((bm, k), lambda i, j: (i, 0)),
            pl.BlockSpec((1, k), lambda i, j: (0, 0)),
            pl.BlockSpec((k, bn), lambda i, j: (0, j + cb)),
        ],
        out_specs=pl.BlockSpec((bm, bn), lambda i, j: (i, j)),
        scratch_shapes=[pltpu.VMEM((bm, k), BF16)],
        compiler_params=_params(2),
        name=name,
    )(x, lnw.reshape(1, k), w)


def _mm_res_kernel(a_ref, w_ref, r_ref, o_ref):
    o_ref[...] = r_ref[...] + jnp.dot(a_ref[...], w_ref[...], preferred_element_type=F32)


def _matmul_residual(a, w, res, name, bm, bn):
    m, k = a.shape
    n = w.shape[1]
    return pl.pallas_call(
        _mm_res_kernel,
        out_shape=jax.ShapeDtypeStruct((m, n), F32),
        grid=(m // bm, n // bn),
        in_specs=[
            pl.BlockSpec((bm, k), lambda i, j: (i, 0)),
            pl.BlockSpec((k, bn), lambda i, j: (0, j)),
            pl.BlockSpec((bm, bn), lambda i, j: (i, j)),
        ],
        out_specs=pl.BlockSpec((bm, bn), lambda i, j: (i, j)),
        compiler_params=_params(2),
        name=name,
    )(a, w, res)


def _mm_res_norm_kernel(emit_sum, a_ref, w_ref, r_ref, lnw_ref, *out_refs):
    y = r_ref[...] + jnp.dot(a_ref[...], w_ref[...], preferred_element_type=F32)
    ms = jnp.mean(y * y, axis=-1, keepdims=True)
    hn = y * lax.rsqrt(ms + RMS_EPS) * lnw_ref[...]
    if emit_sum:
        out_refs[0][...] = y
    out_refs[-1][...] = hn.astype(out_refs[-1].dtype)


def _matmul_residual_norm(a, w, res, lnw, norm_dtype, emit_sum, name, bm, single_buffer_w=False):
    m, k = a.shape
    n = w.shape[1]
    row = lambda i: (i, 0)
    w_spec = (pl.BlockSpec((k, n), lambda i: (0, 0), pipeline_mode=pl.Buffered(1)) if single_buffer_w
              else pl.BlockSpec((k, n), lambda i: (0, 0)))
    norm_shape = jax.ShapeDtypeStruct((m, n), norm_dtype)
    out_shape = (jax.ShapeDtypeStruct((m, n), F32), norm_shape) if emit_sum else norm_shape
    out_specs = (pl.BlockSpec((bm, n), row), pl.BlockSpec((bm, n), row)) if emit_sum else pl.BlockSpec((bm, n), row)
    return pl.pallas_call(
        functools.partial(_mm_res_norm_kernel, emit_sum),
        out_shape=out_shape,
        grid=(m // bm,),
        in_specs=[pl.BlockSpec((bm, k), row), w_spec, pl.BlockSpec((bm, n), row), pl.BlockSpec((1, n), lambda i: (0, 0))],
        out_specs=out_specs,
        compiler_params=_params(1),
        name=name,
    )(a, w, res, lnw.reshape(1, n))


def _swiglu_kernel(a_ref, w1_ref, w3_ref, o_ref):
    a = a_ref[...]
    u = jnp.dot(a, w1_ref[...].astype(BF16), preferred_element_type=F32)
    g = jnp.dot(a, w3_ref[...].astype(BF16), preferred_element_type=F32)
    o_ref[...] = (u * jax.nn.sigmoid(u) * g).astype(o_ref.dtype)


def _swiglu(a, w1, w3, name, bm, bn):
    m, k = a.shape
    n = w1.shape[1]
    return pl.pallas_call(
        _swiglu_kernel,
        out_shape=jax.ShapeDtypeStruct((m, n), BF16),
        grid=(m // bm, n // bn),
        in_specs=[
            pl.BlockSpec((bm, k), lambda i, j: (i, 0)),
            pl.BlockSpec((k, bn), lambda i, j: (0, j)),
            pl.BlockSpec((k, bn), lambda i, j: (0, j)),
        ],
        out_specs=pl.BlockSpec((bm, bn), lambda i, j: (i, j)),
        compiler_params=_params(2),
        name=name,
    )(a, w1, w3)


def _merge_kernel(oa_ref, oh_ref, ga_ref, gb_ref, wa_ref, wb_ref, o_ref):
    pa = jnp.dot(oa_ref[...], wa_ref[...], preferred_element_type=F32)
    pb = jnp.dot(oh_ref[...], wb_ref[...], preferred_element_type=F32)
    merged = jax.nn.sigmoid(ga_ref[...]) * pa + jax.nn.sigmoid(gb_ref[...]) * pb
    o_ref[...] = merged.astype(o_ref.dtype)


def _merge(o_att, o_hg, gates, gate_col0, wa, wb, name, bm, bn):
    m = o_att.shape[0]
    d = wa.shape[1]
    ga0 = gate_col0 // bn
    gb0 = (gate_col0 + d) // bn
    return pl.pallas_call(
        _merge_kernel,
        out_shape=jax.ShapeDtypeStruct((m, d), BF16),
        grid=(m // bm, d // bn),
        in_specs=[
            pl.BlockSpec((bm, o_att.shape[1]), lambda i, j: (i, 0)),
            pl.BlockSpec((bm, o_hg.shape[1]), lambda i, j: (i, 0)),
            pl.BlockSpec((bm, bn), lambda i, j: (i, j + ga0)),
            pl.BlockSpec((bm, bn), lambda i, j: (i, j + gb0)),
            pl.BlockSpec((wa.shape[0], bn), lambda i, j: (0, j)),
            pl.BlockSpec((wb.shape[0], bn), lambda i, j: (0, j)),
        ],
        out_specs=pl.BlockSpec((bm, bn), lambda i, j: (i, j)),
        compiler_params=_params(2),
        name=name,
    )(o_att, o_hg, gates, gates, wa, wb)


def _softmax_block(q, kk, vv, mask):
    s = lax.dot_general(q, kk, (((1,), (1,)), ((), ())), preferred_element_type=F32) * (HEAD_DIM ** -0.5)
    s = jnp.where(mask, s, -jnp.inf)
    m = jnp.max(s, axis=-1, keepdims=True)
    p = jnp.exp(s - m)
    l = jnp.sum(p, axis=-1, keepdims=True)
    pn = (p * (1.0 / l)).astype(BF16)
    o = jnp.dot(pn, vv, preferred_element_type=F32)
    return o, m + jnp.log(l)


def _attn_kernel(q0, k0, v0, q1, k1, v1, q2, k2, v2, o_ref, st, og, ls):
    seq = q0.shape[0]
    blk = ATT_BLOCK
    for idx, ref in enumerate((q1, k1, v1, q2, k2, v2)):
        st[idx] = ref[...].astype(F32)

    qi = lax.broadcasted_iota(jnp.int32, (blk, 2 * blk), 0)
    kj = lax.broadcasted_iota(jnp.int32, (blk, 2 * blk), 1)
    dist = qi - kj + blk
    first_mask = (lax.broadcasted_iota(jnp.int32, (blk, blk), 0) >= lax.broadcasted_iota(jnp.int32, (blk, blk), 1))

    srcs = ((q0, k0, v0), (0, 1, 2), (3, 4, 5))
    for g, (window, dil) in enumerate(ATT_GROUPS):
        w = window // dil
        band_mask = (dist >= 0) & (dist <= w)
        sub_len = seq // dil
        nb = sub_len // blk
        for r in range(dil):
            for n in range(nb):
                def rows(first_blk, count):
                    start = first_blk * blk * dil + r
                    if dil == 1:
                        return pl.ds(start, count)
                    return pl.ds(start, count, stride=dil)

                def load(which, first_blk, count):
                    if g == 0:
                        return srcs[0][which][rows(first_blk, count), :]
                    return st[srcs[g][which], rows(first_blk, count), :].astype(BF16)

                q = load(0, n, blk)
                if n == 0:
                    o, lse = _softmax_block(q, load(1, 0, blk), load(2, 0, blk), first_mask)
                else:
                    o, lse = _softmax_block(q, load(1, n - 1, 2 * blk), load(2, n - 1, 2 * blk), band_mask)
                og[g, rows(n, blk), :] = o
                ls[g, rows(n, blk), :] = jnp.broadcast_to(lse, (blk, HEAD_DIM))

    for n in range(seq // blk):
        sl = pl.ds(n * blk, blk)
        l0, l1, l2 = ls[0, sl, :], ls[1, sl, :], ls[2, sl, :]
        mx = jnp.maximum(jnp.maximum(l0, l1), l2)
        e0, e1, e2 = jnp.exp(l0 - mx), jnp.exp(l1 - mx), jnp.exp(l2 - mx)
        inv = 1.0 / (e0 + e1 + e2)
        out = (e0 * inv) * og[0, sl, :] + (e1 * inv) * og[1, sl, :] + (e2 * inv) * og[2, sl, :]
        o_ref[sl, :] = out.astype(o_ref.dtype)


def _dilated_attention(qkv, name):
    b, s, _ = qkv.shape
    hpg = ATT_HEADS_PER_GROUP
    n_cols = ATT_WIDTH // HEAD_DIM

    def spec(which, g):
        return pl.BlockSpec((None, s, HEAD_DIM), lambda bi, h: (bi, 0, which * n_cols + g * hpg + h))

    in_specs = [spec(which, g) for g in range(N_ATT_GROUPS) for which in range(3)]
    return pl.pallas_call(
        _attn_kernel,
        out_shape=jax.ShapeDtypeStruct((b, s, ATT_OUT_WIDTH), BF16),
        grid=(b, hpg),
        in_specs=in_specs,
        out_specs=pl.BlockSpec((None, s, HEAD_DIM), lambda bi, h: (bi, 0, h)),
        scratch_shapes=[
            pltpu.VMEM((6, s, HEAD_DIM), F32),
            pltpu.VMEM((3, s, HEAD_DIM), F32),
            pltpu.VMEM((3, s, HEAD_DIM), F32),
        ],
        compiler_params=_params(2),
        name=name,
    )(*([qkv] * 9))


def _hgrn_kernel(layer, q_ref, f_ref, i_ref, g_ref, lbp_ref, nw_ref, o_ref, state):
    ts = q_ref.shape[0]
    c = HG_CHUNK

    @pl.when(pl.program_id(1) == 0)
    def _():
        state[...] = jnp.zeros_like(state)

    a = lbp_ref[...]
    e = jnp.exp(a - jnp.max(a, axis=0, keepdims=True))
    sm = e / jnp.sum(e, axis=0, keepdims=True)
    lb_all = jnp.sum(sm[: layer + 1], axis=0, keepdims=True)
    nw = nw_ref[...]

    ri = lax.broadcasted_iota(jnp.int32, (c, c), 0)
    ci = lax.broadcasted_iota(jnp.int32, (c, c), 1)
    causal = ri >= ci
    tri = causal.astype(F32)

    def chunk(ci_, carry):
        rows = pl.ds(pl.multiple_of(ci_ * c, c), c)
        for h in range(HG_HEADS):
            cols = pl.ds(h * HG_KEY, HG_KEY)
            lb = lb_all[:, h * HG_KEY:(h + 1) * HG_KEY]
            f = lb + (1.0 - lb) * jax.nn.sigmoid(f_ref[rows, cols])
            k = 1.0 - f
            bcum = jnp.dot(tri, jnp.log(f), preferred_element_type=F32, precision=lax.Precision.HIGHEST)
            b_last = bcum[c - 1:c, :]
            b_mid = bcum[c // 2:c // 2 + 1, :]
            qh = q_ref[rows, cols]
            qs = qh * jax.nn.sigmoid(qh)
            v = i_ref[rows, cols].astype(BF16)
            st = state[h]
            inter = lax.dot_general((qs * jnp.exp(bcum)).astype(BF16), st.astype(BF16),
                                    (((1,), (1,)), ((), ())), preferred_element_type=F32)
            att = lax.dot_general((qs * jnp.exp(bcum - b_mid)).astype(BF16),
                                  (k * jnp.exp(b_mid - bcum)).astype(BF16),
                                  (((1,), (1,)), ((), ())), preferred_element_type=F32)
            att = jnp.where(causal, att, 0.0)
            intra = jnp.dot(att.astype(BF16), v, preferred_element_type=F32)
            kv_t = lax.dot_general(v, (k * jnp.exp(b_last - bcum)).astype(BF16),
                                   (((0,), (0,)), ((), ())), preferred_element_type=F32)
            state[h] = jnp.exp(b_last) * st + kv_t
            o = inter + intra
            o = o * lax.rsqrt(jnp.mean(o * o, axis=-1, keepdims=True) + RMS_EPS) * nw
            gh = g_ref[rows, cols]
            o_ref[rows, cols] = (o * (gh * jax.nn.sigmoid(gh))).astype(o_ref.dtype)
        return carry

    lax.fori_loop(0, ts // c, chunk, 0)


def _hgrn2(proj, col0, lower_bounds_param, norm_w, layer, name, ts=512):
    b, s, _ = proj.shape
    cb = col0 // HG_WIDTH

    def spec(which):
        return pl.BlockSpec((None, ts, HG_WIDTH), lambda bi, t: (bi, t, cb + which))

    n_layers = lower_bounds_param.shape[0]
    return pl.pallas_call(
        functools.partial(_hgrn_kernel, layer),
        out_shape=jax.ShapeDtypeStruct((b, s, HG_WIDTH), BF16),
        grid=(b, s // ts),
        in_specs=[spec(0), spec(1), spec(2), spec(3),
                  pl.BlockSpec((n_layers, HG_WIDTH), lambda bi, t: (0, 0)),
                  pl.BlockSpec((1, HG_KEY), lambda bi, t: (0, 0))],
        out_specs=pl.BlockSpec((None, ts, HG_WIDTH), lambda bi, t: (bi, t, 0)),
        scratch_shapes=[pltpu.VMEM((HG_HEADS, HG_KEY, HG_KEY), F32)],
        compiler_params=_params(2),
        name=name,
    )(proj, proj, proj, proj, lower_bounds_param, norm_w.reshape(1, HG_KEY))


def _cross_kernel(q_ref, k_ref, v_ref, o_ref):
    rb = 256
    k = k_ref[...]
    v = v_ref[...]
    for n in range(q_ref.shape[0] // rb):
        sl = pl.ds(n * rb, rb)
        s = lax.dot_general(q_ref[sl, :], k, (((1,), (1,)), ((), ())), preferred_element_type=F32) * (HEAD_DIM ** -0.5)
        m = jnp.max(s, axis=-1, keepdims=True)
        p = jnp.exp(s - m)
        l = jnp.sum(p, axis=-1, keepdims=True)
        pn = (p * (1.0 / l)).astype(BF16)
        o_ref[sl, :] = jnp.dot(pn, v, preferred_element_type=F32).astype(o_ref.dtype)


def _cross_attention(qc, kvc, name):
    b, s, _ = qc.shape
    mlen = kvc.shape[1]
    return pl.pallas_call(
        _cross_kernel,
        out_shape=jax.ShapeDtypeStruct((b, s, CROSS_WIDTH), BF16),
        grid=(b, CROSS_HEADS),
        in_specs=[
            pl.BlockSpec((None, s, HEAD_DIM), lambda bi, h: (bi, 0, h)),
            pl.BlockSpec((None, mlen, HEAD_DIM), lambda bi, h: (bi, 0, h)),
            pl.BlockSpec((None, mlen, HEAD_DIM), lambda bi, h: (bi, 0, CROSS_HEADS + h)),
        ],
        out_specs=pl.BlockSpec((None, s, HEAD_DIM), lambda bi, h: (bi, 0, h)),
        compiler_params=_params(2),
        name=name,
    )(qc, kvc, kvc)


def kernel(x, mem, ln_mix_w, w_in, hg_norm_w, hg_lower_bounds, w_branch_a, w_branch_b, w_out, ln_cross_w, ln_mem_w, wq_cross, wkv_cross, wo_cross, ln_ffn_w, w1, w3, w2, ln_final_w):
    b, s, d = x.shape
    mlen = mem.shape[1]
    t = b * s
    depth = w_in.shape[0]
    qkv_w = 3 * ATT_WIDTH
    hg_w = 4 * HG_WIDTH

    xf = x.reshape(t, d)
    memf = mem.reshape(b * mlen, d)
    for l in range(depth):
        qkv = _norm_proj(xf, ln_mix_w[l], w_in[l], 0, qkv_w, BF16, "proj_qkv", bm=1024, bn=512)
        hgg = _norm_proj(xf, ln_mix_w[l], w_in[l], qkv_w, hg_w + 2 * d, F32, "proj_hg_gates", bm=1024, bn=512)
        o_att = _dilated_attention(qkv.reshape(b, s, qkv_w), "dilated_attn")
        o_hg = _hgrn2(hgg.reshape(b, s, hg_w + 2 * d), 0, hg_lower_bounds, hg_norm_w[l], l, "hgrn2")
        merged = _merge(o_att.reshape(t, ATT_OUT_WIDTH), o_hg.reshape(t, HG_WIDTH), hgg, hg_w,
                        w_branch_a[l].astype(BF16), w_branch_b[l].astype(BF16), "merge", bm=1024, bn=512)
        xf, hc = _matmul_residual_norm(merged, w_out[l].astype(BF16), xf, ln_cross_w[l], BF16, True, "out_proj", bm=512)
        mn = _rmsnorm(memf, ln_mem_w[l], BF16, "rms_mem")
        qc = _matmul(hc, wq_cross[l].astype(BF16), 0, CROSS_WIDTH, BF16, "cross_q", bm=1024, bn=512)
        kvc = _matmul(mn, wkv_cross[l].astype(BF16), 0, 2 * CROSS_WIDTH, BF16, "cross_kv", bm=512, bn=512)
        oc = _cross_attention(qc.reshape(b, s, CROSS_WIDTH), kvc.reshape(b, mlen, 2 * CROSS_WIDTH), "cross_attn")
        xf, hf = _matmul_residual_norm(oc.reshape(t, CROSS_WIDTH), wo_cross[l].astype(BF16), xf, ln_ffn_w[l], BF16, True,
                                       "cross_o", bm=512)
        act = _swiglu(hf, w1[l], w3[l], "ffn_up", bm=1024, bn=512)
        if l + 1 < depth:
            xf = _matmul_residual(act, w2[l].astype(BF16), xf, "ffn_down", bm=512, bn=512)
        else:
            out = _matmul_residual_norm(act, w2[l].astype(BF16), xf, ln_final_w, F32, False, "ffn_down_final", bm=256,
                                        single_buffer_w=True)
    return out.reshape(b, s, d)
```

```python
import functools

import jax
import jax.numpy as jnp
from jax import lax
from jax.experimental import pallas as pl
from jax.experimental.pallas import tpu as pltpu

F32 = jnp.float32
BF16 = jnp.bfloat16

HEAD_DIM = 128
ATT_GROUPS = ((128, 1), (512, 4), (2048, 16))
ATT_HEADS_PER_GROUP = 4
N_ATT_GROUPS = len(ATT_GROUPS)
ATT_WIDTH = N_ATT_GROUPS * ATT_HEADS_PER_GROUP * HEAD_DIM
ATT_OUT_WIDTH = ATT_HEADS_PER_GROUP * HEAD_DIM
ATT_BLOCK = 128
HG_HEADS = 8
HG_KEY = 128
HG_VAL = 128
HG_WIDTH = HG_HEADS * HG_KEY
HG_CHUNK = 64
CROSS_HEADS = 4
CROSS_WIDTH = CROSS_HEADS * HEAD_DIM
RMS_EPS = 1e-6

V7X_VMEM_LIMIT_BYTES = 56 * 1024 * 1024


def _params(n_axes):
    return pltpu.CompilerParams(
        dimension_semantics=("arbitrary",) * n_axes,
        vmem_limit_bytes=V7X_VMEM_LIMIT_BYTES,
    )


def _rms_kernel(x_ref, w_ref, o_ref):
    x = x_ref[...]
    ms = jnp.mean(x * x, axis=-1, keepdims=True)
    o_ref[...] = (x * lax.rsqrt(ms + RMS_EPS) * w_ref[...]).astype(o_ref.dtype)


def _rmsnorm(x, w, out_dtype, name, bm=512):
    m, d = x.shape
    return pl.pallas_call(
        _rms_kernel,
        out_shape=jax.ShapeDtypeStruct((m, d), out_dtype),
        grid=(m // bm,),
        in_specs=[pl.BlockSpec((bm, d), lambda i: (i, 0)), pl.BlockSpec((1, d), lambda i: (0, 0))],
        out_specs=pl.BlockSpec((bm, d), lambda i: (i, 0)),
        compiler_params=_params(1),
        name=name,
    )(x, w.reshape(1, d))


def _mm_kernel(a_ref, w_ref, o_ref):
    o_ref[...] = jnp.dot(a_ref[...], w_ref[...].astype(BF16), preferred_element_type=F32).astype(o_ref.dtype)


def _matmul(a, w, col0, n, out_dtype, name, bm, bn):
    m, k = a.shape
    cb = col0 // bn
    return pl.pallas_call(
        _mm_kernel,
        out_shape=jax.ShapeDtypeStruct((m, n), out_dtype),
        grid=(m // bm, n // bn),
        in_specs=[
            pl.BlockSpec((bm, k), lambda i, j: (i, 0)),
            pl.BlockSpec((k, bn), lambda i, j: (0, j + cb)),
        ],
        out_specs=pl.BlockSpec((bm, bn), lambda i, j: (i, j)),
        compiler_params=_params(2),
        name=name,
    )(a, w)


def _mm_res_kernel(a_ref, w_ref, r_ref, o_ref):
    o_ref[...] = r_ref[...] + jnp.dot(a_ref[...], w_ref[...], preferred_element_type=F32)


def _matmul_residual(a, w, res, name, bm, bn):
    m, k = a.shape
    n = w.shape[1]
    return pl.pallas_call(
        _mm_res_kernel,
        out_shape=jax.ShapeDtypeStruct((m, n), F32),
        grid=(m // bm, n // bn),
        in_specs=[
            pl.BlockSpec((bm, k), lambda i, j: (i, 0)),
            pl.BlockSpec((k, bn), lambda i, j: (0, j)),
            pl.BlockSpec((bm, bn), lambda i, j: (i, j)),
        ],
        out_specs=pl.BlockSpec((bm, bn), lambda i, j: (i, j)),
        compiler_params=_params(2),
        name=name,
    )(a, w, res)


def _mm_res_norm_kernel(emit_sum, a_ref, w_ref, r_ref, lnw_ref, *out_refs):
    y = r_ref[...] + jnp.dot(a_ref[...], w_ref[...], preferred_element_type=F32)
    ms = jnp.mean(y * y, axis=-1, keepdims=True)
    hn = y * lax.rsqrt(ms + RMS_EPS) * lnw_ref[...]
    if emit_sum:
        out_refs[0][...] = y
    out_refs[-1][...] = hn.astype(out_refs[-1].dtype)


def _matmul_residual_norm(a, w, res, lnw, norm_dtype, emit_sum, name, bm, single_buffer_w=False):
    m, k = a.shape
    n = w.shape[1]
    row = lambda i: (i, 0)
    w_spec = (pl.BlockSpec((k, n), lambda i: (0, 0), pipeline_mode=pl.Buffered(1)) if single_buffer_w
              else pl.BlockSpec((k, n), lambda i: (0, 0)))
    norm_shape = jax.ShapeDtypeStruct((m, n), norm_dtype)
    out_shape = (jax.ShapeDtypeStruct((m, n), F32), norm_shape) if emit_sum else norm_shape
    out_specs = (pl.BlockSpec((bm, n), row), pl.BlockSpec((bm, n), row)) if emit_sum else pl.BlockSpec((bm, n), row)
    return pl.pallas_call(
        functools.partial(_mm_res_norm_kernel, emit_sum),
        out_shape=out_shape,
        grid=(m // bm,),
        in_specs=[pl.BlockSpec((bm, k), row), w_spec, pl.BlockSpec((bm, n), row), pl.BlockSpec((1, n), lambda i: (0, 0))],
        out_specs=out_specs,
        compiler_params=_params(1),
        name=name,
    )(a, w, res, lnw.reshape(1, n))


def _swiglu_kernel(a_ref, w1_ref, w3_ref, o_ref):
    a = a_ref[...]
    u = jnp.dot(a, w1_ref[...].astype(BF16), preferred_element_type=F32)
    g = jnp.dot(a, w3_ref[...].astype(BF16), preferred_element_type=F32)
    o_ref[...] = (u * jax.nn.sigmoid(u) * g).astype(o_ref.dtype)


def _swiglu(a, w1, w3, name, bm, bn):
    m, k = a.shape
    n = w1.shape[1]
    return pl.pallas_call(
        _swiglu_kernel,
        out_shape=jax.ShapeDtypeStruct((m, n), BF16),
        grid=(m // bm, n // bn),
        in_specs=[
            pl.BlockSpec((bm, k), lambda i, j: (i, 0)),
            pl.BlockSpec((k, bn), lambda i, j: (0, j)),
            pl.BlockSpec((k, bn), lambda i, j: (0, j)),
        ],
        out_specs=pl.BlockSpec((bm, bn), lambda i, j: (i, j)),
        compiler_params=_params(2),
        name=name,
    )(a, w1, w3)


def _merge_kernel(oa_ref, oh_ref, ga_ref, gb_ref, wa_ref, wb_ref, o_ref):
    pa = jnp.dot(oa_ref[...], wa_ref[...], preferred_element_type=F32)
    pb = jnp.dot(oh_ref[...], wb_ref[...], preferred_element_type=F32)
    merged = jax.nn.sigmoid(ga_ref[...]) * pa + jax.nn.sigmoid(gb_ref[...]) * pb
    o_ref[...] = merged.astype(o_ref.dtype)


def _merge(o_att, o_hg, gates, gate_col0, wa, wb, name, bm, bn):
    m = o_att.shape[0]
    d = wa.shape[1]
    ga0 = gate_col0 // bn
    gb0 = (gate_col0 + d) // bn
    return pl.pallas_call(
        _merge_kernel,
        out_shape=jax.ShapeDtypeStruct((m, d), BF16),
        grid=(m // bm, d // bn),
        in_specs=[
            pl.BlockSpec((bm, o_att.shape[1]), lambda i, j: (i, 0)),
            pl.BlockSpec((bm, o_hg.shape[1]), lambda i, j: (i, 0)),
            pl.BlockSpec((bm, bn), lambda i, j: (i, j + ga0)),
            pl.BlockSpec((bm, bn), lambda i, j: (i, j + gb0)),
            pl.BlockSpec((wa.shape[0], bn), lambda i, j: (0, j)),
            pl.BlockSpec((wb.shape[0], bn), lambda i, j: (0, j)),
        ],
        out_specs=pl.BlockSpec((bm, bn), lambda i, j: (i, j)),
        compiler_params=_params(2),
        name=name,
    )(o_att, o_hg, gates, gates, wa, wb)


def _block_scores(q, kk):
    return lax.dot_general(q, kk, (((1,), (1,)), ((), ())), preferred_element_type=F32) * (HEAD_DIM ** -0.5)


def _block_softmax_pv(s, vv, mask):
    s = jnp.where(mask, s, -jnp.inf)
    m = jnp.max(s, axis=-1, keepdims=True)
    p = jnp.exp(s - m)
    l = jnp.sum(p, axis=-1, keepdims=True)
    pn = (p * (1.0 / l)).astype(BF16)
    o = jnp.dot(pn, vv, preferred_element_type=F32)
    return o, m + jnp.log(l)


ATT_BLOCKS_IN_FLIGHT = 8


def _attn_kernel(q0, k0, v0, q1, k1, v1, q2, k2, v2, o_ref, st, og, ls):
    seq = q0.shape[0]
    blk = ATT_BLOCK
    for idx, ref in enumerate((q1, k1, v1, q2, k2, v2)):
        st[idx] = ref[...].astype(F32)

    qi = lax.broadcasted_iota(jnp.int32, (blk, 2 * blk), 0)
    kj = lax.broadcasted_iota(jnp.int32, (blk, 2 * blk), 1)
    dist = qi - kj + blk
    first_mask = (lax.broadcasted_iota(jnp.int32, (blk, blk), 0) >= lax.broadcasted_iota(jnp.int32, (blk, blk), 1))

    srcs = ((q0, k0, v0), (0, 1, 2), (3, 4, 5))

    def rows(dil, r, first_blk, count):
        start = first_blk * blk * dil + r
        if dil == 1:
            return pl.ds(start, count)
        return pl.ds(start, count, stride=dil)

    def load(g, which, rws):
        if g == 0:
            return srcs[0][which][rws, :]
        return st[srcs[g][which], rws, :].astype(BF16)

    tasks = []
    for g, (window, dil) in enumerate(ATT_GROUPS):
        band_mask = (dist >= 0) & (dist <= window // dil)
        for r in range(dil):
            for n in range(seq // dil // blk):
                if n == 0:
                    tasks.append((g, rows(dil, r, 0, blk), rows(dil, r, 0, blk), first_mask))
                else:
                    tasks.append((g, rows(dil, r, n, blk), rows(dil, r, n - 1, 2 * blk), band_mask))

    def finish(batch, scores):
        for (g, q_rows, k_rows, mask), s in zip(batch, scores):
            o, lse = _block_softmax_pv(s, load(g, 2, k_rows), mask)
            og[g, q_rows, :] = o
            ls[g, q_rows, :] = jnp.broadcast_to(lse, (blk, HEAD_DIM))

    pending = None
    for t0 in range(0, len(tasks), ATT_BLOCKS_IN_FLIGHT):
        batch = tasks[t0:t0 + ATT_BLOCKS_IN_FLIGHT]
        scores = [_block_scores(load(g, 0, q_rows), load(g, 1, k_rows)) for g, q_rows, k_rows, _ in batch]
        if pending is not None:
            finish(*pending)
        pending = (batch, scores)
    finish(*pending)

    for n in range(seq // blk):
        sl = pl.ds(n * blk, blk)
        l0, l1, l2 = ls[0, sl, :], ls[1, sl, :], ls[2, sl, :]
        mx = jnp.maximum(jnp.maximum(l0, l1), l2)
        e0, e1, e2 = jnp.exp(l0 - mx), jnp.exp(l1 - mx), jnp.exp(l2 - mx)
        inv = 1.0 / (e0 + e1 + e2)
        out = (e0 * inv) * og[0, sl, :] + (e1 * inv) * og[1, sl, :] + (e2 * inv) * og[2, sl, :]
        o_ref[sl, :] = out.astype(o_ref.dtype)


def _dilated_attention(qkv, name):
    b, s, _ = qkv.shape
    hpg = ATT_HEADS_PER_GROUP
    n_cols = ATT_WIDTH // HEAD_DIM

    def spec(which, g):
        return pl.BlockSpec((None, s, HEAD_DIM), lambda bi, h: (bi, 0, which * n_cols + g * hpg + h))

    in_specs = [spec(which, g) for g in range(N_ATT_GROUPS) for which in range(3)]
    return pl.pallas_call(
        _attn_kernel,
        out_shape=jax.ShapeDtypeStruct((b, s, ATT_OUT_WIDTH), BF16),
        grid=(b, hpg),
        in_specs=in_specs,
        out_specs=pl.BlockSpec((None, s, HEAD_DIM), lambda bi, h: (bi, 0, h)),
        scratch_shapes=[
            pltpu.VMEM((6, s, HEAD_DIM), F32),
            pltpu.VMEM((3, s, HEAD_DIM), F32),
            pltpu.VMEM((3, s, HEAD_DIM), F32),
        ],
        compiler_params=_params(2),
        name=name,
    )(*([qkv] * 9))


def _hgrn_kernel(layer, q_ref, f_ref, i_ref, g_ref, lbp_ref, nw_ref, o_ref, state, qa_s, ka_s, v_s, gs_s, rv_s):
    ts = q_ref.shape[0]
    c = HG_CHUNK
    n_chunks = ts // c

    @pl.when(pl.program_id(1) == 0)
    def _():
        state[...] = jnp.zeros_like(state)

    a = lbp_ref[...]
    e = jnp.exp(a - jnp.max(a, axis=0, keepdims=True))
    sm = e / jnp.sum(e, axis=0, keepdims=True)
    lb = jnp.sum(sm[: layer + 1], axis=0, keepdims=True)
    nw = nw_ref[...]

    ri = lax.broadcasted_iota(jnp.int32, (c, c), 0)
    ci = lax.broadcasted_iota(jnp.int32, (c, c), 1)
    causal = ri >= ci
    tri = jnp.where(causal, 1.0, 0.0).astype(BF16)

    for n in range(n_chunks):
        rows = pl.ds(n * c, c)
        f = lb + (1.0 - lb) * jax.nn.sigmoid(f_ref[rows, :])
        lf = jnp.log(f)
        hi = lf.astype(BF16)
        r1 = lf - hi.astype(F32)
        mid = r1.astype(BF16)
        lo = (r1 - mid.astype(F32)).astype(BF16)
        bcum = (jnp.dot(tri, hi, preferred_element_type=F32) + jnp.dot(tri, mid, preferred_element_type=F32)
                + jnp.dot(tri, lo, preferred_element_type=F32))
        b_mid = bcum[c // 2:c // 2 + 1, :]
        b_last = bcum[c - 1:c, :]
        qh = q_ref[rows, :]
        qa_s[rows, :] = (qh * jax.nn.sigmoid(qh) * jnp.exp(bcum - b_mid)).astype(BF16)
        ka_s[rows, :] = ((1.0 - f) * jnp.exp(b_mid - bcum)).astype(BF16)
        v_s[rows, :] = i_ref[rows, :].astype(BF16)
        gh = g_ref[rows, :]
        gs_s[rows, :] = gh * jax.nn.sigmoid(gh) * nw
        rv_s[0, n:n + 1, :] = jnp.exp(b_mid)
        rv_s[1, n:n + 1, :] = jnp.exp(b_last - b_mid)
        rv_s[2, n:n + 1, :] = jnp.exp(b_last)

    heads = [pl.ds(h * HG_KEY, HG_KEY) for h in range(HG_HEADS)]
    for n in range(n_chunks):
        rows = pl.ds(n * c, c)
        both = []
        for h, cols in enumerate(heads):
            rhs = jnp.concatenate([(state[h] * rv_s[0, n:n + 1, cols]).astype(BF16), ka_s[rows, cols]], axis=0)
            both.append(lax.dot_general(qa_s[rows, cols], rhs, (((1,), (1,)), ((), ())), preferred_element_type=F32))
        kv_t = [lax.dot_general(v_s[rows, cols], ka_s[rows, cols], (((0,), (0,)), ((), ())), preferred_element_type=F32)
                for cols in heads]
        for h, cols in enumerate(heads):
            att = jnp.where(causal, both[h][:, HG_VAL:], 0.0).astype(BF16)
            o = both[h][:, :HG_VAL] + jnp.dot(att, v_s[rows, cols], preferred_element_type=F32)
            o = o * lax.rsqrt(jnp.mean(o * o, axis=-1, keepdims=True) + RMS_EPS) * gs_s[rows, cols]
            o_ref[rows, cols] = o.astype(o_ref.dtype)
        for h, cols in enumerate(heads):
            state[h] = rv_s[2, n:n + 1, cols] * state[h] + kv_t[h] * rv_s[1, n:n + 1, cols]


def _hgrn2(proj, col0, lower_bounds_param, norm_w, layer, name, ts=512):
    b, s, _ = proj.shape
    cb = col0 // HG_WIDTH

    def spec(which):
        return pl.BlockSpec((None, ts, HG_WIDTH), lambda bi, t: (bi, t, cb + which))

    n_layers = lower_bounds_param.shape[0]
    return pl.pallas_call(
        functools.partial(_hgrn_kernel, layer),
        out_shape=jax.ShapeDtypeStruct((b, s, HG_WIDTH), BF16),
        grid=(b, s // ts),
        in_specs=[spec(0), spec(1), spec(2), spec(3),
                  pl.BlockSpec((n_layers, HG_WIDTH), lambda bi, t: (0, 0)),
                  pl.BlockSpec((1, HG_WIDTH), lambda bi, t: (0, 0))],
        out_specs=pl.BlockSpec((None, ts, HG_WIDTH), lambda bi, t: (bi, t, 0)),
        scratch_shapes=[
            pltpu.VMEM((HG_HEADS, HG_VAL, HG_KEY), F32),
            pltpu.VMEM((ts, HG_WIDTH), BF16),
            pltpu.VMEM((ts, HG_WIDTH), BF16),
            pltpu.VMEM((ts, HG_WIDTH), BF16),
            pltpu.VMEM((ts, HG_WIDTH), F32),
            pltpu.VMEM((3, ts // HG_CHUNK, HG_WIDTH), F32),
        ],
        compiler_params=_params(2),
        name=name,
    )(proj, proj, proj, proj, lower_bounds_param, jnp.tile(norm_w, HG_HEADS).reshape(1, HG_WIDTH))


def _cross_kernel(q_ref, k_ref, v_ref, o_ref):
    rb = 256
    k = k_ref[...]
    v = v_ref[...]
    for n in range(q_ref.shape[0] // rb):
        sl = pl.ds(n * rb, rb)
        s = lax.dot_general(q_ref[sl, :], k, (((1,), (1,)), ((), ())), preferred_element_type=F32) * (HEAD_DIM ** -0.5)
        m = jnp.max(s, axis=-1, keepdims=True)
        p = jnp.exp(s - m)
        l = jnp.sum(p, axis=-1, keepdims=True)
        pn = (p * (1.0 / l)).astype(BF16)
        o_ref[sl, :] = jnp.dot(pn, v, preferred_element_type=F32).astype(o_ref.dtype)


def _cross_attention(qc, kvc, name):
    b, s, _ = qc.shape
    mlen = kvc.shape[1]
    return pl.pallas_call(
        _cross_kernel,
        out_shape=jax.ShapeDtypeStruct((b, s, CROSS_WIDTH), BF16),
        grid=(b, CROSS_HEADS),
        in_specs=[
            pl.BlockSpec((None, s, HEAD_DIM), lambda bi, h: (bi, 0, h)),
            pl.BlockSpec((None, mlen, HEAD_DIM), lambda bi, h: (bi, 0, h)),
            pl.BlockSpec((None, mlen, HEAD_DIM), lambda bi, h: (bi, 0, CROSS_HEADS + h)),
        ],
        out_specs=pl.BlockSpec((None, s, HEAD_DIM), lambda bi, h: (bi, 0, h)),
        compiler_params=_params(2),
        name=name,
    )(qc, kvc, kvc)


def kernel(x, mem, ln_mix_w, w_in, hg_norm_w, hg_lower_bounds, w_branch_a, w_branch_b, w_out, ln_cross_w, ln_mem_w, wq_cross, wkv_cross, wo_cross, ln_ffn_w, w1, w3, w2, ln_final_w):
    b, s, d = x.shape
    mlen = mem.shape[1]
    t = b * s
    depth = w_in.shape[0]
    qkv_w = 3 * ATT_WIDTH
    hg_w = 4 * HG_WIDTH

    xf = x.reshape(t, d)
    memf = mem.reshape(b * mlen, d)
    for l in range(depth):
        h = _rmsnorm(xf, ln_mix_w[l], BF16, "rms_mix")
        qkv = _matmul(h, w_in[l], 0, qkv_w, BF16, "proj_qkv", bm=2048, bn=512)
        hgg = _matmul(h, w_in[l], qkv_w, hg_w + 2 * d, F32, "proj_hg_gates", bm=2048, bn=512)
        o_att = _dilated_attention(qkv.reshape(b, s, qkv_w), "dilated_attn")
        o_hg = _hgrn2(hgg.reshape(b, s, hg_w + 2 * d), 0, hg_lower_bounds, hg_norm_w[l], l, "hgrn2")
        merged = _merge(o_att.reshape(t, ATT_OUT_WIDTH), o_hg.reshape(t, HG_WIDTH), hgg, hg_w,
                        w_branch_a[l].astype(BF16), w_branch_b[l].astype(BF16), "merge", bm=1024, bn=512)
        xf, hc = _matmul_residual_norm(merged, w_out[l].astype(BF16), xf, ln_cross_w[l], BF16, True, "out_proj", bm=512)
        mn = _rmsnorm(memf, ln_mem_w[l], BF16, "rms_mem")
        qc = _matmul(hc, wq_cross[l].astype(BF16), 0, CROSS_WIDTH, BF16, "cross_q", bm=1024, bn=512)
        kvc = _matmul(mn, wkv_cross[l].astype(BF16), 0, 2 * CROSS_WIDTH, BF16, "cross_kv", bm=512, bn=512)
        oc = _cross_attention(qc.reshape(b, s, CROSS_WIDTH), kvc.reshape(b, mlen, 2 * CROSS_WIDTH), "cross_attn")
        xf, hf = _matmul_residual_norm(oc.reshape(t, CROSS_WIDTH), wo_cross[l].astype(BF16), xf, ln_ffn_w[l], BF16, True,
                                       "cross_o", bm=512)
        act = _swiglu(hf, w1[l], w3[l], "ffn_up", bm=2048, bn=512)
        if l + 1 < depth:
            xf = _matmul_residual(act, w2[l].astype(BF16), xf, "ffn_down", bm=512, bn=512)
        else:
            out = _matmul_residual_norm(act, w2[l].astype(BF16), xf, ln_final_w, F32, False, "ffn_down_final", bm=256,
                                        single_buffer_w=True)
    return out.reshape(b, s, d)
```

```python
import functools

import jax
import jax.numpy as jnp
from jax import lax
from jax.experimental import pallas as pl
from jax.experimental.pallas import tpu as pltpu

F32 = jnp.float32
BF16 = jnp.bfloat16

HEAD_DIM = 128
ATT_GROUPS = ((128, 1), (512, 4), (2048, 16))
ATT_HEADS_PER_GROUP = 4
N_ATT_GROUPS = len(ATT_GROUPS)
ATT_WIDTH = N_ATT_GROUPS * ATT_HEADS_PER_GROUP * HEAD_DIM
ATT_OUT_WIDTH = ATT_HEADS_PER_GROUP * HEAD_DIM
ATT_BLOCK = 128
HG_HEADS = 8
HG_KEY = 128
HG_VAL = 128
HG_WIDTH = HG_HEADS * HG_KEY
HG_CHUNK = 64
CROSS_HEADS = 4
CROSS_WIDTH = CROSS_HEADS * HEAD_DIM
RMS_EPS = 1e-6

V7X_VMEM_LIMIT_BYTES = 56 * 1024 * 1024


def _params(n_axes):
    return pltpu.CompilerParams(
        dimension_semantics=("arbitrary",) * n_axes,
        vmem_limit_bytes=V7X_VMEM_LIMIT_BYTES,
    )


def _rms_kernel(x_ref, w_ref, o_ref):
    x = x_ref[...]
    ms = jnp.mean(x * x, axis=-1, keepdims=True)
    o_ref[...] = (x * lax.rsqrt(ms + RMS_EPS) * w_ref[...]).astype(o_ref.dtype)


def _rmsnorm(x, w, out_dtype, name, bm=512):
    m, d = x.shape
    return pl.pallas_call(
        _rms_kernel,
        out_shape=jax.ShapeDtypeStruct((m, d), out_dtype),
        grid=(m // bm,),
        in_specs=[pl.BlockSpec((bm, d), lambda i: (i, 0)), pl.BlockSpec((1, d), lambda i: (0, 0))],
        out_specs=pl.BlockSpec((bm, d), lambda i: (i, 0)),
        compiler_params=_params(1),
        name=name,
    )(x, w.reshape(1, d))


def _mm_kernel(a_ref, w_ref, o_ref):
    o_ref[...] = jnp.dot(a_ref[...], w_ref[...].astype(BF16), preferred_element_type=F32).astype(o_ref.dtype)


def _matmul(a, w, col0, n, out_dtype, name, bm, bn):
    m, k = a.shape
    cb = col0 // bn
    return pl.pallas_call(
        _mm_kernel,
        out_shape=jax.ShapeDtypeStruct((m, n), out_dtype),
        grid=(m // bm, n // bn),
        in_specs=[
            pl.BlockSpec((bm, k), lambda i, j: (i, 0)),
            pl.BlockSpec((k, bn), lambda i, j: (0, j + cb)),
        ],
        out_specs=pl.BlockSpec((bm, bn), lambda i, j: (i, j)),
        compiler_params=_params(2),
        name=name,
    )(a, w)


def _mm_res_kernel(a_ref, w_ref, r_ref, o_ref):
    o_ref[...] = r_ref[...] + jnp.dot(a_ref[...], w_ref[...], preferred_element_type=F32)


def _matmul_residual(a, w, res, name, bm, bn):
    m, k = a.shape
    n = w.shape[1]
    return pl.pallas_call(
        _mm_res_kernel,
        out_shape=jax.ShapeDtypeStruct((m, n), F32),
        grid=(m // bm, n // bn),
        in_specs=[
            pl.BlockSpec((bm, k), lambda i, j: (i, 0)),
            pl.BlockSpec((k, bn), lambda i, j: (0, j)),
            pl.BlockSpec((bm, bn), lambda i, j: (i, j)),
        ],
        out_specs=pl.BlockSpec((bm, bn), lambda i, j: (i, j)),
        compiler_params=_params(2),
        name=name,
    )(a, w, res)


def _mm_res_norm_kernel(emit_sum, a_ref, w_ref, r_ref, lnw_ref, *out_refs):
    y = r_ref[...] + jnp.dot(a_ref[...], w_ref[...], preferred_element_type=F32)
    ms = jnp.mean(y * y, axis=-1, keepdims=True)
    hn = y * lax.rsqrt(ms + RMS_EPS) * lnw_ref[...]
    if emit_sum:
        out_refs[0][...] = y
    out_refs[-1][...] = hn.astype(out_refs[-1].dtype)


def _matmul_residual_norm(a, w, res, lnw, norm_dtype, emit_sum, name, bm, single_buffer_w=False):
    m, k = a.shape
    n = w.shape[1]
    row = lambda i: (i, 0)
    w_spec = (pl.BlockSpec((k, n), lambda i: (0, 0), pipeline_mode=pl.Buffered(1)) if single_buffer_w
              else pl.BlockSpec((k, n), lambda i: (0, 0)))
    norm_shape = jax.ShapeDtypeStruct((m, n), norm_dtype)
    out_shape = (jax.ShapeDtypeStruct((m, n), F32), norm_shape) if emit_sum else norm_shape
    out_specs = (pl.BlockSpec((bm, n), row), pl.BlockSpec((bm, n), row)) if emit_sum else pl.BlockSpec((bm, n), row)
    return pl.pallas_call(
        functools.partial(_mm_res_norm_kernel, emit_sum),
        out_shape=out_shape,
        grid=(m // bm,),
        in_specs=[pl.BlockSpec((bm, k), row), w_spec, pl.BlockSpec((bm, n), row), pl.BlockSpec((1, n), lambda i: (0, 0))],
        out_specs=out_specs,
        compiler_params=_params(1),
        name=name,
    )(a, w, res, lnw.reshape(1, n))


def _swiglu_kernel(a_ref, w1_ref, w3_ref, o_ref):
    a = a_ref[...]
    u = jnp.dot(a, w1_ref[...].astype(BF16), preferred_element_type=F32)
    g = jnp.dot(a, w3_ref[...].astype(BF16), preferred_element_type=F32)
    o_ref[...] = (u * jax.nn.sigmoid(u) * g).astype(o_ref.dtype)


def _swiglu(a, w1, w3, name, bm, bn):
    m, k = a.shape
    n = w1.shape[1]
    return pl.pallas_call(
        _swiglu_kernel,
        out_shape=jax.ShapeDtypeStruct((m, n), BF16),
        grid=(m // bm, n // bn),
        in_specs=[
            pl.BlockSpec((bm, k), lambda i, j: (i, 0)),
            pl.BlockSpec((k, bn), lambda i, j: (0, j)),
            pl.BlockSpec((k, bn), lambda i, j: (0, j)),
        ],
        out_specs=pl.BlockSpec((bm, bn), lambda i, j: (i, j)),
        compiler_params=_params(2),
        name=name,
    )(a, w1, w3)


def _merge_kernel(h_ref, oa_ref, oh_ref, wga_ref, wgb_ref, wa_ref, wb_ref, o_ref):
    h = h_ref[...]
    ga = jnp.dot(h, wga_ref[...].astype(BF16), preferred_element_type=F32)
    gb = jnp.dot(h, wgb_ref[...].astype(BF16), preferred_element_type=F32)
    pa = jnp.dot(oa_ref[...], wa_ref[...].astype(BF16), preferred_element_type=F32)
    pb = jnp.dot(oh_ref[...], wb_ref[...].astype(BF16), preferred_element_type=F32)
    o_ref[...] = (jax.nn.sigmoid(ga) * pa + jax.nn.sigmoid(gb) * pb).astype(o_ref.dtype)


def _merge(h, o_att, o_hg, w_in, gate_col0, wa, wb, name, bm, bn):
    m, k = h.shape
    d = wa.shape[1]
    ga0 = gate_col0 // bn
    gb0 = (gate_col0 + d) // bn
    return pl.pallas_call(
        _merge_kernel,
        out_shape=jax.ShapeDtypeStruct((m, d), BF16),
        grid=(m // bm, d // bn),
        in_specs=[
            pl.BlockSpec((bm, k), lambda i, j: (i, 0)),
            pl.BlockSpec((bm, o_att.shape[1]), lambda i, j: (i, 0)),
            pl.BlockSpec((bm, o_hg.shape[1]), lambda i, j: (i, 0)),
            pl.BlockSpec((k, bn), lambda i, j: (0, j + ga0)),
            pl.BlockSpec((k, bn), lambda i, j: (0, j + gb0)),
            pl.BlockSpec((wa.shape[0], bn), lambda i, j: (0, j)),
            pl.BlockSpec((wb.shape[0], bn), lambda i, j: (0, j)),
        ],
        out_specs=pl.BlockSpec((bm, bn), lambda i, j: (i, j)),
        compiler_params=_params(2),
        name=name,
    )(h, o_att, o_hg, w_in, w_in, wa, wb)


def _mid_kernel(mg_ref, x_ref, kv_ref, wout_ref, wq_ref, wo_ref, lnc_ref, lnf_ref, x2_ref, hf_ref):
    x1 = x_ref[...] + jnp.dot(mg_ref[...], wout_ref[...], preferred_element_type=F32)
    hc = (x1 * lax.rsqrt(jnp.mean(x1 * x1, axis=-1, keepdims=True) + RMS_EPS) * lnc_ref[...]).astype(BF16)
    qc = jnp.dot(hc, wq_ref[...], preferred_element_type=F32).astype(BF16)
    heads = []
    for h in range(CROSS_HEADS):
        k = kv_ref[:, h * HEAD_DIM:(h + 1) * HEAD_DIM]
        v = kv_ref[:, CROSS_WIDTH + h * HEAD_DIM:CROSS_WIDTH + (h + 1) * HEAD_DIM]
        s = lax.dot_general(qc[:, h * HEAD_DIM:(h + 1) * HEAD_DIM], k, (((1,), (1,)), ((), ())),
                            preferred_element_type=F32) * (HEAD_DIM ** -0.5)
        m = jnp.max(s, axis=-1, keepdims=True)
        p = jnp.exp(s - m)
        pn = (p * (1.0 / jnp.sum(p, axis=-1, keepdims=True))).astype(BF16)
        heads.append(jnp.dot(pn, v, preferred_element_type=F32).astype(BF16))
    oc = jnp.concatenate(heads, axis=-1)
    x2 = x1 + jnp.dot(oc, wo_ref[...], preferred_element_type=F32)
    x2_ref[...] = x2
    hf = x2 * lax.rsqrt(jnp.mean(x2 * x2, axis=-1, keepdims=True) + RMS_EPS) * lnf_ref[...]
    hf_ref[...] = hf.astype(hf_ref.dtype)


def _mid_block(merged, x, kvc, w_out, wq, wo, ln_cross, ln_ffn, name, bm):
    b, s, d = x.shape
    mlen = kvc.shape[1]
    tile = pl.BlockSpec((None, bm, d), lambda bi, i: (bi, i, 0))

    def resident(shape):
        return pl.BlockSpec(shape, lambda bi, i: (0,) * len(shape), pipeline_mode=pl.Buffered(1))

    return pl.pallas_call(
        _mid_kernel,
        out_shape=(jax.ShapeDtypeStruct((b, s, d), F32), jax.ShapeDtypeStruct((b, s, d), BF16)),
        grid=(b, s // bm),
        in_specs=[
            tile, tile,
            pl.BlockSpec((None, mlen, 2 * CROSS_WIDTH), lambda bi, i: (bi, 0, 0)),
            resident(w_out.shape), resident(wq.shape), resident(wo.shape),
            resident((1, d)), resident((1, d)),
        ],
        out_specs=(tile, tile),
        compiler_params=_params(2),
        name=name,
    )(merged, x, kvc, w_out, wq, wo, ln_cross.reshape(1, d), ln_ffn.reshape(1, d))


def _block_scores(q, kk):
    return lax.dot_general(q, kk, (((1,), (1,)), ((), ())), preferred_element_type=F32) * (HEAD_DIM ** -0.5)


def _block_softmax_pv(s, vv, mask):
    s = jnp.where(mask, s, -jnp.inf)
    m = jnp.max(s, axis=-1, keepdims=True)
    p = jnp.exp(s - m)
    l = jnp.sum(p, axis=-1, keepdims=True)
    pn = (p * (1.0 / l)).astype(BF16)
    o = jnp.dot(pn, vv, preferred_element_type=F32)
    return o, m + jnp.log(l)


ATT_BLOCKS_IN_FLIGHT = 8


def _attn_kernel(q0, k0, v0, q1, k1, v1, q2, k2, v2, o_ref, st, og, ls):
    seq = q0.shape[0]
    blk = ATT_BLOCK
    for idx, ref in enumerate((q1, k1, v1, q2, k2, v2)):
        st[idx] = ref[...].astype(F32)

    qi = lax.broadcasted_iota(jnp.int32, (blk, 2 * blk), 0)
    kj = lax.broadcasted_iota(jnp.int32, (blk, 2 * blk), 1)
    dist = qi - kj + blk
    first_mask = (lax.broadcasted_iota(jnp.int32, (blk, blk), 0) >= lax.broadcasted_iota(jnp.int32, (blk, blk), 1))

    srcs = ((q0, k0, v0), (0, 1, 2), (3, 4, 5))

    def rows(dil, r, first_blk, count):
        start = first_blk * blk * dil + r
        if dil == 1:
            return pl.ds(start, count)
        return pl.ds(start, count, stride=dil)

    def load(g, which, rws):
        if g == 0:
            return srcs[0][which][rws, :]
        return st[srcs[g][which], rws, :].astype(BF16)

    tasks = []
    for g, (window, dil) in enumerate(ATT_GROUPS):
        band_mask = (dist >= 0) & (dist <= window // dil)
        for r in range(dil):
            for n in range(seq // dil // blk):
                if n == 0:
                    tasks.append((g, rows(dil, r, 0, blk), rows(dil, r, 0, blk), first_mask))
                else:
                    tasks.append((g, rows(dil, r, n, blk), rows(dil, r, n - 1, 2 * blk), band_mask))

    def finish(batch, scores):
        for (g, q_rows, k_rows, mask), s in zip(batch, scores):
            o, lse = _block_softmax_pv(s, load(g, 2, k_rows), mask)
            og[g, q_rows, :] = o
            ls[g, q_rows, :] = jnp.broadcast_to(lse, (blk, HEAD_DIM))

    pending = None
    for t0 in range(0, len(tasks), ATT_BLOCKS_IN_FLIGHT):
        batch = tasks[t0:t0 + ATT_BLOCKS_IN_FLIGHT]
        scores = [_block_scores(load(g, 0, q_rows), load(g, 1, k_rows)) for g, q_rows, k_rows, _ in batch]
        if pending is not None:
            finish(*pending)
        pending = (batch, scores)
    finish(*pending)

    for n in range(seq // blk):
        sl = pl.ds(n * blk, blk)
        l0, l1, l2 = ls[0, sl, :], ls[1, sl, :], ls[2, sl, :]
        mx = jnp.maximum(jnp.maximum(l0, l1), l2)
        e0, e1, e2 = jnp.exp(l0 - mx), jnp.exp(l1 - mx), jnp.exp(l2 - mx)
        inv = 1.0 / (e0 + e1 + e2)
        out = (e0 * inv) * og[0, sl, :] + (e1 * inv) * og[1, sl, :] + (e2 * inv) * og[2, sl, :]
        o_ref[sl, :] = out.astype(o_ref.dtype)


def _dilated_attention(qkv, name):
    b, s, _ = qkv.shape
    hpg = ATT_HEADS_PER_GROUP
    n_cols = ATT_WIDTH // HEAD_DIM

    def spec(which, g):
        return pl.BlockSpec((None, s, HEAD_DIM), lambda bi, h: (bi, 0, which * n_cols + g * hpg + h))

    in_specs = [spec(which, g) for g in range(N_ATT_GROUPS) for which in range(3)]
    return pl.pallas_call(
        _attn_kernel,
        out_shape=jax.ShapeDtypeStruct((b, s, ATT_OUT_WIDTH), BF16),
        grid=(b, hpg),
        in_specs=in_specs,
        out_specs=pl.BlockSpec((None, s, HEAD_DIM), lambda bi, h: (bi, 0, h)),
        scratch_shapes=[
            pltpu.VMEM((6, s, HEAD_DIM), F32),
            pltpu.VMEM((3, s, HEAD_DIM), F32),
            pltpu.VMEM((3, s, HEAD_DIM), F32),
        ],
        compiler_params=_params(2),
        name=name,
    )(*([qkv] * 9))


def _hgrn_kernel(layer, q_ref, f_ref, i_ref, g_ref, lbp_ref, nw_ref, o_ref, state, qa_s, ka_s, v_s, gs_s, rv_s):
    ts = q_ref.shape[0]
    c = HG_CHUNK
    n_chunks = ts // c

    @pl.when(pl.program_id(1) == 0)
    def _():
        state[...] = jnp.zeros_like(state)

    a = lbp_ref[...]
    e = jnp.exp(a - jnp.max(a, axis=0, keepdims=True))
    sm = e / jnp.sum(e, axis=0, keepdims=True)
    lb = jnp.sum(sm[: layer + 1], axis=0, keepdims=True)
    nw = nw_ref[...]

    ri = lax.broadcasted_iota(jnp.int32, (c, c), 0)
    ci = lax.broadcasted_iota(jnp.int32, (c, c), 1)
    causal = ri >= ci
    tri = jnp.where(causal, 1.0, 0.0).astype(BF16)

    for n in range(n_chunks):
        rows = pl.ds(n * c, c)
        f = lb + (1.0 - lb) * jax.nn.sigmoid(f_ref[rows, :])
        lf = jnp.log(f)
        hi = lf.astype(BF16)
        r1 = lf - hi.astype(F32)
        mid = r1.astype(BF16)
        lo = (r1 - mid.astype(F32)).astype(BF16)
        bcum = (jnp.dot(tri, hi, preferred_element_type=F32) + jnp.dot(tri, mid, preferred_element_type=F32)
                + jnp.dot(tri, lo, preferred_element_type=F32))
        b_mid = bcum[c // 2:c // 2 + 1, :]
        b_last = bcum[c - 1:c, :]
        qh = q_ref[rows, :]
        qa_s[rows, :] = (qh * jax.nn.sigmoid(qh) * jnp.exp(bcum - b_mid)).astype(BF16)
        ka_s[rows, :] = ((1.0 - f) * jnp.exp(b_mid - bcum)).astype(BF16)
        v_s[rows, :] = i_ref[rows, :].astype(BF16)
        gh = g_ref[rows, :]
        gs_s[rows, :] = gh * jax.nn.sigmoid(gh) * nw
        rv_s[0, n:n + 1, :] = jnp.exp(b_mid)
        rv_s[1, n:n + 1, :] = jnp.exp(b_last - b_mid)
        rv_s[2, n:n + 1, :] = jnp.exp(b_last)

    heads = [pl.ds(h * HG_KEY, HG_KEY) for h in range(HG_HEADS)]
    for n in range(n_chunks):
        rows = pl.ds(n * c, c)
        both = []
        for h, cols in enumerate(heads):
            rhs = jnp.concatenate([(state[h] * rv_s[0, n:n + 1, cols]).astype(BF16), ka_s[rows, cols]], axis=0)
            both.append(lax.dot_general(qa_s[rows, cols], rhs, (((1,), (1,)), ((), ())), preferred_element_type=F32))
        kv_t = [lax.dot_general(v_s[rows, cols], ka_s[rows, cols], (((0,), (0,)), ((), ())), preferred_element_type=F32)
                for cols in heads]
        for h, cols in enumerate(heads):
            att = jnp.where(causal, both[h][:, HG_VAL:], 0.0).astype(BF16)
            o = both[h][:, :HG_VAL] + jnp.dot(att, v_s[rows, cols], preferred_element_type=F32)
            o = o * lax.rsqrt(jnp.mean(o * o, axis=-1, keepdims=True) + RMS_EPS) * gs_s[rows, cols]
            o_ref[rows, cols] = o.astype(o_ref.dtype)
        for h, cols in enumerate(heads):
            state[h] = rv_s[2, n:n + 1, cols] * state[h] + kv_t[h] * rv_s[1, n:n + 1, cols]


def _hgrn2(proj, col0, lower_bounds_param, norm_w, layer, name, ts=512):
    b, s, _ = proj.shape
    cb = col0 // HG_WIDTH

    def spec(which):
        return pl.BlockSpec((None, ts, HG_WIDTH), lambda bi, t: (bi, t, cb + which))

    n_layers = lower_bounds_param.shape[0]
    return pl.pallas_call(
        functools.partial(_hgrn_kernel, layer),
        out_shape=jax.ShapeDtypeStruct((b, s, HG_WIDTH), BF16),
        grid=(b, s // ts),
        in_specs=[spec(0), spec(1), spec(2), spec(3),
                  pl.BlockSpec((n_layers, HG_WIDTH), lambda bi, t: (0, 0)),
                  pl.BlockSpec((1, HG_WIDTH), lambda bi, t: (0, 0))],
        out_specs=pl.BlockSpec((None, ts, HG_WIDTH), lambda bi, t: (bi, t, 0)),
        scratch_shapes=[
            pltpu.VMEM((HG_HEADS, HG_VAL, HG_KEY), F32),
            pltpu.VMEM((ts, HG_WIDTH), BF16),
            pltpu.VMEM((ts, HG_WIDTH), BF16),
            pltpu.VMEM((ts, HG_WIDTH), BF16),
            pltpu.VMEM((ts, HG_WIDTH), F32),
            pltpu.VMEM((3, ts // HG_CHUNK, HG_WIDTH), F32),
        ],
        compiler_params=_params(2),
        name=name,
    )(proj, proj, proj, proj, lower_bounds_param, jnp.tile(norm_w, HG_HEADS).reshape(1, HG_WIDTH))


def kernel(x, mem, ln_mix_w, w_in, hg_norm_w, hg_lower_bounds, w_branch_a, w_branch_b, w_out, ln_cross_w, ln_mem_w, wq_cross, wkv_cross, wo_cross, ln_ffn_w, w1, w3, w2, ln_final_w):
    b, s, d = x.shape
    mlen = mem.shape[1]
    t = b * s
    depth = w_in.shape[0]
    qkv_w = 3 * ATT_WIDTH
    hg_w = 4 * HG_WIDTH

    xf = x.reshape(t, d)
    memf = mem.reshape(b * mlen, d)
    for l in range(depth):
        h = _rmsnorm(xf, ln_mix_w[l], BF16, "rms_mix")
        qkv = _matmul(h, w_in[l], 0, qkv_w, BF16, "proj_qkv", bm=2048, bn=512)
        hg = _matmul(h, w_in[l], qkv_w, hg_w, F32, "proj_hg", bm=2048, bn=512)
        o_att = _dilated_attention(qkv.reshape(b, s, qkv_w), "dilated_attn")
        o_hg = _hgrn2(hg.reshape(b, s, hg_w), 0, hg_lower_bounds, hg_norm_w[l], l, "hgrn2")
        merged = _merge(h, o_att.reshape(t, ATT_OUT_WIDTH), o_hg.reshape(t, HG_WIDTH), w_in[l], qkv_w + hg_w,
                        w_branch_a[l], w_branch_b[l], "gate_merge", bm=1024, bn=512)
        mn = _rmsnorm(memf, ln_mem_w[l], BF16, "rms_mem")
        kvc = _matmul(mn, wkv_cross[l], 0, 2 * CROSS_WIDTH, BF16, "cross_kv", bm=512, bn=512)
        x2, hf = _mid_block(merged.reshape(b, s, d), xf.reshape(b, s, d), kvc.reshape(b, mlen, 2 * CROSS_WIDTH),
                            w_out[l].astype(BF16), wq_cross[l].astype(BF16), wo_cross[l].astype(BF16),
                            ln_cross_w[l], ln_ffn_w[l], "out_proj_cross", bm=512)
        xf, hf = x2.reshape(t, d), hf.reshape(t, d)
        act = _swiglu(hf, w1[l], w3[l], "ffn_up", bm=2048, bn=512)
        if l + 1 < depth:
            xf = _matmul_residual(act, w2[l].astype(BF16), xf, "ffn_down", bm=512, bn=512)
        else:
            out = _matmul_residual_norm(act, w2[l].astype(BF16), xf, ln_final_w, F32, False, "ffn_down_final", bm=256,
                                        single_buffer_w=True)
    return out.reshape(b, s, d)
```

```python
import functools

import jax
import jax.numpy as jnp
from jax import lax
from jax.experimental import pallas as pl
from jax.experimental.pallas import tpu as pltpu

F32 = jnp.float32
BF16 = jnp.bfloat16

HEAD_DIM = 128
ATT_GROUPS = ((128, 1), (512, 4), (2048, 16))
ATT_HEADS_PER_GROUP = 4
N_ATT_GROUPS = len(ATT_GROUPS)
ATT_WIDTH = N_ATT_GROUPS * ATT_HEADS_PER_GROUP * HEAD_DIM
ATT_OUT_WIDTH = ATT_HEADS_PER_GROUP * HEAD_DIM
ATT_BLOCK = 128
HG_HEADS = 8
HG_KEY = 128
HG_VAL = 128
HG_WIDTH = HG_HEADS * HG_KEY
HG_CHUNK = 64
CROSS_HEADS = 4
CROSS_WIDTH = CROSS_HEADS * HEAD_DIM
RMS_EPS = 1e-6

V7X_VMEM_LIMIT_BYTES = 56 * 1024 * 1024
MXU_ROW_TILE = 256
MXU_COL_TILE = 512


def _params(n_axes):
    return pltpu.CompilerParams(
        dimension_semantics=("arbitrary",) * n_axes,
        vmem_limit_bytes=V7X_VMEM_LIMIT_BYTES,
    )


def _rms_kernel(x_ref, w_ref, o_ref):
    x = x_ref[...]
    ms = jnp.mean(x * x, axis=-1, keepdims=True)
    o_ref[...] = (x * lax.rsqrt(ms + RMS_EPS) * w_ref[...]).astype(o_ref.dtype)


def _rmsnorm(x, w, out_dtype, name, bm=512):
    m, d = x.shape
    return pl.pallas_call(
        _rms_kernel,
        out_shape=jax.ShapeDtypeStruct((m, d), out_dtype),
        grid=(m // bm,),
        in_specs=[pl.BlockSpec((bm, d), lambda i: (i, 0)), pl.BlockSpec((1, d), lambda i: (0, 0))],
        out_specs=pl.BlockSpec((bm, d), lambda i: (i, 0)),
        compiler_params=_params(1),
        name=name,
    )(x, w.reshape(1, d))


def _row_tiles(n_rows):
    tile = min(MXU_ROW_TILE, n_rows)
    return [pl.ds(r * tile, tile) for r in range(n_rows // tile)]


def _col_tiles(n_cols):
    tile = min(MXU_COL_TILE, n_cols)
    return [pl.ds(c * tile, tile) for c in range(n_cols // tile)]


def _mm_kernel(a_ref, w_ref, o_ref, w_s):
    w_s[...] = w_ref[...].astype(BF16)
    for rows in _row_tiles(a_ref.shape[0]):
        o_ref[rows, :] = jnp.dot(a_ref[rows, :], w_s[...], preferred_element_type=F32).astype(o_ref.dtype)


def _matmul(a, w, col0, n, out_dtype, name, bm, bn):
    m, k = a.shape
    cb = col0 // bn
    return pl.pallas_call(
        _mm_kernel,
        out_shape=jax.ShapeDtypeStruct((m, n), out_dtype),
        grid=(m // bm, n // bn),
        in_specs=[
            pl.BlockSpec((bm, k), lambda i, j: (i, 0)),
            pl.BlockSpec((k, bn), lambda i, j: (0, j + cb)),
        ],
        out_specs=pl.BlockSpec((bm, bn), lambda i, j: (i, j)),
        scratch_shapes=[pltpu.VMEM((k, bn), BF16)],
        compiler_params=_params(2),
        name=name,
    )(a, w)


def _mm_res_kernel(a_ref, w_ref, r_ref, o_ref):
    o_ref[...] = r_ref[...] + jnp.dot(a_ref[...], w_ref[...], preferred_element_type=F32)


def _matmul_residual(a, w, res, name, bm, bn):
    m, k = a.shape
    n = w.shape[1]
    return pl.pallas_call(
        _mm_res_kernel,
        out_shape=jax.ShapeDtypeStruct((m, n), F32),
        grid=(m // bm, n // bn),
        in_specs=[
            pl.BlockSpec((bm, k), lambda i, j: (i, 0)),
            pl.BlockSpec((k, bn), lambda i, j: (0, j)),
            pl.BlockSpec((bm, bn), lambda i, j: (i, j)),
        ],
        out_specs=pl.BlockSpec((bm, bn), lambda i, j: (i, j)),
        compiler_params=_params(2),
        name=name,
    )(a, w, res)


def _mm_res_norm_kernel(emit_sum, a_ref, w_ref, r_ref, lnw_ref, *out_refs):
    n = w_ref.shape[1]
    for rows in _row_tiles(a_ref.shape[0]):
        a = a_ref[rows, :]
        ys = [r_ref[rows, cols] + jnp.dot(a, w_ref[:, cols], preferred_element_type=F32) for cols in _col_tiles(n)]
        ssq = sum(jnp.sum(y * y, axis=-1, keepdims=True) for y in ys)
        scale = lax.rsqrt(ssq * (1.0 / n) + RMS_EPS)
        for cols, y in zip(_col_tiles(n), ys):
            if emit_sum:
                out_refs[0][rows, cols] = y
            out_refs[-1][rows, cols] = (y * scale * lnw_ref[:, cols]).astype(out_refs[-1].dtype)


def _matmul_residual_norm(a, w, res, lnw, norm_dtype, emit_sum, name, bm, single_buffer_w=False):
    m, k = a.shape
    n = w.shape[1]
    row = lambda i: (i, 0)
    w_spec = (pl.BlockSpec((k, n), lambda i: (0, 0), pipeline_mode=pl.Buffered(1)) if single_buffer_w
              else pl.BlockSpec((k, n), lambda i: (0, 0)))
    norm_shape = jax.ShapeDtypeStruct((m, n), norm_dtype)
    out_shape = (jax.ShapeDtypeStruct((m, n), F32), norm_shape) if emit_sum else norm_shape
    out_specs = (pl.BlockSpec((bm, n), row), pl.BlockSpec((bm, n), row)) if emit_sum else pl.BlockSpec((bm, n), row)
    return pl.pallas_call(
        functools.partial(_mm_res_norm_kernel, emit_sum),
        out_shape=out_shape,
        grid=(m // bm,),
        in_specs=[pl.BlockSpec((bm, k), row), w_spec, pl.BlockSpec((bm, n), row), pl.BlockSpec((1, n), lambda i: (0, 0))],
        out_specs=out_specs,
        compiler_params=_params(1),
        name=name,
    )(a, w, res, lnw.reshape(1, n))


def _swiglu_kernel(a_ref, w1_ref, w3_ref, o_ref, w1_s, w3_s):
    w1_s[...] = w1_ref[...].astype(BF16)
    w3_s[...] = w3_ref[...].astype(BF16)
    for r in range(a_ref.shape[0] // MXU_ROW_TILE):
        rows = pl.ds(r * MXU_ROW_TILE, MXU_ROW_TILE)
        a = a_ref[rows, :]
        u = jnp.dot(a, w1_s[...], preferred_element_type=F32)
        g = jnp.dot(a, w3_s[...], preferred_element_type=F32)
        o_ref[rows, :] = (u * jax.nn.sigmoid(u) * g).astype(o_ref.dtype)


def _swiglu(a, w1, w3, name, bm, bn):
    m, k = a.shape
    n = w1.shape[1]
    return pl.pallas_call(
        _swiglu_kernel,
        scratch_shapes=[pltpu.VMEM((k, bn), BF16), pltpu.VMEM((k, bn), BF16)],
        out_shape=jax.ShapeDtypeStruct((m, n), BF16),
        grid=(m // bm, n // bn),
        in_specs=[
            pl.BlockSpec((bm, k), lambda i, j: (i, 0)),
            pl.BlockSpec((k, bn), lambda i, j: (0, j)),
            pl.BlockSpec((k, bn), lambda i, j: (0, j)),
        ],
        out_specs=pl.BlockSpec((bm, bn), lambda i, j: (i, j)),
        compiler_params=_params(2),
        name=name,
    )(a, w1, w3)


def _merge_kernel(h_ref, oa_ref, oh_ref, wga_ref, wgb_ref, wa_ref, wb_ref, o_ref, wga_s, wgb_s, wa_s, wb_s):
    for src, dst in ((wga_ref, wga_s), (wgb_ref, wgb_s), (wa_ref, wa_s), (wb_ref, wb_s)):
        dst[...] = src[...].astype(BF16)
    for rows in _row_tiles(h_ref.shape[0]):
        h = h_ref[rows, :]
        ga = jnp.dot(h, wga_s[...], preferred_element_type=F32)
        gb = jnp.dot(h, wgb_s[...], preferred_element_type=F32)
        pa = jnp.dot(oa_ref[rows, :], wa_s[...], preferred_element_type=F32)
        pb = jnp.dot(oh_ref[rows, :], wb_s[...], preferred_element_type=F32)
        o_ref[rows, :] = (jax.nn.sigmoid(ga) * pa + jax.nn.sigmoid(gb) * pb).astype(o_ref.dtype)


def _merge(h, o_att, o_hg, w_in, gate_col0, wa, wb, name, bm, bn):
    m, k = h.shape
    d = wa.shape[1]
    ga0 = gate_col0 // bn
    gb0 = (gate_col0 + d) // bn
    return pl.pallas_call(
        _merge_kernel,
        out_shape=jax.ShapeDtypeStruct((m, d), BF16),
        grid=(m // bm, d // bn),
        in_specs=[
            pl.BlockSpec((bm, k), lambda i, j: (i, 0)),
            pl.BlockSpec((bm, o_att.shape[1]), lambda i, j: (i, 0)),
            pl.BlockSpec((bm, o_hg.shape[1]), lambda i, j: (i, 0)),
            pl.BlockSpec((k, bn), lambda i, j: (0, j + ga0)),
            pl.BlockSpec((k, bn), lambda i, j: (0, j + gb0)),
            pl.BlockSpec((wa.shape[0], bn), lambda i, j: (0, j)),
            pl.BlockSpec((wb.shape[0], bn), lambda i, j: (0, j)),
        ],
        out_specs=pl.BlockSpec((bm, bn), lambda i, j: (i, j)),
        scratch_shapes=[pltpu.VMEM((k, bn), BF16), pltpu.VMEM((k, bn), BF16),
                        pltpu.VMEM((wa.shape[0], bn), BF16), pltpu.VMEM((wb.shape[0], bn), BF16)],
        compiler_params=_params(2),
        name=name,
    )(h, o_att, o_hg, w_in, w_in, wa, wb)


def _mid_kernel(mg_ref, x_ref, kv_ref, wout_ref, wq_ref, wo_ref, lnc_ref, lnf_ref, x2_ref, hf_ref):
    d = x_ref.shape[1]
    row_tiles = _row_tiles(x_ref.shape[0])
    col_tiles = _col_tiles(d)

    def residual_norm(rows, lhs, w_ref, base_ref, ln_ref):
        ys = [base_ref[rows, cols] + jnp.dot(lhs, w_ref[:, cols], preferred_element_type=F32) for cols in col_tiles]
        ssq = sum(jnp.sum(y * y, axis=-1, keepdims=True) for y in ys)
        scale = lax.rsqrt(ssq * (1.0 / d) + RMS_EPS)
        for cols, y in zip(col_tiles, ys):
            x2_ref[rows, cols] = y
            hf_ref[rows, cols] = (y * scale * ln_ref[:, cols]).astype(hf_ref.dtype)

    for rows in row_tiles:
        residual_norm(rows, mg_ref[rows, :], wout_ref, x_ref, lnc_ref)
    qcs = [jnp.dot(hf_ref[rows, :], wq_ref[...], preferred_element_type=F32).astype(BF16) for rows in row_tiles]
    scores = [[lax.dot_general(qc[:, h * HEAD_DIM:(h + 1) * HEAD_DIM], kv_ref[:, h * HEAD_DIM:(h + 1) * HEAD_DIM],
                               (((1,), (1,)), ((), ())), preferred_element_type=F32) * (HEAD_DIM ** -0.5)
               for h in range(CROSS_HEADS)] for qc in qcs]
    ocs = []
    for per_head in scores:
        outs = []
        for h, s in enumerate(per_head):
            m = jnp.max(s, axis=-1, keepdims=True)
            p = jnp.exp(s - m)
            pn = (p * (1.0 / jnp.sum(p, axis=-1, keepdims=True))).astype(BF16)
            v = kv_ref[:, CROSS_WIDTH + h * HEAD_DIM:CROSS_WIDTH + (h + 1) * HEAD_DIM]
            outs.append(jnp.dot(pn, v, preferred_element_type=F32).astype(BF16))
        ocs.append(jnp.concatenate(outs, axis=-1))
    for rows, oc in zip(row_tiles, ocs):
        residual_norm(rows, oc, wo_ref, x2_ref, lnf_ref)


def _mid_block(merged, x, kvc, w_out, wq, wo, ln_cross, ln_ffn, name, bm):
    b, s, d = x.shape
    mlen = kvc.shape[1]
    tile = pl.BlockSpec((None, bm, d), lambda bi, i: (bi, i, 0))

    def resident(shape):
        return pl.BlockSpec(shape, lambda bi, i: (0,) * len(shape), pipeline_mode=pl.Buffered(1))

    return pl.pallas_call(
        _mid_kernel,
        out_shape=(jax.ShapeDtypeStruct((b, s, d), F32), jax.ShapeDtypeStruct((b, s, d), BF16)),
        grid=(b, s // bm),
        in_specs=[
            tile, tile,
            pl.BlockSpec((None, mlen, 2 * CROSS_WIDTH), lambda bi, i: (bi, 0, 0)),
            resident(w_out.shape), resident(wq.shape), resident(wo.shape),
            resident((1, d)), resident((1, d)),
        ],
        out_specs=(tile, tile),
        compiler_params=_params(2),
        name=name,
    )(merged, x, kvc, w_out, wq, wo, ln_cross.reshape(1, d), ln_ffn.reshape(1, d))


def _block_scores(q, kk):
    return lax.dot_general(q, kk, (((1,), (1,)), ((), ())), preferred_element_type=F32) * (HEAD_DIM ** -0.5)


def _block_softmax_pv(s, vv, mask):
    s = jnp.where(mask, s, -jnp.inf)
    m = jnp.max(s, axis=-1, keepdims=True)
    p = jnp.exp(s - m)
    l = jnp.sum(p, axis=-1, keepdims=True)
    pn = (p * (1.0 / l)).astype(BF16)
    o = jnp.dot(pn, vv, preferred_element_type=F32)
    return o, m + jnp.log(l)


ATT_BLOCKS_IN_FLIGHT = 8


def _attn_kernel(q0, k0, v0, q1, k1, v1, q2, k2, v2, o_ref, st, og, ls):
    seq = q0.shape[0]
    blk = ATT_BLOCK
    for idx, ref in enumerate((q1, k1, v1, q2, k2, v2)):
        st[idx] = ref[...].astype(F32)

    qi = lax.broadcasted_iota(jnp.int32, (blk, 2 * blk), 0)
    kj = lax.broadcasted_iota(jnp.int32, (blk, 2 * blk), 1)
    dist = qi - kj + blk
    first_mask = (lax.broadcasted_iota(jnp.int32, (blk, blk), 0) >= lax.broadcasted_iota(jnp.int32, (blk, blk), 1))

    srcs = ((q0, k0, v0), (0, 1, 2), (3, 4, 5))

    def rows(dil, r, first_blk, count):
        start = first_blk * blk * dil + r
        if dil == 1:
            return pl.ds(start, count)
        return pl.ds(start, count, stride=dil)

    def load(g, which, rws):
        if g == 0:
            return srcs[0][which][rws, :]
        return st[srcs[g][which], rws, :].astype(BF16)

    tasks = []
    for g, (window, dil) in enumerate(ATT_GROUPS):
        band_mask = (dist >= 0) & (dist <= window // dil)
        for r in range(dil):
            for n in range(seq // dil // blk):
                if n == 0:
                    tasks.append((g, rows(dil, r, 0, blk), rows(dil, r, 0, blk), first_mask))
                else:
                    tasks.append((g, rows(dil, r, n, blk), rows(dil, r, n - 1, 2 * blk), band_mask))

    def finish(batch, scores):
        for (g, q_rows, k_rows, mask), s in zip(batch, scores):
            o, lse = _block_softmax_pv(s, load(g, 2, k_rows), mask)
            og[g, q_rows, :] = o
            ls[g, q_rows, :] = jnp.broadcast_to(lse, (blk, HEAD_DIM))

    pending = None
    for t0 in range(0, len(tasks), ATT_BLOCKS_IN_FLIGHT):
        batch = tasks[t0:t0 + ATT_BLOCKS_IN_FLIGHT]
        scores = [_block_scores(load(g, 0, q_rows), load(g, 1, k_rows)) for g, q_rows, k_rows, _ in batch]
        if pending is not None:
            finish(*pending)
        pending = (batch, scores)
    finish(*pending)

    for n in range(seq // blk):
        sl = pl.ds(n * blk, blk)
        l0, l1, l2 = ls[0, sl, :], ls[1, sl, :], ls[2, sl, :]
        mx = jnp.maximum(jnp.maximum(l0, l1), l2)
        e0, e1, e2 = jnp.exp(l0 - mx), jnp.exp(l1 - mx), jnp.exp(l2 - mx)
        inv = 1.0 / (e0 + e1 + e2)
        out = (e0 * inv) * og[0, sl, :] + (e1 * inv) * og[1, sl, :] + (e2 * inv) * og[2, sl, :]
        o_ref[sl, :] = out.astype(o_ref.dtype)


def _dilated_attention(qkv, name):
    b, s, _ = qkv.shape
    hpg = ATT_HEADS_PER_GROUP
    n_cols = ATT_WIDTH // HEAD_DIM

    def spec(which, g):
        return pl.BlockSpec((None, s, HEAD_DIM), lambda bi, h: (bi, 0, which * n_cols + g * hpg + h))

    in_specs = [spec(which, g) for g in range(N_ATT_GROUPS) for which in range(3)]
    return pl.pallas_call(
        _attn_kernel,
        out_shape=jax.ShapeDtypeStruct((b, s, ATT_OUT_WIDTH), BF16),
        grid=(b, hpg),
        in_specs=in_specs,
        out_specs=pl.BlockSpec((None, s, HEAD_DIM), lambda bi, h: (bi, 0, h)),
        scratch_shapes=[
            pltpu.VMEM((6, s, HEAD_DIM), F32),
            pltpu.VMEM((3, s, HEAD_DIM), F32),
            pltpu.VMEM((3, s, HEAD_DIM), F32),
        ],
        compiler_params=_params(2),
        name=name,
    )(*([qkv] * 9))


def _hgrn_kernel(layer, q_ref, f_ref, i_ref, g_ref, lbp_ref, nw_ref, o_ref, state, qa_s, ka_s, v_s, gs_s, rv_s):
    ts = q_ref.shape[0]
    c = HG_CHUNK
    n_chunks = ts // c

    @pl.when(pl.program_id(1) == 0)
    def _():
        state[...] = jnp.zeros_like(state)

    a = lbp_ref[...]
    e = jnp.exp(a - jnp.max(a, axis=0, keepdims=True))
    sm = e / jnp.sum(e, axis=0, keepdims=True)
    lb = jnp.sum(sm[: layer + 1], axis=0, keepdims=True)
    nw = nw_ref[...]

    ri = lax.broadcasted_iota(jnp.int32, (c, c), 0)
    ci = lax.broadcasted_iota(jnp.int32, (c, c), 1)
    causal = ri >= ci
    tri = jnp.where(causal, 1.0, 0.0).astype(BF16)

    for n in range(n_chunks):
        rows = pl.ds(n * c, c)
        f = lb + (1.0 - lb) * jax.nn.sigmoid(f_ref[rows, :])
        lf = jnp.log(f)
        hi = lf.astype(BF16)
        r1 = lf - hi.astype(F32)
        mid = r1.astype(BF16)
        lo = (r1 - mid.astype(F32)).astype(BF16)
        bcum = (jnp.dot(tri, hi, preferred_element_type=F32) + jnp.dot(tri, mid, preferred_element_type=F32)
                + jnp.dot(tri, lo, preferred_element_type=F32))
        b_mid = bcum[c // 2:c // 2 + 1, :]
        b_last = bcum[c - 1:c, :]
        qh = q_ref[rows, :]
        qa_s[rows, :] = (qh * jax.nn.sigmoid(qh) * jnp.exp(bcum - b_mid)).astype(BF16)
        ka_s[rows, :] = ((1.0 - f) * jnp.exp(b_mid - bcum)).astype(BF16)
        v_s[rows, :] = i_ref[rows, :].astype(BF16)
        gh = g_ref[rows, :]
        gs_s[rows, :] = gh * jax.nn.sigmoid(gh) * nw
        rv_s[0, n:n + 1, :] = jnp.exp(b_mid)
        rv_s[1, n:n + 1, :] = jnp.exp(b_last - b_mid)
        rv_s[2, n:n + 1, :] = jnp.exp(b_last)

    heads = [pl.ds(h * HG_KEY, HG_KEY) for h in range(HG_HEADS)]
    for n in range(n_chunks):
        rows = pl.ds(n * c, c)
        both = []
        for h, cols in enumerate(heads):
            rhs = jnp.concatenate([(state[h] * rv_s[0, n:n + 1, cols]).astype(BF16), ka_s[rows, cols]], axis=0)
            both.append(lax.dot_general(qa_s[rows, cols], rhs, (((1,), (1,)), ((), ())), preferred_element_type=F32))
        kv_t = [lax.dot_general(v_s[rows, cols], ka_s[rows, cols], (((0,), (0,)), ((), ())), preferred_element_type=F32)
                for cols in heads]
        for h, cols in enumerate(heads):
            att = jnp.where(causal, both[h][:, HG_VAL:], 0.0).astype(BF16)
            o = both[h][:, :HG_VAL] + jnp.dot(att, v_s[rows, cols], preferred_element_type=F32)
            o = o * lax.rsqrt(jnp.mean(o * o, axis=-1, keepdims=True) + RMS_EPS) * gs_s[rows, cols]
            o_ref[rows, cols] = o.astype(o_ref.dtype)
        for h, cols in enumerate(heads):
            state[h] = rv_s[2, n:n + 1, cols] * state[h] + kv_t[h] * rv_s[1, n:n + 1, cols]


def _hgrn2(proj, col0, lower_bounds_param, norm_w, layer, name, ts=512):
    b, s, _ = proj.shape
    cb = col0 // HG_WIDTH

    def spec(which):
        return pl.BlockSpec((None, ts, HG_WIDTH), lambda bi, t: (bi, t, cb + which))

    n_layers = lower_bounds_param.shape[0]
    return pl.pallas_call(
        functools.partial(_hgrn_kernel, layer),
        out_shape=jax.ShapeDtypeStruct((b, s, HG_WIDTH), BF16),
        grid=(b, s // ts),
        in_specs=[spec(0), spec(1), spec(2), spec(3),
                  pl.BlockSpec((n_layers, HG_WIDTH), lambda bi, t: (0, 0)),
                  pl.BlockSpec((1, HG_WIDTH), lambda bi, t: (0, 0))],
        out_specs=pl.BlockSpec((None, ts, HG_WIDTH), lambda bi, t: (bi, t, 0)),
        scratch_shapes=[
            pltpu.VMEM((HG_HEADS, HG_VAL, HG_KEY), F32),
            pltpu.VMEM((ts, HG_WIDTH), BF16),
            pltpu.VMEM((ts, HG_WIDTH), BF16),
            pltpu.VMEM((ts, HG_WIDTH), BF16),
            pltpu.VMEM((ts, HG_WIDTH), F32),
            pltpu.VMEM((3, ts // HG_CHUNK, HG_WIDTH), F32),
        ],
        compiler_params=_params(2),
        name=name,
    )(proj, proj, proj, proj, lower_bounds_param, jnp.tile(norm_w, HG_HEADS).reshape(1, HG_WIDTH))


def kernel(x, mem, ln_mix_w, w_in, hg_norm_w, hg_lower_bounds, w_branch_a, w_branch_b, w_out, ln_cross_w, ln_mem_w, wq_cross, wkv_cross, wo_cross, ln_ffn_w, w1, w3, w2, ln_final_w):
    b, s, d = x.shape
    mlen = mem.shape[1]
    t = b * s
    depth = w_in.shape[0]
    qkv_w = 3 * ATT_WIDTH
    hg_w = 4 * HG_WIDTH

    xf = x.reshape(t, d)
    memf = mem.reshape(b * mlen, d)
    for l in range(depth):
        h = _rmsnorm(xf, ln_mix_w[l], BF16, "rms_mix")
        qkv = _matmul(h, w_in[l], 0, qkv_w, BF16, "proj_qkv", bm=2048, bn=512)
        hg = _matmul(h, w_in[l], qkv_w, hg_w, F32, "proj_hg", bm=2048, bn=512)
        o_att = _dilated_attention(qkv.reshape(b, s, qkv_w), "dilated_attn")
        o_hg = _hgrn2(hg.reshape(b, s, hg_w), 0, hg_lower_bounds, hg_norm_w[l], l, "hgrn2")
        merged = _merge(h, o_att.reshape(t, ATT_OUT_WIDTH), o_hg.reshape(t, HG_WIDTH), w_in[l], qkv_w + hg_w,
                        w_branch_a[l], w_branch_b[l], "gate_merge", bm=1024, bn=512)
        mn = _rmsnorm(memf, ln_mem_w[l], BF16, "rms_mem")
        kvc = _matmul(mn, wkv_cross[l], 0, 2 * CROSS_WIDTH, BF16, "cross_kv", bm=512, bn=512)
        x2, hf = _mid_block(merged.reshape(b, s, d), xf.reshape(b, s, d), kvc.reshape(b, mlen, 2 * CROSS_WIDTH),
                            w_out[l].astype(BF16), wq_cross[l].astype(BF16), wo_cross[l].astype(BF16),
                            ln_cross_w[l], ln_ffn_w[l], "out_proj_cross", bm=512)
        xf, hf = x2.reshape(t, d), hf.reshape(t, d)
        act = _swiglu(hf, w1[l], w3[l], "ffn_up", bm=2048, bn=512)
        if l + 1 < depth:
            xf = _matmul_residual(act, w2[l].astype(BF16), xf, "ffn_down", bm=512, bn=512)
        else:
            out = _matmul_residual_norm(act, w2[l].astype(BF16), xf, ln_final_w, F32, False, "ffn_down_final", bm=256,
                                        single_buffer_w=True)
    return out.reshape(b, s, d)
```

```python
import functools

import jax
import jax.numpy as jnp
from jax import lax
from jax.experimental import pallas as pl
from jax.experimental.pallas import tpu as pltpu

F32 = jnp.float32
BF16 = jnp.bfloat16

HEAD_DIM = 128
ATT_GROUPS = ((128, 1), (512, 4), (2048, 16))
ATT_HEADS_PER_GROUP = 4
N_ATT_GROUPS = len(ATT_GROUPS)
ATT_WIDTH = N_ATT_GROUPS * ATT_HEADS_PER_GROUP * HEAD_DIM
ATT_OUT_WIDTH = ATT_HEADS_PER_GROUP * HEAD_DIM
ATT_BLOCK = 128
HG_HEADS = 8
HG_KEY = 128
HG_VAL = 128
HG_WIDTH = HG_HEADS * HG_KEY
HG_CHUNK = 64
CROSS_HEADS = 4
CROSS_WIDTH = CROSS_HEADS * HEAD_DIM
RMS_EPS = 1e-6
LOG2_E = 1.4426950408889634

V7X_VMEM_LIMIT_BYTES = 56 * 1024 * 1024
MXU_ROW_TILE = 256
MXU_COL_TILE = 512


def _sigmoid(x):
    return 0.5 * jnp.tanh(0.5 * x) + 0.5


def _silu(x):
    hx = 0.5 * x
    return hx * jnp.tanh(hx) + hx


def _params(n_axes):
    return pltpu.CompilerParams(
        dimension_semantics=("arbitrary",) * n_axes,
        vmem_limit_bytes=V7X_VMEM_LIMIT_BYTES,
    )


def _rms_kernel(x_ref, w_ref, o_ref):
    x = x_ref[...]
    ms = jnp.mean(x * x, axis=-1, keepdims=True)
    o_ref[...] = (x * lax.rsqrt(ms + RMS_EPS) * w_ref[...]).astype(o_ref.dtype)


def _rmsnorm(x, w, out_dtype, name, bm=512):
    m, d = x.shape
    return pl.pallas_call(
        _rms_kernel,
        out_shape=jax.ShapeDtypeStruct((m, d), out_dtype),
        grid=(m // bm,),
        in_specs=[pl.BlockSpec((bm, d), lambda i: (i, 0)), pl.BlockSpec((1, d), lambda i: (0, 0))],
        out_specs=pl.BlockSpec((bm, d), lambda i: (i, 0)),
        compiler_params=_params(1),
        name=name,
    )(x, w.reshape(1, d))


def _row_tiles(n_rows):
    tile = min(MXU_ROW_TILE, n_rows)
    return [pl.ds(r * tile, tile) for r in range(n_rows // tile)]


def _col_tiles(n_cols):
    tile = min(MXU_COL_TILE, n_cols)
    return [pl.ds(c * tile, tile) for c in range(n_cols // tile)]


def _mm_kernel(a_ref, w_ref, o_ref, w_s):
    w_s[...] = w_ref[...].astype(BF16)
    for rows in _row_tiles(a_ref.shape[0]):
        o_ref[rows, :] = jnp.dot(a_ref[rows, :], w_s[...], preferred_element_type=F32).astype(o_ref.dtype)


def _matmul(a, w, col0, n, out_dtype, name, bm, bn):
    m, k = a.shape
    cb = col0 // bn
    return pl.pallas_call(
        _mm_kernel,
        out_shape=jax.ShapeDtypeStruct((m, n), out_dtype),
        grid=(m // bm, n // bn),
        in_specs=[
            pl.BlockSpec((bm, k), lambda i, j: (i, 0)),
            pl.BlockSpec((k, bn), lambda i, j: (0, j + cb)),
        ],
        out_specs=pl.BlockSpec((bm, bn), lambda i, j: (i, j)),
        scratch_shapes=[pltpu.VMEM((k, bn), BF16)],
        compiler_params=_params(2),
        name=name,
    )(a, w)


def _mm_res_kernel(a_ref, w_ref, r_ref, o_ref):
    o_ref[...] = r_ref[...] + jnp.dot(a_ref[...], w_ref[...], preferred_element_type=F32)


def _matmul_residual(a, w, res, name, bm, bn):
    m, k = a.shape
    n = w.shape[1]
    return pl.pallas_call(
        _mm_res_kernel,
        out_shape=jax.ShapeDtypeStruct((m, n), F32),
        grid=(m // bm, n // bn),
        in_specs=[
            pl.BlockSpec((bm, k), lambda i, j: (i, 0)),
            pl.BlockSpec((k, bn), lambda i, j: (0, j)),
            pl.BlockSpec((bm, bn), lambda i, j: (i, j)),
        ],
        out_specs=pl.BlockSpec((bm, bn), lambda i, j: (i, j)),
        compiler_params=_params(2),
        name=name,
    )(a, w, res)


def _mm_res_norm_kernel(emit_sum, a_ref, w_ref, r_ref, lnw_ref, *out_refs):
    n = w_ref.shape[1]
    for rows in _row_tiles(a_ref.shape[0]):
        a = a_ref[rows, :]
        ys = [r_ref[rows, cols] + jnp.dot(a, w_ref[:, cols], preferred_element_type=F32) for cols in _col_tiles(n)]
        ssq = sum(jnp.sum(y * y, axis=-1, keepdims=True) for y in ys)
        scale = lax.rsqrt(ssq * (1.0 / n) + RMS_EPS)
        for cols, y in zip(_col_tiles(n), ys):
            if emit_sum:
                out_refs[0][rows, cols] = y
            out_refs[-1][rows, cols] = (y * scale * lnw_ref[:, cols]).astype(out_refs[-1].dtype)


def _matmul_residual_norm(a, w, res, lnw, norm_dtype, emit_sum, name, bm, single_buffer_w=False):
    m, k = a.shape
    n = w.shape[1]
    row = lambda i: (i, 0)
    w_spec = (pl.BlockSpec((k, n), lambda i: (0, 0), pipeline_mode=pl.Buffered(1)) if single_buffer_w
              else pl.BlockSpec((k, n), lambda i: (0, 0)))
    norm_shape = jax.ShapeDtypeStruct((m, n), norm_dtype)
    out_shape = (jax.ShapeDtypeStruct((m, n), F32), norm_shape) if emit_sum else norm_shape
    out_specs = (pl.BlockSpec((bm, n), row), pl.BlockSpec((bm, n), row)) if emit_sum else pl.BlockSpec((bm, n), row)
    return pl.pallas_call(
        functools.partial(_mm_res_norm_kernel, emit_sum),
        out_shape=out_shape,
        grid=(m // bm,),
        in_specs=[pl.BlockSpec((bm, k), row), w_spec, pl.BlockSpec((bm, n), row), pl.BlockSpec((1, n), lambda i: (0, 0))],
        out_specs=out_specs,
        compiler_params=_params(1),
        name=name,
    )(a, w, res, lnw.reshape(1, n))


def _swiglu_kernel(a_ref, w1_ref, w3_ref, o_ref, w1_s, w3_s):
    w1_s[...] = w1_ref[...].astype(BF16)
    w3_s[...] = w3_ref[...].astype(BF16)
    for r in range(a_ref.shape[0] // MXU_ROW_TILE):
        rows = pl.ds(r * MXU_ROW_TILE, MXU_ROW_TILE)
        a = a_ref[rows, :]
        u = jnp.dot(a, w1_s[...], preferred_element_type=F32)
        g = jnp.dot(a, w3_s[...], preferred_element_type=F32)
        o_ref[rows, :] = (_silu(u) * g).astype(o_ref.dtype)


def _swiglu(a, w1, w3, name, bm, bn):
    m, k = a.shape
    n = w1.shape[1]
    return pl.pallas_call(
        _swiglu_kernel,
        scratch_shapes=[pltpu.VMEM((k, bn), BF16), pltpu.VMEM((k, bn), BF16)],
        out_shape=jax.ShapeDtypeStruct((m, n), BF16),
        grid=(m // bm, n // bn),
        in_specs=[
            pl.BlockSpec((bm, k), lambda i, j: (i, 0)),
            pl.BlockSpec((k, bn), lambda i, j: (0, j)),
            pl.BlockSpec((k, bn), lambda i, j: (0, j)),
        ],
        out_specs=pl.BlockSpec((bm, bn), lambda i, j: (i, j)),
        compiler_params=_params(2),
        name=name,
    )(a, w1, w3)


def _merge_kernel(h_ref, oa_ref, oh_ref, wga_ref, wgb_ref, wa_ref, wb_ref, o_ref, wga_s, wgb_s, wa_s, wb_s):
    for src, dst in ((wga_ref, wga_s), (wgb_ref, wgb_s), (wa_ref, wa_s), (wb_ref, wb_s)):
        dst[...] = src[...].astype(BF16)
    for rows in _row_tiles(h_ref.shape[0]):
        h = h_ref[rows, :]
        ga = jnp.dot(h, wga_s[...], preferred_element_type=F32)
        gb = jnp.dot(h, wgb_s[...], preferred_element_type=F32)
        pa = jnp.dot(oa_ref[rows, :], wa_s[...], preferred_element_type=F32)
        pb = jnp.dot(oh_ref[rows, :], wb_s[...], preferred_element_type=F32)
        o_ref[rows, :] = (_sigmoid(ga) * pa + _sigmoid(gb) * pb).astype(o_ref.dtype)


def _merge(h, o_att, o_hg, w_in, gate_col0, wa, wb, name, bm, bn):
    m, k = h.shape
    d = wa.shape[1]
    ga0 = gate_col0 // bn
    gb0 = (gate_col0 + d) // bn
    return pl.pallas_call(
        _merge_kernel,
        out_shape=jax.ShapeDtypeStruct((m, d), BF16),
        grid=(m // bm, d // bn),
        in_specs=[
            pl.BlockSpec((bm, k), lambda i, j: (i, 0)),
            pl.BlockSpec((bm, o_att.shape[1]), lambda i, j: (i, 0)),
            pl.BlockSpec((bm, o_hg.shape[1]), lambda i, j: (i, 0)),
            pl.BlockSpec((k, bn), lambda i, j: (0, j + ga0)),
            pl.BlockSpec((k, bn), lambda i, j: (0, j + gb0)),
            pl.BlockSpec((wa.shape[0], bn), lambda i, j: (0, j)),
            pl.BlockSpec((wb.shape[0], bn), lambda i, j: (0, j)),
        ],
        out_specs=pl.BlockSpec((bm, bn), lambda i, j: (i, j)),
        scratch_shapes=[pltpu.VMEM((k, bn), BF16), pltpu.VMEM((k, bn), BF16),
                        pltpu.VMEM((wa.shape[0], bn), BF16), pltpu.VMEM((wb.shape[0], bn), BF16)],
        compiler_params=_params(2),
        name=name,
    )(h, o_att, o_hg, w_in, w_in, wa, wb)


def _mid_kernel(mg_ref, x_ref, kv_ref, wout_ref, wq_ref, wo_ref, lnc_ref, lnf_ref, x2_ref, hf_ref):
    d = x_ref.shape[1]
    row_tiles = _row_tiles(x_ref.shape[0])
    col_tiles = _col_tiles(d)

    def residual_norm(rows, lhs, w_ref, base_ref, ln_ref):
        ys = [base_ref[rows, cols] + jnp.dot(lhs, w_ref[:, cols], preferred_element_type=F32) for cols in col_tiles]
        ssq = sum(jnp.sum(y * y, axis=-1, keepdims=True) for y in ys)
        scale = lax.rsqrt(ssq * (1.0 / d) + RMS_EPS)
        for cols, y in zip(col_tiles, ys):
            x2_ref[rows, cols] = y
            hf_ref[rows, cols] = (y * scale * ln_ref[:, cols]).astype(hf_ref.dtype)

    for rows in row_tiles:
        residual_norm(rows, mg_ref[rows, :], wout_ref, x_ref, lnc_ref)
    qcs = [jnp.dot(hf_ref[rows, :], wq_ref[...], preferred_element_type=F32).astype(BF16) for rows in row_tiles]
    scores = [[lax.dot_general(qc[:, h * HEAD_DIM:(h + 1) * HEAD_DIM], kv_ref[:, h * HEAD_DIM:(h + 1) * HEAD_DIM],
                               (((1,), (1,)), ((), ())), preferred_element_type=F32) * (HEAD_DIM ** -0.5)
               for h in range(CROSS_HEADS)] for qc in qcs]
    ocs = []
    for per_head in scores:
        outs = []
        for h, s in enumerate(per_head):
            m = jnp.max(s, axis=-1, keepdims=True)
            p = jnp.exp(s - m)
            pn = (p * (1.0 / jnp.sum(p, axis=-1, keepdims=True))).astype(BF16)
            v = kv_ref[:, CROSS_WIDTH + h * HEAD_DIM:CROSS_WIDTH + (h + 1) * HEAD_DIM]
            outs.append(jnp.dot(pn, v, preferred_element_type=F32).astype(BF16))
        ocs.append(jnp.concatenate(outs, axis=-1))
    for rows, oc in zip(row_tiles, ocs):
        residual_norm(rows, oc, wo_ref, x2_ref, lnf_ref)


def _mid_block(merged, x, kvc, w_out, wq, wo, ln_cross, ln_ffn, name, bm):
    b, s, d = x.shape
    mlen = kvc.shape[1]
    tile = pl.BlockSpec((None, bm, d), lambda bi, i: (bi, i, 0))

    def resident(shape):
        return pl.BlockSpec(shape, lambda bi, i: (0,) * len(shape), pipeline_mode=pl.Buffered(1))

    return pl.pallas_call(
        _mid_kernel,
        out_shape=(jax.ShapeDtypeStruct((b, s, d), F32), jax.ShapeDtypeStruct((b, s, d), BF16)),
        grid=(b, s // bm),
        in_specs=[
            tile, tile,
            pl.BlockSpec((None, mlen, 2 * CROSS_WIDTH), lambda bi, i: (bi, 0, 0)),
            resident(w_out.shape), resident(wq.shape), resident(wo.shape),
            resident((1, d)), resident((1, d)),
        ],
        out_specs=(tile, tile),
        compiler_params=_params(2),
        name=name,
    )(merged, x, kvc, w_out, wq, wo, ln_cross.reshape(1, d), ln_ffn.reshape(1, d))


def _block_scores(q, kk):
    return lax.dot_general(q, kk, (((1,), (1,)), ((), ())), preferred_element_type=F32)


def _block_softmax_pv(s, vv, mask):
    scale = HEAD_DIM ** -0.5
    s = jnp.where(mask, s, -jnp.inf)
    m = jnp.max(s, axis=-1, keepdims=True)
    p = jnp.exp2((s - m) * (scale * LOG2_E))
    l = jnp.sum(p, axis=-1, keepdims=True)
    o = jnp.dot(p.astype(BF16), vv, preferred_element_type=F32) * (1.0 / l)
    return o, m * scale + jnp.log(l)


ATT_BLOCKS_IN_FLIGHT = 4


def _attn_kernel(q0, k0, v0, q1, k1, v1, q2, k2, v2, o_ref, st, og, ls):
    seq = q0.shape[0]
    blk = ATT_BLOCK
    for idx, ref in enumerate((q1, k1, v1, q2, k2, v2)):
        st[idx] = ref[...].astype(F32)

    qi = lax.broadcasted_iota(jnp.int32, (blk, 2 * blk), 0)
    kj = lax.broadcasted_iota(jnp.int32, (blk, 2 * blk), 1)
    dist = qi - kj + blk
    first_mask = (lax.broadcasted_iota(jnp.int32, (blk, blk), 0) >= lax.broadcasted_iota(jnp.int32, (blk, blk), 1))

    srcs = ((q0, k0, v0), (0, 1, 2), (3, 4, 5))

    def rows(dil, r, first_blk, count):
        start = first_blk * blk * dil + r
        if dil == 1:
            return pl.ds(start, count)
        return pl.ds(start, count, stride=dil)

    def load(g, which, rws):
        if g == 0:
            return srcs[0][which][rws, :]
        return st[srcs[g][which], rws, :].astype(BF16)

    tasks = []
    for g, (window, dil) in enumerate(ATT_GROUPS):
        band_mask = (dist >= 0) & (dist <= window // dil)
        for r in range(dil):
            for n in range(seq // dil // blk):
                if n == 0:
                    tasks.append((g, rows(dil, r, 0, blk), rows(dil, r, 0, blk), first_mask))
                else:
                    tasks.append((g, rows(dil, r, n, blk), rows(dil, r, n - 1, 2 * blk), band_mask))

    def finish(batch, scores):
        for (g, q_rows, k_rows, mask), s in zip(batch, scores):
            o, lse = _block_softmax_pv(s, load(g, 2, k_rows), mask)
            og[g, q_rows, :] = o
            ls[g, q_rows, :] = jnp.broadcast_to(lse, (blk, HEAD_DIM))

    pending = None
    for t0 in range(0, len(tasks), ATT_BLOCKS_IN_FLIGHT):
        batch = tasks[t0:t0 + ATT_BLOCKS_IN_FLIGHT]
        scores = [_block_scores(load(g, 0, q_rows), load(g, 1, k_rows)) for g, q_rows, k_rows, _ in batch]
        if pending is not None:
            finish(*pending)
        pending = (batch, scores)
    finish(*pending)

    for n in range(seq // blk):
        sl = pl.ds(n * blk, blk)
        l0, l1, l2 = ls[0, sl, :], ls[1, sl, :], ls[2, sl, :]
        mx = jnp.maximum(jnp.maximum(l0, l1), l2)
        e0, e1, e2 = jnp.exp(l0 - mx), jnp.exp(l1 - mx), jnp.exp(l2 - mx)
        inv = 1.0 / (e0 + e1 + e2)
        out = (e0 * inv) * og[0, sl, :] + (e1 * inv) * og[1, sl, :] + (e2 * inv) * og[2, sl, :]
        o_ref[sl, :] = out.astype(o_ref.dtype)


def _dilated_attention(qkv, name):
    b, s, _ = qkv.shape
    hpg = ATT_HEADS_PER_GROUP
    n_cols = ATT_WIDTH // HEAD_DIM

    def spec(which, g):
        return pl.BlockSpec((None, s, HEAD_DIM), lambda bi, h: (bi, 0, which * n_cols + g * hpg + h))

    in_specs = [spec(which, g) for g in range(N_ATT_GROUPS) for which in range(3)]
    return pl.pallas_call(
        _attn_kernel,
        out_shape=jax.ShapeDtypeStruct((b, s, ATT_OUT_WIDTH), BF16),
        grid=(b, hpg),
        in_specs=in_specs,
        out_specs=pl.BlockSpec((None, s, HEAD_DIM), lambda bi, h: (bi, 0, h)),
        scratch_shapes=[
            pltpu.VMEM((6, s, HEAD_DIM), F32),
            pltpu.VMEM((3, s, HEAD_DIM), F32),
            pltpu.VMEM((3, s, HEAD_DIM), F32),
        ],
        compiler_params=_params(2),
        name=name,
    )(*([qkv] * 9))


def _hgrn_kernel(layer, q_ref, f_ref, i_ref, g_ref, lbp_ref, nw_ref, o_ref, state, qa_s, ka_s, v_s, gs_s, rv_s):
    ts = q_ref.shape[0]
    c = HG_CHUNK
    n_chunks = ts // c

    @pl.when(pl.program_id(1) == 0)
    def _():
        state[...] = jnp.zeros_like(state)

    a = lbp_ref[...]
    e = jnp.exp(a - jnp.max(a, axis=0, keepdims=True))
    sm = e / jnp.sum(e, axis=0, keepdims=True)
    lb = jnp.sum(sm[: layer + 1], axis=0, keepdims=True)
    f_mid = 0.5 * (1.0 + lb)
    f_half = 0.5 * (1.0 - lb)
    nw = nw_ref[...]

    ri = lax.broadcasted_iota(jnp.int32, (c, c), 0)
    ci = lax.broadcasted_iota(jnp.int32, (c, c), 1)
    causal = ri >= ci
    tri = jnp.where(causal, 1.0, 0.0).astype(BF16)

    for n in range(n_chunks):
        rows = pl.ds(n * c, c)
        f = f_mid + f_half * jnp.tanh(0.5 * f_ref[rows, :])
        lf = jnp.log(f)
        hi = lf.astype(BF16)
        r1 = lf - hi.astype(F32)
        mid = r1.astype(BF16)
        lo = (r1 - mid.astype(F32)).astype(BF16)
        bcum = (jnp.dot(tri, hi, preferred_element_type=F32) + jnp.dot(tri, mid, preferred_element_type=F32)
                + jnp.dot(tri, lo, preferred_element_type=F32))
        b_mid = bcum[c // 2:c // 2 + 1, :]
        b_last = bcum[c - 1:c, :]
        qh = q_ref[rows, :]
        qa_s[rows, :] = (_silu(qh) * jnp.exp(bcum - b_mid)).astype(BF16)
        ka_s[rows, :] = ((1.0 - f) * jnp.exp(b_mid - bcum)).astype(BF16)
        v_s[rows, :] = i_ref[rows, :].astype(BF16)
        gh = g_ref[rows, :]
        gs_s[rows, :] = _silu(gh) * nw
        rv_s[0, n:n + 1, :] = jnp.exp(b_mid)
        rv_s[1, n:n + 1, :] = jnp.exp(b_last - b_mid)
        rv_s[2, n:n + 1, :] = jnp.exp(b_last)

    heads = [pl.ds(h * HG_KEY, HG_KEY) for h in range(HG_HEADS)]
    for n in range(n_chunks):
        rows = pl.ds(n * c, c)
        both = []
        for h, cols in enumerate(heads):
            rhs = jnp.concatenate([(state[h] * rv_s[0, n:n + 1, cols]).astype(BF16), ka_s[rows, cols]], axis=0)
            both.append(lax.dot_general(qa_s[rows, cols], rhs, (((1,), (1,)), ((), ())), preferred_element_type=F32))
        kv_t = [lax.dot_general(v_s[rows, cols], ka_s[rows, cols], (((0,), (0,)), ((), ())), preferred_element_type=F32)
                for cols in heads]
        for h, cols in enumerate(heads):
            att = jnp.where(causal, both[h][:, HG_VAL:], 0.0).astype(BF16)
            o = both[h][:, :HG_VAL] + jnp.dot(att, v_s[rows, cols], preferred_element_type=F32)
            o = o * lax.rsqrt(jnp.mean(o * o, axis=-1, keepdims=True) + RMS_EPS) * gs_s[rows, cols]
            o_ref[rows, cols] = o.astype(o_ref.dtype)
        for h, cols in enumerate(heads):
            state[h] = rv_s[2, n:n + 1, cols] * state[h] + kv_t[h] * rv_s[1, n:n + 1, cols]


def _hgrn2(proj, col0, lower_bounds_param, norm_w, layer, name, ts=512):
    b, s, _ = proj.shape
    cb = col0 // HG_WIDTH

    def spec(which):
        return pl.BlockSpec((None, ts, HG_WIDTH), lambda bi, t: (bi, t, cb + which))

    n_layers = lower_bounds_param.shape[0]
    return pl.pallas_call(
        functools.partial(_hgrn_kernel, layer),
        out_shape=jax.ShapeDtypeStruct((b, s, HG_WIDTH), BF16),
        grid=(b, s // ts),
        in_specs=[spec(0), spec(1), spec(2), spec(3),
                  pl.BlockSpec((n_layers, HG_WIDTH), lambda bi, t: (0, 0)),
                  pl.BlockSpec((1, HG_WIDTH), lambda bi, t: (0, 0))],
        out_specs=pl.BlockSpec((None, ts, HG_WIDTH), lambda bi, t: (bi, t, 0)),
        scratch_shapes=[
            pltpu.VMEM((HG_HEADS, HG_VAL, HG_KEY), F32),
            pltpu.VMEM((ts, HG_WIDTH), BF16),
            pltpu.VMEM((ts, HG_WIDTH), BF16),
            pltpu.VMEM((ts, HG_WIDTH), BF16),
            pltpu.VMEM((ts, HG_WIDTH), F32),
            pltpu.VMEM((3, ts // HG_CHUNK, HG_WIDTH), F32),
        ],
        compiler_params=_params(2),
        name=name,
    )(proj, proj, proj, proj, lower_bounds_param, jnp.tile(norm_w, HG_HEADS).reshape(1, HG_WIDTH))


def kernel(x, mem, ln_mix_w, w_in, hg_norm_w, hg_lower_bounds, w_branch_a, w_branch_b, w_out, ln_cross_w, ln_mem_w, wq_cross, wkv_cross, wo_cross, ln_ffn_w, w1, w3, w2, ln_final_w):
    b, s, d = x.shape
    mlen = mem.shape[1]
    t = b * s
    depth = w_in.shape[0]
    qkv_w = 3 * ATT_WIDTH
    hg_w = 4 * HG_WIDTH

    xf = x.reshape(t, d)
    memf = mem.reshape(b * mlen, d)
    for l in range(depth):
        h = _rmsnorm(xf, ln_mix_w[l], BF16, "rms_mix")
        qkv = _matmul(h, w_in[l], 0, qkv_w, BF16, "proj_qkv", bm=2048, bn=512)
        hg = _matmul(h, w_in[l], qkv_w, hg_w, F32, "proj_hg", bm=2048, bn=512)
        o_att = _dilated_attention(qkv.reshape(b, s, qkv_w), "dilated_attn")
        o_hg = _hgrn2(hg.reshape(b, s, hg_w), 0, hg_lower_bounds, hg_norm_w[l], l, "hgrn2")
        merged = _merge(h, o_att.reshape(t, ATT_OUT_WIDTH), o_hg.reshape(t, HG_WIDTH), w_in[l], qkv_w + hg_w,
                        w_branch_a[l], w_branch_b[l], "gate_merge", bm=1024, bn=512)
        mn = _rmsnorm(memf, ln_mem_w[l], BF16, "rms_mem")
        kvc = _matmul(mn, wkv_cross[l], 0, 2 * CROSS_WIDTH, BF16, "cross_kv", bm=512, bn=512)
        x2, hf = _mid_block(merged.reshape(b, s, d), xf.reshape(b, s, d), kvc.reshape(b, mlen, 2 * CROSS_WIDTH),
                            w_out[l].astype(BF16), wq_cross[l].astype(BF16), wo_cross[l].astype(BF16),
                            ln_cross_w[l], ln_ffn_w[l], "out_proj_cross", bm=512)
        xf, hf = x2.reshape(t, d), hf.reshape(t, d)
        act = _swiglu(hf, w1[l], w3[l], "ffn_up", bm=2048, bn=512)
        if l + 1 < depth:
            xf = _matmul_residual(act, w2[l].astype(BF16), xf, "ffn_down", bm=512, bn=512)
        else:
            out = _matmul_residual_norm(act, w2[l].astype(BF16), xf, ln_final_w, F32, False, "ffn_down_final", bm=256,
                                        single_buffer_w=True)
    return out.reshape(b, s, d)
```

```python
import functools

import jax
import jax.numpy as jnp
from jax import lax
from jax.experimental import pallas as pl
from jax.experimental.pallas import tpu as pltpu

F32 = jnp.float32
BF16 = jnp.bfloat16

HEAD_DIM = 128
ATT_GROUPS = ((128, 1), (512, 4), (2048, 16))
ATT_HEADS_PER_GROUP = 4
N_ATT_GROUPS = len(ATT_GROUPS)
ATT_WIDTH = N_ATT_GROUPS * ATT_HEADS_PER_GROUP * HEAD_DIM
ATT_OUT_WIDTH = ATT_HEADS_PER_GROUP * HEAD_DIM
ATT_BLOCK = 128
HG_HEADS = 8
HG_KEY = 128
HG_VAL = 128
HG_WIDTH = HG_HEADS * HG_KEY
HG_CHUNK = 64
CROSS_HEADS = 4
CROSS_WIDTH = CROSS_HEADS * HEAD_DIM
RMS_EPS = 1e-6
LOG2_E = 1.4426950408889634

V7X_VMEM_LIMIT_BYTES = 56 * 1024 * 1024
MXU_ROW_TILE = 256
MXU_COL_TILE = 512


def _sigmoid(x):
    return 0.5 * jnp.tanh(0.5 * x) + 0.5


def _silu(x):
    hx = 0.5 * x
    return hx * jnp.tanh(hx) + hx


def _params(n_axes):
    return pltpu.CompilerParams(
        dimension_semantics=("arbitrary",) * n_axes,
        vmem_limit_bytes=V7X_VMEM_LIMIT_BYTES,
    )


def _rms_kernel(x_ref, w_ref, o_ref):
    x = x_ref[...]
    ms = jnp.mean(x * x, axis=-1, keepdims=True)
    o_ref[...] = (x * lax.rsqrt(ms + RMS_EPS) * w_ref[...]).astype(o_ref.dtype)


def _rmsnorm(x, w, out_dtype, name, bm=512):
    m, d = x.shape
    return pl.pallas_call(
        _rms_kernel,
        out_shape=jax.ShapeDtypeStruct((m, d), out_dtype),
        grid=(m // bm,),
        in_specs=[pl.BlockSpec((bm, d), lambda i: (i, 0)), pl.BlockSpec((1, d), lambda i: (0, 0))],
        out_specs=pl.BlockSpec((bm, d), lambda i: (i, 0)),
        compiler_params=_params(1),
        name=name,
    )(x, w.reshape(1, d))


def _row_tiles(n_rows):
    tile = min(MXU_ROW_TILE, n_rows)
    return [pl.ds(r * tile, tile) for r in range(n_rows // tile)]


def _col_tiles(n_cols):
    tile = min(MXU_COL_TILE, n_cols)
    return [pl.ds(c * tile, tile) for c in range(n_cols // tile)]


def _mm_kernel(a_ref, w_ref, o_ref, w_s):
    w_s[...] = w_ref[...].astype(BF16)
    for rows in _row_tiles(a_ref.shape[0]):
        res = jnp.dot(a_ref[rows, :], w_s[...], preferred_element_type=F32).astype(o_ref.dtype)
        if len(o_ref.shape) == 2:
            o_ref[rows, :] = res
        else:
            for hd in range(o_ref.shape[0]):
                o_ref[hd, rows, :] = res[:, hd * HEAD_DIM:(hd + 1) * HEAD_DIM]


def _matmul(a, w, col0, n, out_dtype, name, bm, bn, head_major=False):
    m, k = a.shape
    cb = col0 // bn
    if head_major:
        out_shape = jax.ShapeDtypeStruct((n // HEAD_DIM, m, HEAD_DIM), out_dtype)
        out_spec = pl.BlockSpec((bn // HEAD_DIM, bm, HEAD_DIM), lambda i, j: (j, i, 0))
    else:
        out_shape = jax.ShapeDtypeStruct((m, n), out_dtype)
        out_spec = pl.BlockSpec((bm, bn), lambda i, j: (i, j))
    return pl.pallas_call(
        _mm_kernel,
        out_shape=out_shape,
        grid=(m // bm, n // bn),
        in_specs=[
            pl.BlockSpec((bm, k), lambda i, j: (i, 0)),
            pl.BlockSpec((k, bn), lambda i, j: (0, j + cb)),
        ],
        out_specs=out_spec,
        scratch_shapes=[pltpu.VMEM((k, bn), BF16)],
        compiler_params=_params(2),
        name=name,
    )(a, w)


def _norm_mm_kernel(x_ref, lnw_ref, w_ref, o_ref):
    x = x_ref[...]
    h = (x * lax.rsqrt(jnp.mean(x * x, axis=-1, keepdims=True) + RMS_EPS) * lnw_ref[...]).astype(BF16)
    o_ref[...] = jnp.dot(h, w_ref[...].astype(BF16), preferred_element_type=F32).astype(o_ref.dtype)


def _norm_matmul(x, lnw, w, out_dtype, name, bm, bn):
    m, k = x.shape
    n = w.shape[1]
    return pl.pallas_call(
        _norm_mm_kernel,
        out_shape=jax.ShapeDtypeStruct((m, n), out_dtype),
        grid=(m // bm, n // bn),
        in_specs=[
            pl.BlockSpec((bm, k), lambda i, j: (i, 0)),
            pl.BlockSpec((1, k), lambda i, j: (0, 0)),
            pl.BlockSpec((k, bn), lambda i, j: (0, j)),
        ],
        out_specs=pl.BlockSpec((bm, bn), lambda i, j: (i, j)),
        compiler_params=_params(2),
        name=name,
    )(x, lnw.reshape(1, k), w)


def _mm_res_kernel(a_ref, w_ref, r_ref, o_ref):
    o_ref[...] = r_ref[...] + jnp.dot(a_ref[...], w_ref[...], preferred_element_type=F32)


def _matmul_residual(a, w, res, name, bm, bn):
    m, k = a.shape
    n = w.shape[1]
    return pl.pallas_call(
        _mm_res_kernel,
        out_shape=jax.ShapeDtypeStruct((m, n), F32),
        grid=(m // bm, n // bn),
        in_specs=[
            pl.BlockSpec((bm, k), lambda i, j: (i, 0)),
            pl.BlockSpec((k, bn), lambda i, j: (0, j)),
            pl.BlockSpec((bm, bn), lambda i, j: (i, j)),
        ],
        out_specs=pl.BlockSpec((bm, bn), lambda i, j: (i, j)),
        compiler_params=_params(2),
        name=name,
    )(a, w, res)


def _mm_res_norm_kernel(emit_sum, a_ref, w_ref, r_ref, lnw_ref, *out_refs):
    n = w_ref.shape[1]
    for rows in _row_tiles(a_ref.shape[0]):
        a = a_ref[rows, :]
        ys = [r_ref[rows, cols] + jnp.dot(a, w_ref[:, cols], preferred_element_type=F32) for cols in _col_tiles(n)]
        ssq = sum(jnp.sum(y * y, axis=-1, keepdims=True) for y in ys)
        scale = lax.rsqrt(ssq * (1.0 / n) + RMS_EPS)
        for cols, y in zip(_col_tiles(n), ys):
            if emit_sum:
                out_refs[0][rows, cols] = y
            out_refs[-1][rows, cols] = (y * scale * lnw_ref[:, cols]).astype(out_refs[-1].dtype)


def _matmul_residual_norm(a, w, res, lnw, norm_dtype, emit_sum, name, bm, single_buffer_w=False):
    m, k = a.shape
    n = w.shape[1]
    row = lambda i: (i, 0)
    w_spec = (pl.BlockSpec((k, n), lambda i: (0, 0), pipeline_mode=pl.Buffered(1)) if single_buffer_w
              else pl.BlockSpec((k, n), lambda i: (0, 0)))
    norm_shape = jax.ShapeDtypeStruct((m, n), norm_dtype)
    out_shape = (jax.ShapeDtypeStruct((m, n), F32), norm_shape) if emit_sum else norm_shape
    out_specs = (pl.BlockSpec((bm, n), row), pl.BlockSpec((bm, n), row)) if emit_sum else pl.BlockSpec((bm, n), row)
    return pl.pallas_call(
        functools.partial(_mm_res_norm_kernel, emit_sum),
        out_shape=out_shape,
        grid=(m // bm,),
        in_specs=[pl.BlockSpec((bm, k), row), w_spec, pl.BlockSpec((bm, n), row), pl.BlockSpec((1, n), lambda i: (0, 0))],
        out_specs=out_specs,
        compiler_params=_params(1),
        name=name,
    )(a, w, res, lnw.reshape(1, n))


def _swiglu_kernel(a_ref, w1_ref, w3_ref, o_ref, w1_s, w3_s):
    w1_s[...] = w1_ref[...].astype(BF16)
    w3_s[...] = w3_ref[...].astype(BF16)
    for r in range(a_ref.shape[0] // MXU_ROW_TILE):
        rows = pl.ds(r * MXU_ROW_TILE, MXU_ROW_TILE)
        a = a_ref[rows, :]
        u = jnp.dot(a, w1_s[...], preferred_element_type=F32)
        g = jnp.dot(a, w3_s[...], preferred_element_type=F32)
        o_ref[rows, :] = (_silu(u) * g).astype(o_ref.dtype)


def _swiglu(a, w1, w3, name, bm, bn):
    m, k = a.shape
    n = w1.shape[1]
    return pl.pallas_call(
        _swiglu_kernel,
        scratch_shapes=[pltpu.VMEM((k, bn), BF16), pltpu.VMEM((k, bn), BF16)],
        out_shape=jax.ShapeDtypeStruct((m, n), BF16),
        grid=(m // bm, n // bn),
        in_specs=[
            pl.BlockSpec((bm, k), lambda i, j: (i, 0)),
            pl.BlockSpec((k, bn), lambda i, j: (0, j)),
            pl.BlockSpec((k, bn), lambda i, j: (0, j)),
        ],
        out_specs=pl.BlockSpec((bm, bn), lambda i, j: (i, j)),
        compiler_params=_params(2),
        name=name,
    )(a, w1, w3)


def _merge_kernel(h_ref, oa_ref, oh_ref, wga_ref, wgb_ref, wa_ref, wb_ref, o_ref, wga_s, wgb_s, wa_s, wb_s):
    for src, dst in ((wga_ref, wga_s), (wgb_ref, wgb_s), (wa_ref, wa_s), (wb_ref, wb_s)):
        dst[...] = src[...].astype(BF16)
    for rows in _row_tiles(h_ref.shape[0]):
        h = h_ref[rows, :]
        ga = jnp.dot(h, wga_s[...], preferred_element_type=F32)
        gb = jnp.dot(h, wgb_s[...], preferred_element_type=F32)
        pa = jnp.dot(oa_ref[rows, :], wa_s[...], preferred_element_type=F32)
        pb = jnp.dot(oh_ref[rows, :], wb_s[...], preferred_element_type=F32)
        o_ref[rows, :] = (_sigmoid(ga) * pa + _sigmoid(gb) * pb).astype(o_ref.dtype)


def _merge(h, o_att, o_hg, w_in, gate_col0, wa, wb, name, bm, bn):
    m, k = h.shape
    d = wa.shape[1]
    ga0 = gate_col0 // bn
    gb0 = (gate_col0 + d) // bn
    return pl.pallas_call(
        _merge_kernel,
        out_shape=jax.ShapeDtypeStruct((m, d), BF16),
        grid=(m // bm, d // bn),
        in_specs=[
            pl.BlockSpec((bm, k), lambda i, j: (i, 0)),
            pl.BlockSpec((bm, o_att.shape[1]), lambda i, j: (i, 0)),
            pl.BlockSpec((bm, o_hg.shape[1]), lambda i, j: (i, 0)),
            pl.BlockSpec((k, bn), lambda i, j: (0, j + ga0)),
            pl.BlockSpec((k, bn), lambda i, j: (0, j + gb0)),
            pl.BlockSpec((wa.shape[0], bn), lambda i, j: (0, j)),
            pl.BlockSpec((wb.shape[0], bn), lambda i, j: (0, j)),
        ],
        out_specs=pl.BlockSpec((bm, bn), lambda i, j: (i, j)),
        scratch_shapes=[pltpu.VMEM((k, bn), BF16), pltpu.VMEM((k, bn), BF16),
                        pltpu.VMEM((wa.shape[0], bn), BF16), pltpu.VMEM((wb.shape[0], bn), BF16)],
        compiler_params=_params(2),
        name=name,
    )(h, o_att, o_hg, w_in, w_in, wa, wb)


def _mid_kernel(mg_ref, x_ref, kv_ref, wout_ref, wq_ref, wo_ref, lnc_ref, lnf_ref, x2_ref, hf_ref):
    d = x_ref.shape[1]
    row_tiles = _row_tiles(x_ref.shape[0])
    col_tiles = _col_tiles(d)

    def residual_norm(rows, lhs, w_ref, base_ref, ln_ref):
        ys = [base_ref[rows, cols] + jnp.dot(lhs, w_ref[:, cols], preferred_element_type=F32) for cols in col_tiles]
        ssq = sum(jnp.sum(y * y, axis=-1, keepdims=True) for y in ys)
        scale = lax.rsqrt(ssq * (1.0 / d) + RMS_EPS)
        for cols, y in zip(col_tiles, ys):
            x2_ref[rows, cols] = y
            hf_ref[rows, cols] = (y * scale * ln_ref[:, cols]).astype(hf_ref.dtype)

    for rows in row_tiles:
        residual_norm(rows, mg_ref[rows, :], wout_ref, x_ref, lnc_ref)
    qcs = [jnp.dot(hf_ref[rows, :], wq_ref[...], preferred_element_type=F32).astype(BF16) for rows in row_tiles]
    scores = [[lax.dot_general(qc[:, h * HEAD_DIM:(h + 1) * HEAD_DIM], kv_ref[:, h * HEAD_DIM:(h + 1) * HEAD_DIM],
                               (((1,), (1,)), ((), ())), preferred_element_type=F32) * (HEAD_DIM ** -0.5)
               for h in range(CROSS_HEADS)] for qc in qcs]
    ocs = []
    for per_head in scores:
        outs = []
        for h, s in enumerate(per_head):
            m = jnp.max(s, axis=-1, keepdims=True)
            p = jnp.exp(s - m)
            pn = (p * (1.0 / jnp.sum(p, axis=-1, keepdims=True))).astype(BF16)
            v = kv_ref[:, CROSS_WIDTH + h * HEAD_DIM:CROSS_WIDTH + (h + 1) * HEAD_DIM]
            outs.append(jnp.dot(pn, v, preferred_element_type=F32).astype(BF16))
        ocs.append(jnp.concatenate(outs, axis=-1))
    for rows, oc in zip(row_tiles, ocs):
        residual_norm(rows, oc, wo_ref, x2_ref, lnf_ref)


def _mid_block(merged, x, kvc, w_out, wq, wo, ln_cross, ln_ffn, name, bm):
    b, s, d = x.shape
    mlen = kvc.shape[1]
    tile = pl.BlockSpec((None, bm, d), lambda bi, i: (bi, i, 0))

    def resident(shape):
        return pl.BlockSpec(shape, lambda bi, i: (0,) * len(shape), pipeline_mode=pl.Buffered(1))

    return pl.pallas_call(
        _mid_kernel,
        out_shape=(jax.ShapeDtypeStruct((b, s, d), F32), jax.ShapeDtypeStruct((b, s, d), BF16)),
        grid=(b, s // bm),
        in_specs=[
            tile, tile,
            pl.BlockSpec((None, mlen, 2 * CROSS_WIDTH), lambda bi, i: (bi, 0, 0)),
            resident(w_out.shape), resident(wq.shape), resident(wo.shape),
            resident((1, d)), resident((1, d)),
        ],
        out_specs=(tile, tile),
        compiler_params=_params(2),
        name=name,
    )(merged, x, kvc, w_out, wq, wo, ln_cross.reshape(1, d), ln_ffn.reshape(1, d))


def _block_scores(q, kk):
    return lax.dot_general(q, kk, (((1,), (1,)), ((), ())), preferred_element_type=F32)


def _block_softmax_pv(s, vv, mask):
    scale = HEAD_DIM ** -0.5
    s = jnp.where(mask, s, -jnp.inf)
    m = jnp.max(s, axis=-1, keepdims=True)
    p = jnp.exp2((s - m) * (scale * LOG2_E))
    l = jnp.sum(p, axis=-1, keepdims=True)
    o = jnp.dot(p.astype(BF16), vv, preferred_element_type=F32) * (1.0 / l)
    return o, m * scale + jnp.log(l)


ATT_BLOCKS_IN_FLIGHT = 4


def _attn_kernel(q0, k0, v0, q1, k1, v1, q2, k2, v2, o_ref, st, og, ls):
    seq = q0.shape[0]
    blk = ATT_BLOCK
    for idx, ref in enumerate((q1, k1, v1, q2, k2, v2)):
        st[idx] = ref[...].astype(F32)

    qi = lax.broadcasted_iota(jnp.int32, (blk, 2 * blk), 0)
    kj = lax.broadcasted_iota(jnp.int32, (blk, 2 * blk), 1)
    dist = qi - kj + blk
    first_mask = (lax.broadcasted_iota(jnp.int32, (blk, blk), 0) >= lax.broadcasted_iota(jnp.int32, (blk, blk), 1))

    srcs = ((q0, k0, v0), (0, 1, 2), (3, 4, 5))

    def rows(dil, r, first_blk, count):
        start = first_blk * blk * dil + r
        if dil == 1:
            return pl.ds(start, count)
        return pl.ds(start, count, stride=dil)

    def load(g, which, rws):
        if g == 0:
            return srcs[0][which][rws, :]
        return st[srcs[g][which], rws, :].astype(BF16)

    tasks = []
    for g, (window, dil) in enumerate(ATT_GROUPS):
        band_mask = (dist >= 0) & (dist <= window // dil)
        for r in range(dil):
            for n in range(seq // dil // blk):
                if n == 0:
                    tasks.append((g, rows(dil, r, 0, blk), rows(dil, r, 0, blk), first_mask))
                else:
                    tasks.append((g, rows(dil, r, n, blk), rows(dil, r, n - 1, 2 * blk), band_mask))

    def finish(batch, scores):
        for (g, q_rows, k_rows, mask), s in zip(batch, scores):
            o, lse = _block_softmax_pv(s, load(g, 2, k_rows), mask)
            og[g, q_rows, :] = o
            ls[g, q_rows, :] = jnp.broadcast_to(lse, (blk, HEAD_DIM))

    pending = None
    for t0 in range(0, len(tasks), ATT_BLOCKS_IN_FLIGHT):
        batch = tasks[t0:t0 + ATT_BLOCKS_IN_FLIGHT]
        scores = [_block_scores(load(g, 0, q_rows), load(g, 1, k_rows)) for g, q_rows, k_rows, _ in batch]
        if pending is not None:
            finish(*pending)
        pending = (batch, scores)
    finish(*pending)

    for n in range(seq // blk):
        sl = pl.ds(n * blk, blk)
        l0, l1, l2 = ls[0, sl, :], ls[1, sl, :], ls[2, sl, :]
        mx = jnp.maximum(jnp.maximum(l0, l1), l2)
        e0, e1, e2 = jnp.exp(l0 - mx), jnp.exp(l1 - mx), jnp.exp(l2 - mx)
        inv = 1.0 / (e0 + e1 + e2)
        out = (e0 * inv) * og[0, sl, :] + (e1 * inv) * og[1, sl, :] + (e2 * inv) * og[2, sl, :]
        o_ref[sl, :] = out.astype(o_ref.dtype)


def _dilated_attention(qkv, b, name):
    s = qkv.shape[1] // b
    hpg = ATT_HEADS_PER_GROUP
    n_cols = ATT_WIDTH // HEAD_DIM

    def spec(which, g):
        return pl.BlockSpec((None, s, HEAD_DIM), lambda bi, h: (which * n_cols + g * hpg + h, bi, 0))

    in_specs = [spec(which, g) for g in range(N_ATT_GROUPS) for which in range(3)]
    return pl.pallas_call(
        _attn_kernel,
        out_shape=jax.ShapeDtypeStruct((b, s, ATT_OUT_WIDTH), BF16),
        grid=(b, hpg),
        in_specs=in_specs,
        out_specs=pl.BlockSpec((None, s, HEAD_DIM), lambda bi, h: (bi, 0, h)),
        scratch_shapes=[
            pltpu.VMEM((6, s, HEAD_DIM), F32),
            pltpu.VMEM((3, s, HEAD_DIM), F32),
            pltpu.VMEM((3, s, HEAD_DIM), F32),
        ],
        compiler_params=_params(2),
        name=name,
    )(*([qkv] * 9))


def _hgrn_kernel(layer, q_ref, f_ref, i_ref, g_ref, lbp_ref, nw_ref, o_ref, state, qa_s, ka_s, v_s, gs_s, rv_s):
    ts = q_ref.shape[0]
    c = HG_CHUNK
    n_chunks = ts // c

    @pl.when(pl.program_id(1) == 0)
    def _():
        state[...] = jnp.zeros_like(state)

    a = lbp_ref[...]
    e = jnp.exp(a - jnp.max(a, axis=0, keepdims=True))
    sm = e / jnp.sum(e, axis=0, keepdims=True)
    lb = jnp.sum(sm[: layer + 1], axis=0, keepdims=True)
    f_mid = 0.5 * (1.0 + lb)
    f_half = 0.5 * (1.0 - lb)
    nw = nw_ref[...]

    ri = lax.broadcasted_iota(jnp.int32, (c, c), 0)
    ci = lax.broadcasted_iota(jnp.int32, (c, c), 1)
    causal = ri >= ci
    tri = jnp.where(causal, 1.0, 0.0).astype(BF16)

    for n in range(n_chunks):
        rows = pl.ds(n * c, c)
        f = f_mid + f_half * jnp.tanh(0.5 * f_ref[rows, :])
        lf = jnp.log(f)
        hi = lf.astype(BF16)
        r1 = lf - hi.astype(F32)
        mid = r1.astype(BF16)
        lo = (r1 - mid.astype(F32)).astype(BF16)
        bcum = (jnp.dot(tri, hi, preferred_element_type=F32) + jnp.dot(tri, mid, preferred_element_type=F32)
                + jnp.dot(tri, lo, preferred_element_type=F32))
        b_mid = bcum[c // 2:c // 2 + 1, :]
        b_last = bcum[c - 1:c, :]
        qh = q_ref[rows, :]
        qa_s[rows, :] = (_silu(qh) * jnp.exp(bcum - b_mid)).astype(BF16)
        ka_s[rows, :] = ((1.0 - f) * jnp.exp(b_mid - bcum)).astype(BF16)
        v_s[rows, :] = i_ref[rows, :].astype(BF16)
        gh = g_ref[rows, :]
        gs_s[rows, :] = _silu(gh) * nw
        rv_s[0, n:n + 1, :] = jnp.exp(b_mid)
        rv_s[1, n:n + 1, :] = jnp.exp(b_last - b_mid)
        rv_s[2, n:n + 1, :] = jnp.exp(b_last)

    heads = [pl.ds(h * HG_KEY, HG_KEY) for h in range(HG_HEADS)]
    for n in range(n_chunks):
        rows = pl.ds(n * c, c)
        both = []
        for h, cols in enumerate(heads):
            rhs = jnp.concatenate([(state[h] * rv_s[0, n:n + 1, cols]).astype(BF16), ka_s[rows, cols]], axis=0)
            both.append(lax.dot_general(qa_s[rows, cols], rhs, (((1,), (1,)), ((), ())), preferred_element_type=F32))
        kv_t = [lax.dot_general(v_s[rows, cols], ka_s[rows, cols], (((0,), (0,)), ((), ())), preferred_element_type=F32)
                for cols in heads]
        for h, cols in enumerate(heads):
            att = jnp.where(causal, both[h][:, HG_VAL:], 0.0).astype(BF16)
            o = both[h][:, :HG_VAL] + jnp.dot(att, v_s[rows, cols], preferred_element_type=F32)
            o = o * lax.rsqrt(jnp.mean(o * o, axis=-1, keepdims=True) + RMS_EPS) * gs_s[rows, cols]
            o_ref[rows, cols] = o.astype(o_ref.dtype)
        for h, cols in enumerate(heads):
            state[h] = rv_s[2, n:n + 1, cols] * state[h] + kv_t[h] * rv_s[1, n:n + 1, cols]


def _hgrn2(proj, col0, lower_bounds_param, norm_w, layer, name, ts=512):
    b, s, _ = proj.shape
    cb = col0 // HG_WIDTH

    def spec(which):
        return pl.BlockSpec((None, ts, HG_WIDTH), lambda bi, t: (bi, t, cb + which))

    n_layers = lower_bounds_param.shape[0]
    return pl.pallas_call(
        functools.partial(_hgrn_kernel, layer),
        out_shape=jax.ShapeDtypeStruct((b, s, HG_WIDTH), BF16),
        grid=(b, s // ts),
        in_specs=[spec(0), spec(1), spec(2), spec(3),
                  pl.BlockSpec((n_layers, HG_WIDTH), lambda bi, t: (0, 0)),
                  pl.BlockSpec((1, HG_WIDTH), lambda bi, t: (0, 0))],
        out_specs=pl.BlockSpec((None, ts, HG_WIDTH), lambda bi, t: (bi, t, 0)),
        scratch_shapes=[
            pltpu.VMEM((HG_HEADS, HG_VAL, HG_KEY), F32),
            pltpu.VMEM((ts, HG_WIDTH), BF16),
            pltpu.VMEM((ts, HG_WIDTH), BF16),
            pltpu.VMEM((ts, HG_WIDTH), BF16),
            pltpu.VMEM((ts, HG_WIDTH), F32),
            pltpu.VMEM((3, ts // HG_CHUNK, HG_WIDTH), F32),
        ],
        compiler_params=_params(2),
        name=name,
    )(proj, proj, proj, proj, lower_bounds_param, jnp.tile(norm_w, HG_HEADS).reshape(1, HG_WIDTH))


def kernel(x, mem, ln_mix_w, w_in, hg_norm_w, hg_lower_bounds, w_branch_a, w_branch_b, w_out, ln_cross_w, ln_mem_w, wq_cross, wkv_cross, wo_cross, ln_ffn_w, w1, w3, w2, ln_final_w):
    b, s, d = x.shape
    mlen = mem.shape[1]
    t = b * s
    depth = w_in.shape[0]
    qkv_w = 3 * ATT_WIDTH
    hg_w = 4 * HG_WIDTH

    xf = x.reshape(t, d)
    memf = mem.reshape(b * mlen, d)
    for l in range(depth):
        h = _rmsnorm(xf, ln_mix_w[l], BF16, "rms_mix")
        qkv = _matmul(h, w_in[l], 0, qkv_w, BF16, "proj_qkv", bm=2048, bn=768, head_major=True)
        hg = _matmul(h, w_in[l], qkv_w, hg_w, F32, "proj_hg", bm=2048, bn=512)
        o_att = _dilated_attention(qkv, b, "dilated_attn")
        o_hg = _hgrn2(hg.reshape(b, s, hg_w), 0, hg_lower_bounds, hg_norm_w[l], l, "hgrn2")
        merged = _merge(h, o_att.reshape(t, ATT_OUT_WIDTH), o_hg.reshape(t, HG_WIDTH), w_in[l], qkv_w + hg_w,
                        w_branch_a[l], w_branch_b[l], "gate_merge", bm=1024, bn=512)
        kvc = _norm_matmul(memf, ln_mem_w[l], wkv_cross[l], BF16, "cross_kv", bm=512, bn=512)
        x2, hf = _mid_block(merged.reshape(b, s, d), xf.reshape(b, s, d), kvc.reshape(b, mlen, 2 * CROSS_WIDTH),
                            w_out[l].astype(BF16), wq_cross[l].astype(BF16), wo_cross[l].astype(BF16),
                            ln_cross_w[l], ln_ffn_w[l], "out_proj_cross", bm=512)
        xf, hf = x2.reshape(t, d), hf.reshape(t, d)
        act = _swiglu(hf, w1[l], w3[l], "ffn_up", bm=2048, bn=512)
        if l + 1 < depth:
            xf = _matmul_residual(act, w2[l].astype(BF16), xf, "ffn_down", bm=512, bn=512)
        else:
            out = _matmul_residual_norm(act, w2[l].astype(BF16), xf, ln_final_w, F32, False, "ffn_down_final", bm=512,
                                        single_buffer_w=True)
    return out.reshape(b, s, d)
```

```python
import functools

import jax
import jax.numpy as jnp
from jax import lax
from jax.experimental import pallas as pl
from jax.experimental.pallas import tpu as pltpu

F32 = jnp.float32
BF16 = jnp.bfloat16

HEAD_DIM = 128
ATT_GROUPS = ((128, 1), (512, 4), (2048, 16))
ATT_HEADS_PER_GROUP = 4
N_ATT_GROUPS = len(ATT_GROUPS)
ATT_WIDTH = N_ATT_GROUPS * ATT_HEADS_PER_GROUP * HEAD_DIM
ATT_OUT_WIDTH = ATT_HEADS_PER_GROUP * HEAD_DIM
ATT_BLOCK = 128
HG_HEADS = 8
HG_KEY = 128
HG_VAL = 128
HG_WIDTH = HG_HEADS * HG_KEY
HG_CHUNK = 64
CROSS_HEADS = 4
CROSS_WIDTH = CROSS_HEADS * HEAD_DIM
RMS_EPS = 1e-6
LOG2_E = 1.4426950408889634

V7X_VMEM_LIMIT_BYTES = 56 * 1024 * 1024
MXU_ROW_TILE = 256
MXU_COL_TILE = 512


def _sigmoid(x):
    return 0.5 * jnp.tanh(0.5 * x) + 0.5


def _silu(x):
    hx = 0.5 * x
    return hx * jnp.tanh(hx) + hx


def _params(n_axes):
    return pltpu.CompilerParams(
        dimension_semantics=("arbitrary",) * n_axes,
        vmem_limit_bytes=V7X_VMEM_LIMIT_BYTES,
    )


def _rms_kernel(x_ref, w_ref, o_ref):
    x = x_ref[...]
    ms = jnp.mean(x * x, axis=-1, keepdims=True)
    o_ref[...] = (x * lax.rsqrt(ms + RMS_EPS) * w_ref[...]).astype(o_ref.dtype)


def _rmsnorm(x, w, out_dtype, name, bm=512):
    m, d = x.shape
    return pl.pallas_call(
        _rms_kernel,
        out_shape=jax.ShapeDtypeStruct((m, d), out_dtype),
        grid=(m // bm,),
        in_specs=[pl.BlockSpec((bm, d), lambda i: (i, 0)), pl.BlockSpec((1, d), lambda i: (0, 0))],
        out_specs=pl.BlockSpec((bm, d), lambda i: (i, 0)),
        compiler_params=_params(1),
        name=name,
    )(x, w.reshape(1, d))


def _row_tiles(n_rows):
    tile = min(MXU_ROW_TILE, n_rows)
    return [pl.ds(r * tile, tile) for r in range(n_rows // tile)]


def _col_tiles(n_cols):
    tile = min(MXU_COL_TILE, n_cols)
    return [pl.ds(c * tile, tile) for c in range(n_cols // tile)]


def _mm_kernel(a_ref, w_ref, o_ref, w_s):
    w_s[...] = w_ref[...].astype(BF16)
    for rows in _row_tiles(a_ref.shape[0]):
        res = jnp.dot(a_ref[rows, :], w_s[...], preferred_element_type=F32).astype(o_ref.dtype)
        if len(o_ref.shape) == 2:
            o_ref[rows, :] = res
        else:
            for hd in range(o_ref.shape[0]):
                o_ref[hd, rows, :] = res[:, hd * HEAD_DIM:(hd + 1) * HEAD_DIM]


def _matmul(a, w, col0, n, out_dtype, name, bm, bn, head_major=False):
    m, k = a.shape
    cb = col0 // bn
    if head_major:
        out_shape = jax.ShapeDtypeStruct((n // HEAD_DIM, m, HEAD_DIM), out_dtype)
        out_spec = pl.BlockSpec((bn // HEAD_DIM, bm, HEAD_DIM), lambda i, j: (j, i, 0))
    else:
        out_shape = jax.ShapeDtypeStruct((m, n), out_dtype)
        out_spec = pl.BlockSpec((bm, bn), lambda i, j: (i, j))
    return pl.pallas_call(
        _mm_kernel,
        out_shape=out_shape,
        grid=(m // bm, n // bn),
        in_specs=[
            pl.BlockSpec((bm, k), lambda i, j: (i, 0)),
            pl.BlockSpec((k, bn), lambda i, j: (0, j + cb)),
        ],
        out_specs=out_spec,
        scratch_shapes=[pltpu.VMEM((k, bn), BF16)],
        compiler_params=_params(2),
        name=name,
    )(a, w)


def _norm_mm_kernel(x_ref, lnw_ref, w_ref, o_ref):
    x = x_ref[...]
    h = (x * lax.rsqrt(jnp.mean(x * x, axis=-1, keepdims=True) + RMS_EPS) * lnw_ref[...]).astype(BF16)
    o_ref[...] = jnp.dot(h, w_ref[...].astype(BF16), preferred_element_type=F32).astype(o_ref.dtype)


def _norm_matmul(x, lnw, w, out_dtype, name, bm, bn):
    m, k = x.shape
    n = w.shape[1]
    return pl.pallas_call(
        _norm_mm_kernel,
        out_shape=jax.ShapeDtypeStruct((m, n), out_dtype),
        grid=(m // bm, n // bn),
        in_specs=[
            pl.BlockSpec((bm, k), lambda i, j: (i, 0)),
            pl.BlockSpec((1, k), lambda i, j: (0, 0)),
            pl.BlockSpec((k, bn), lambda i, j: (0, j)),
        ],
        out_specs=pl.BlockSpec((bm, bn), lambda i, j: (i, j)),
        compiler_params=_params(2),
        name=name,
    )(x, lnw.reshape(1, k), w)


def _mm_res_kernel(a_ref, w_ref, r_ref, o_ref):
    o_ref[...] = r_ref[...] + jnp.dot(a_ref[...], w_ref[...], preferred_element_type=F32)


def _matmul_residual(a, w, res, name, bm, bn):
    m, k = a.shape
    n = w.shape[1]
    return pl.pallas_call(
        _mm_res_kernel,
        out_shape=jax.ShapeDtypeStruct((m, n), F32),
        grid=(m // bm, n // bn),
        in_specs=[
            pl.BlockSpec((bm, k), lambda i, j: (i, 0)),
            pl.BlockSpec((k, bn), lambda i, j: (0, j)),
            pl.BlockSpec((bm, bn), lambda i, j: (i, j)),
        ],
        out_specs=pl.BlockSpec((bm, bn), lambda i, j: (i, j)),
        compiler_params=_params(2),
        name=name,
    )(a, w, res)


def _mm_res_norm_kernel(emit_sum, a_ref, w_ref, r_ref, lnw_ref, *out_refs):
    n = w_ref.shape[1]
    for rows in _row_tiles(a_ref.shape[0]):
        a = a_ref[rows, :]
        ys = [r_ref[rows, cols] + jnp.dot(a, w_ref[:, cols], preferred_element_type=F32) for cols in _col_tiles(n)]
        ssq = sum(jnp.sum(y * y, axis=-1, keepdims=True) for y in ys)
        scale = lax.rsqrt(ssq * (1.0 / n) + RMS_EPS)
        for cols, y in zip(_col_tiles(n), ys):
            if emit_sum:
                out_refs[0][rows, cols] = y
            out_refs[-1][rows, cols] = (y * scale * lnw_ref[:, cols]).astype(out_refs[-1].dtype)


def _matmul_residual_norm(a, w, res, lnw, norm_dtype, emit_sum, name, bm, single_buffer_w=False):
    m, k = a.shape
    n = w.shape[1]
    row = lambda i: (i, 0)
    w_spec = (pl.BlockSpec((k, n), lambda i: (0, 0), pipeline_mode=pl.Buffered(1)) if single_buffer_w
              else pl.BlockSpec((k, n), lambda i: (0, 0)))
    norm_shape = jax.ShapeDtypeStruct((m, n), norm_dtype)
    out_shape = (jax.ShapeDtypeStruct((m, n), F32), norm_shape) if emit_sum else norm_shape
    out_specs = (pl.BlockSpec((bm, n), row), pl.BlockSpec((bm, n), row)) if emit_sum else pl.BlockSpec((bm, n), row)
    return pl.pallas_call(
        functools.partial(_mm_res_norm_kernel, emit_sum),
        out_shape=out_shape,
        grid=(m // bm,),
        in_specs=[pl.BlockSpec((bm, k), row), w_spec, pl.BlockSpec((bm, n), row), pl.BlockSpec((1, n), lambda i: (0, 0))],
        out_specs=out_specs,
        compiler_params=_params(1),
        name=name,
    )(a, w, res, lnw.reshape(1, n))


def _swiglu_kernel(a_ref, w1_ref, w3_ref, o_ref, w1_s, w3_s):
    w1_s[...] = w1_ref[...].astype(BF16)
    w3_s[...] = w3_ref[...].astype(BF16)
    for r in range(a_ref.shape[0] // MXU_ROW_TILE):
        rows = pl.ds(r * MXU_ROW_TILE, MXU_ROW_TILE)
        a = a_ref[rows, :]
        u = jnp.dot(a, w1_s[...], preferred_element_type=F32)
        g = jnp.dot(a, w3_s[...], preferred_element_type=F32)
        o_ref[rows, :] = (_silu(u) * g).astype(o_ref.dtype)


def _swiglu(a, w1, w3, name, bm, bn):
    m, k = a.shape
    n = w1.shape[1]
    return pl.pallas_call(
        _swiglu_kernel,
        scratch_shapes=[pltpu.VMEM((k, bn), BF16), pltpu.VMEM((k, bn), BF16)],
        out_shape=jax.ShapeDtypeStruct((m, n), BF16),
        grid=(m // bm, n // bn),
        in_specs=[
            pl.BlockSpec((bm, k), lambda i, j: (i, 0)),
            pl.BlockSpec((k, bn), lambda i, j: (0, j)),
            pl.BlockSpec((k, bn), lambda i, j: (0, j)),
        ],
        out_specs=pl.BlockSpec((bm, bn), lambda i, j: (i, j)),
        compiler_params=_params(2),
        name=name,
    )(a, w1, w3)


def _merge_kernel(h_ref, oa_ref, oh_ref, wga_ref, wgb_ref, wa_ref, wb_ref, o_ref, wga_s, wgb_s, wa_s, wb_s):
    for src, dst in ((wga_ref, wga_s), (wgb_ref, wgb_s), (wa_ref, wa_s), (wb_ref, wb_s)):
        dst[...] = src[...].astype(BF16)
    for rows in _row_tiles(h_ref.shape[0]):
        h = h_ref[rows, :]
        ga = jnp.dot(h, wga_s[...], preferred_element_type=F32)
        gb = jnp.dot(h, wgb_s[...], preferred_element_type=F32)
        pa = jnp.dot(oa_ref[rows, :], wa_s[...], preferred_element_type=F32)
        pb = jnp.dot(oh_ref[rows, :], wb_s[...], preferred_element_type=F32)
        o_ref[rows, :] = (_sigmoid(ga) * pa + _sigmoid(gb) * pb).astype(o_ref.dtype)


def _merge(h, o_att, o_hg, w_in, gate_col0, wa, wb, name, bm, bn):
    m, k = h.shape
    d = wa.shape[1]
    ga0 = gate_col0 // bn
    gb0 = (gate_col0 + d) // bn
    return pl.pallas_call(
        _merge_kernel,
        out_shape=jax.ShapeDtypeStruct((m, d), BF16),
        grid=(m // bm, d // bn),
        in_specs=[
            pl.BlockSpec((bm, k), lambda i, j: (i, 0)),
            pl.BlockSpec((bm, o_att.shape[1]), lambda i, j: (i, 0)),
            pl.BlockSpec((bm, o_hg.shape[1]), lambda i, j: (i, 0)),
            pl.BlockSpec((k, bn), lambda i, j: (0, j + ga0)),
            pl.BlockSpec((k, bn), lambda i, j: (0, j + gb0)),
            pl.BlockSpec((wa.shape[0], bn), lambda i, j: (0, j)),
            pl.BlockSpec((wb.shape[0], bn), lambda i, j: (0, j)),
        ],
        out_specs=pl.BlockSpec((bm, bn), lambda i, j: (i, j)),
        scratch_shapes=[pltpu.VMEM((k, bn), BF16), pltpu.VMEM((k, bn), BF16),
                        pltpu.VMEM((wa.shape[0], bn), BF16), pltpu.VMEM((wb.shape[0], bn), BF16)],
        compiler_params=_params(2),
        name=name,
    )(h, o_att, o_hg, w_in, w_in, wa, wb)


def _mid_kernel(mg_ref, x_ref, kv_ref, wout_ref, wq_ref, wo_ref, lnc_ref, lnf_ref, x2_ref, hf_ref):
    d = x_ref.shape[1]
    row_tiles = _row_tiles(x_ref.shape[0])
    col_tiles = _col_tiles(d)

    def residual_norm(rows, lhs, w_ref, base_ref, ln_ref):
        ys = [base_ref[rows, cols] + jnp.dot(lhs, w_ref[:, cols], preferred_element_type=F32) for cols in col_tiles]
        ssq = sum(jnp.sum(y * y, axis=-1, keepdims=True) for y in ys)
        scale = lax.rsqrt(ssq * (1.0 / d) + RMS_EPS)
        for cols, y in zip(col_tiles, ys):
            x2_ref[rows, cols] = y
            hf_ref[rows, cols] = (y * scale * ln_ref[:, cols]).astype(hf_ref.dtype)

    for rows in row_tiles:
        residual_norm(rows, mg_ref[rows, :], wout_ref, x_ref, lnc_ref)
    qcs = [jnp.dot(hf_ref[rows, :], wq_ref[...], preferred_element_type=F32).astype(BF16) for rows in row_tiles]
    scores = [[lax.dot_general(qc[:, h * HEAD_DIM:(h + 1) * HEAD_DIM], kv_ref[:, h * HEAD_DIM:(h + 1) * HEAD_DIM],
                               (((1,), (1,)), ((), ())), preferred_element_type=F32) * (HEAD_DIM ** -0.5)
               for h in range(CROSS_HEADS)] for qc in qcs]
    ocs = []
    for per_head in scores:
        outs = []
        for h, s in enumerate(per_head):
            m = jnp.max(s, axis=-1, keepdims=True)
            p = jnp.exp(s - m)
            pn = (p * (1.0 / jnp.sum(p, axis=-1, keepdims=True))).astype(BF16)
            v = kv_ref[:, CROSS_WIDTH + h * HEAD_DIM:CROSS_WIDTH + (h + 1) * HEAD_DIM]
            outs.append(jnp.dot(pn, v, preferred_element_type=F32).astype(BF16))
        ocs.append(jnp.concatenate(outs, axis=-1))
    for rows, oc in zip(row_tiles, ocs):
        residual_norm(rows, oc, wo_ref, x2_ref, lnf_ref)


def _mid_block(merged, x, kvc, w_out, wq, wo, ln_cross, ln_ffn, name, bm):
    b, s, d = x.shape
    mlen = kvc.shape[1]
    tile = pl.BlockSpec((None, bm, d), lambda bi, i: (bi, i, 0))

    def resident(shape):
        return pl.BlockSpec(shape, lambda bi, i: (0,) * len(shape), pipeline_mode=pl.Buffered(1))

    return pl.pallas_call(
        _mid_kernel,
        out_shape=(jax.ShapeDtypeStruct((b, s, d), F32), jax.ShapeDtypeStruct((b, s, d), BF16)),
        grid=(b, s // bm),
        in_specs=[
            tile, tile,
            pl.BlockSpec((None, mlen, 2 * CROSS_WIDTH), lambda bi, i: (bi, 0, 0)),
            resident(w_out.shape), resident(wq.shape), resident(wo.shape),
            resident((1, d)), resident((1, d)),
        ],
        out_specs=(tile, tile),
        compiler_params=_params(2),
        name=name,
    )(merged, x, kvc, w_out, wq, wo, ln_cross.reshape(1, d), ln_ffn.reshape(1, d))


def _block_scores(q, kk):
    return lax.dot_general(q, kk, (((1,), (1,)), ((), ())), preferred_element_type=F32)


def _block_softmax_pv(s, vv, mask):
    scale = HEAD_DIM ** -0.5
    s = jnp.where(mask, s, -jnp.inf)
    m = jnp.max(s, axis=-1, keepdims=True)
    p = jnp.exp2((s - m) * (scale * LOG2_E))
    l = jnp.sum(p, axis=-1, keepdims=True)
    o = jnp.dot(p.astype(BF16), vv, preferred_element_type=F32) * (1.0 / l)
    return o, m * scale + jnp.log(l)


ATT_BLOCKS_IN_FLIGHT = 4


def _attn_kernel(q0, k0, v0, q1, k1, v1, q2, k2, v2, o_ref, st, og, ls):
    seq = q0.shape[0]
    blk = ATT_BLOCK
    for idx, ref in enumerate((q1, k1, v1, q2, k2, v2)):
        st[idx] = ref[...].astype(F32)

    qi = lax.broadcasted_iota(jnp.int32, (blk, 2 * blk), 0)
    kj = lax.broadcasted_iota(jnp.int32, (blk, 2 * blk), 1)
    dist = qi - kj + blk
    first_mask = (lax.broadcasted_iota(jnp.int32, (blk, blk), 0) >= lax.broadcasted_iota(jnp.int32, (blk, blk), 1))

    srcs = ((q0, k0, v0), (0, 1, 2), (3, 4, 5))

    def rows(dil, r, first_blk, count):
        start = first_blk * blk * dil + r
        if dil == 1:
            return pl.ds(start, count)
        return pl.ds(start, count, stride=dil)

    def load(g, which, rws):
        if g == 0:
            return srcs[0][which][rws, :]
        return st[srcs[g][which], rws, :].astype(BF16)

    tasks = []
    for g, (window, dil) in enumerate(ATT_GROUPS):
        band_mask = (dist >= 0) & (dist <= window // dil)
        for r in range(dil):
            for n in range(seq // dil // blk):
                if n == 0:
                    tasks.append((g, rows(dil, r, 0, blk), rows(dil, r, 0, blk), first_mask))
                else:
                    tasks.append((g, rows(dil, r, n, blk), rows(dil, r, n - 1, 2 * blk), band_mask))

    def finish(batch, scores):
        for (g, q_rows, k_rows, mask), s in zip(batch, scores):
            o, lse = _block_softmax_pv(s, load(g, 2, k_rows), mask)
            og[g, q_rows, :] = o
            ls[g, q_rows, :] = jnp.broadcast_to(lse, (blk, HEAD_DIM))

    pending = None
    for t0 in range(0, len(tasks), ATT_BLOCKS_IN_FLIGHT):
        batch = tasks[t0:t0 + ATT_BLOCKS_IN_FLIGHT]
        scores = [_block_scores(load(g, 0, q_rows), load(g, 1, k_rows)) for g, q_rows, k_rows, _ in batch]
        if pending is not None:
            finish(*pending)
        pending = (batch, scores)
    finish(*pending)

    for n in range(seq // blk):
        sl = pl.ds(n * blk, blk)
        l0, l1, l2 = ls[0, sl, :], ls[1, sl, :], ls[2, sl, :]
        mx = jnp.maximum(jnp.maximum(l0, l1), l2)
        e0, e1, e2 = jnp.exp(l0 - mx), jnp.exp(l1 - mx), jnp.exp(l2 - mx)
        inv = 1.0 / (e0 + e1 + e2)
        out = (e0 * inv) * og[0, sl, :] + (e1 * inv) * og[1, sl, :] + (e2 * inv) * og[2, sl, :]
        o_ref[sl, :] = out.astype(o_ref.dtype)


def _dilated_attention(qkv, b, name):
    s = qkv.shape[1] // b
    hpg = ATT_HEADS_PER_GROUP
    n_cols = ATT_WIDTH // HEAD_DIM

    def spec(which, g):
        return pl.BlockSpec((None, s, HEAD_DIM), lambda bi, h: (which * n_cols + g * hpg + h, bi, 0))

    in_specs = [spec(which, g) for g in range(N_ATT_GROUPS) for which in range(3)]
    return pl.pallas_call(
        _attn_kernel,
        out_shape=jax.ShapeDtypeStruct((b, s, ATT_OUT_WIDTH), BF16),
        grid=(b, hpg),
        in_specs=in_specs,
        out_specs=pl.BlockSpec((None, s, HEAD_DIM), lambda bi, h: (bi, 0, h)),
        scratch_shapes=[
            pltpu.VMEM((6, s, HEAD_DIM), F32),
            pltpu.VMEM((3, s, HEAD_DIM), F32),
            pltpu.VMEM((3, s, HEAD_DIM), F32),
        ],
        compiler_params=_params(2),
        name=name,
    )(*([qkv] * 9))


def _hgrn_kernel(layer, q_ref, f_ref, i_ref, g_ref, lbp_ref, nw_ref, o_ref, state, qa_s, ka_s, v_s, gs_s, rv_s):
    ts = q_ref.shape[0]
    c = HG_CHUNK
    n_chunks = ts // c

    @pl.when(pl.program_id(1) == 0)
    def _():
        state[...] = jnp.zeros_like(state)

    a = lbp_ref[...]
    e = jnp.exp(a - jnp.max(a, axis=0, keepdims=True))
    sm = e / jnp.sum(e, axis=0, keepdims=True)
    lb = jnp.sum(sm[: layer + 1], axis=0, keepdims=True)
    f_mid = 0.5 * (1.0 + lb)
    f_half = 0.5 * (1.0 - lb)
    nw = nw_ref[...]

    ri = lax.broadcasted_iota(jnp.int32, (c, c), 0)
    ci = lax.broadcasted_iota(jnp.int32, (c, c), 1)
    causal = ri >= ci
    tri = jnp.where(causal, 1.0, 0.0).astype(BF16)

    for n in range(n_chunks):
        rows = pl.ds(n * c, c)
        f = f_mid + f_half * jnp.tanh(0.5 * f_ref[rows, :])
        lf = jnp.log(f)
        hi = lf.astype(BF16)
        r1 = lf - hi.astype(F32)
        mid = r1.astype(BF16)
        lo = (r1 - mid.astype(F32)).astype(BF16)
        bcum = (jnp.dot(tri, hi, preferred_element_type=F32) + jnp.dot(tri, mid, preferred_element_type=F32)
                + jnp.dot(tri, lo, preferred_element_type=F32))
        b_mid = bcum[c // 2:c // 2 + 1, :]
        b_last = bcum[c - 1:c, :]
        qh = q_ref[rows, :]
        gh = g_ref[rows, :]
        wide = ((qa_s, (_silu(qh) * jnp.exp(bcum - b_mid)).astype(BF16)),
                (ka_s, ((1.0 - f) * jnp.exp(b_mid - bcum)).astype(BF16)),
                (v_s, i_ref[rows, :].astype(BF16)),
                (gs_s, _silu(gh) * nw))
        for dst, val in wide:
            for h in range(HG_HEADS):
                dst[h, rows, :] = val[:, h * HG_KEY:(h + 1) * HG_KEY]
        rv_s[0, n:n + 1, :] = jnp.exp(b_mid)
        rv_s[1, n:n + 1, :] = jnp.exp(b_last - b_mid)
        rv_s[2, n:n + 1, :] = jnp.exp(b_last)

    heads = [pl.ds(h * HG_KEY, HG_KEY) for h in range(HG_HEADS)]
    for n in range(n_chunks):
        rows = pl.ds(n * c, c)
        both = []
        for h, cols in enumerate(heads):
            rhs = jnp.concatenate([(state[h] * rv_s[0, n:n + 1, cols]).astype(BF16), ka_s[h, rows, :]], axis=0)
            both.append(lax.dot_general(qa_s[h, rows, :], rhs, (((1,), (1,)), ((), ())), preferred_element_type=F32))
        kv_t = [lax.dot_general(v_s[h, rows, :], ka_s[h, rows, :], (((0,), (0,)), ((), ())), preferred_element_type=F32)
                for h in range(HG_HEADS)]
        for h, cols in enumerate(heads):
            att = jnp.where(causal, both[h][:, HG_VAL:], 0.0).astype(BF16)
            o = both[h][:, :HG_VAL] + jnp.dot(att, v_s[h, rows, :], preferred_element_type=F32)
            o = o * lax.rsqrt(jnp.mean(o * o, axis=-1, keepdims=True) + RMS_EPS) * gs_s[h, rows, :]
            o_ref[rows, cols] = o.astype(o_ref.dtype)
        for h, cols in enumerate(heads):
            state[h] = rv_s[2, n:n + 1, cols] * state[h] + kv_t[h] * rv_s[1, n:n + 1, cols]


def _hgrn2(proj, col0, lower_bounds_param, norm_w, layer, name, ts=1024):
    b, s, _ = proj.shape
    cb = col0 // HG_WIDTH

    def spec(which):
        return pl.BlockSpec((None, ts, HG_WIDTH), lambda bi, t: (bi, t, cb + which))

    n_layers = lower_bounds_param.shape[0]
    return pl.pallas_call(
        functools.partial(_hgrn_kernel, layer),
        out_shape=jax.ShapeDtypeStruct((b, s, HG_WIDTH), BF16),
        grid=(b, s // ts),
        in_specs=[spec(0), spec(1), spec(2), spec(3),
                  pl.BlockSpec((n_layers, HG_WIDTH), lambda bi, t: (0, 0)),
                  pl.BlockSpec((1, HG_WIDTH), lambda bi, t: (0, 0))],
        out_specs=pl.BlockSpec((None, ts, HG_WIDTH), lambda bi, t: (bi, t, 0)),
        scratch_shapes=[
            pltpu.VMEM((HG_HEADS, HG_VAL, HG_KEY), F32),
            pltpu.VMEM((HG_HEADS, ts, HG_KEY), BF16),
            pltpu.VMEM((HG_HEADS, ts, HG_KEY), BF16),
            pltpu.VMEM((HG_HEADS, ts, HG_VAL), BF16),
            pltpu.VMEM((HG_HEADS, ts, HG_VAL), F32),
            pltpu.VMEM((3, ts // HG_CHUNK, HG_WIDTH), F32),
        ],
        compiler_params=_params(2),
        name=name,
    )(proj, proj, proj, proj, lower_bounds_param, jnp.tile(norm_w, HG_HEADS).reshape(1, HG_WIDTH))


def kernel(x, mem, ln_mix_w, w_in, hg_norm_w, hg_lower_bounds, w_branch_a, w_branch_b, w_out, ln_cross_w, ln_mem_w, wq_cross, wkv_cross, wo_cross, ln_ffn_w, w1, w3, w2, ln_final_w):
    b, s, d = x.shape
    mlen = mem.shape[1]
    t = b * s
    depth = w_in.shape[0]
    qkv_w = 3 * ATT_WIDTH
    hg_w = 4 * HG_WIDTH

    xf = x.reshape(t, d)
    memf = mem.reshape(b * mlen, d)
    for l in range(depth):
        h = _rmsnorm(xf, ln_mix_w[l], BF16, "rms_mix")
        qkv = _matmul(h, w_in[l], 0, qkv_w, BF16, "proj_qkv", bm=2048, bn=768, head_major=True)
        hg = _matmul(h, w_in[l], qkv_w, hg_w, F32, "proj_hg", bm=2048, bn=512)
        o_att = _dilated_attention(qkv, b, "dilated_attn")
        o_hg = _hgrn2(hg.reshape(b, s, hg_w), 0, hg_lower_bounds, hg_norm_w[l], l, "hgrn2")
        merged = _merge(h, o_att.reshape(t, ATT_OUT_WIDTH), o_hg.reshape(t, HG_WIDTH), w_in[l], qkv_w + hg_w,
                        w_branch_a[l], w_branch_b[l], "gate_merge", bm=1024, bn=512)
        kvc = _norm_matmul(memf, ln_mem_w[l], wkv_cross[l], BF16, "cross_kv", bm=512, bn=512)
        x2, hf = _mid_block(merged.reshape(b, s, d), xf.reshape(b, s, d), kvc.reshape(b, mlen, 2 * CROSS_WIDTH),
                            w_out[l].astype(BF16), wq_cross[l].astype(BF16), wo_cross[l].astype(BF16),
                            ln_cross_w[l], ln_ffn_w[l], "out_proj_cross", bm=512)
        xf, hf = x2.reshape(t, d), hf.reshape(t, d)
        act = _swiglu(hf, w1[l], w3[l], "ffn_up", bm=2048, bn=512)
        if l + 1 < depth:
            xf = _matmul_residual(act, w2[l].astype(BF16), xf, "ffn_down", bm=512, bn=512)
        else:
            out = _matmul_residual_norm(act, w2[l].astype(BF16), xf, ln_final_w, F32, False, "ffn_down_final", bm=512,
                                        single_buffer_w=True)
    return out.reshape(b, s, d)
```

```python
import functools

import jax
import jax.numpy as jnp
from jax import lax
from jax.experimental import pallas as pl
from jax.experimental.pallas import tpu as pltpu

F32 = jnp.float32
BF16 = jnp.bfloat16

HEAD_DIM = 128
ATT_GROUPS = ((128, 1), (512, 4), (2048, 16))
ATT_HEADS_PER_GROUP = 4
N_ATT_GROUPS = len(ATT_GROUPS)
ATT_WIDTH = N_ATT_GROUPS * ATT_HEADS_PER_GROUP * HEAD_DIM
ATT_OUT_WIDTH = ATT_HEADS_PER_GROUP * HEAD_DIM
ATT_BLOCK = 128
HG_HEADS = 8
HG_KEY = 128
HG_VAL = 128
HG_WIDTH = HG_HEADS * HG_KEY
HG_CHUNK = 64
CROSS_HEADS = 4
CROSS_WIDTH = CROSS_HEADS * HEAD_DIM
RMS_EPS = 1e-6
LOG2_E = 1.4426950408889634

V7X_VMEM_LIMIT_BYTES = 56 * 1024 * 1024
MXU_ROW_TILE = 256
MXU_COL_TILE = 512
BF16_SUBLANES = 16


def _sigmoid(x):
    return 0.5 * jnp.tanh(0.5 * x) + 0.5


def _silu(x):
    hx = 0.5 * x
    return hx * jnp.tanh(hx) + hx


def _params(n_axes):
    return pltpu.CompilerParams(
        dimension_semantics=("arbitrary",) * n_axes,
        vmem_limit_bytes=V7X_VMEM_LIMIT_BYTES,
    )


def _rms_kernel(x_ref, w_ref, o_ref):
    x = x_ref[...]
    ms = jnp.mean(x * x, axis=-1, keepdims=True)
    o_ref[...] = (x * lax.rsqrt(ms + RMS_EPS) * w_ref[...]).astype(o_ref.dtype)


def _rmsnorm(x, w, out_dtype, name, bm=512):
    m, d = x.shape
    return pl.pallas_call(
        _rms_kernel,
        out_shape=jax.ShapeDtypeStruct((m, d), out_dtype),
        grid=(m // bm,),
        in_specs=[pl.BlockSpec((bm, d), lambda i: (i, 0)), pl.BlockSpec((1, d), lambda i: (0, 0))],
        out_specs=pl.BlockSpec((bm, d), lambda i: (i, 0)),
        compiler_params=_params(1),
        name=name,
    )(x, w.reshape(1, d))


def _row_tiles(n_rows):
    tile = min(MXU_ROW_TILE, n_rows)
    return [pl.ds(r * tile, tile) for r in range(n_rows // tile)]


def _col_tiles(n_cols):
    tile = min(MXU_COL_TILE, n_cols)
    return [pl.ds(c * tile, tile) for c in range(n_cols // tile)]


def _mm_kernel(a_ref, w_ref, o_ref, w_s):
    w_s[...] = w_ref[...].astype(BF16)
    for rows in _row_tiles(a_ref.shape[0]):
        res = jnp.dot(a_ref[rows, :], w_s[...], preferred_element_type=F32).astype(o_ref.dtype)
        if len(o_ref.shape) == 2:
            o_ref[rows, :] = res
        else:
            for hd in range(o_ref.shape[0]):
                o_ref[hd, rows, :] = res[:, hd * HEAD_DIM:(hd + 1) * HEAD_DIM]


def _matmul(a, w, col0, n, out_dtype, name, bm, bn, head_major=False):
    m, k = a.shape
    cb = col0 // bn
    if head_major:
        out_shape = jax.ShapeDtypeStruct((n // HEAD_DIM, m, HEAD_DIM), out_dtype)
        out_spec = pl.BlockSpec((bn // HEAD_DIM, bm, HEAD_DIM), lambda i, j: (j, i, 0))
    else:
        out_shape = jax.ShapeDtypeStruct((m, n), out_dtype)
        out_spec = pl.BlockSpec((bm, bn), lambda i, j: (i, j))
    return pl.pallas_call(
        _mm_kernel,
        out_shape=out_shape,
        grid=(m // bm, n // bn),
        in_specs=[
            pl.BlockSpec((bm, k), lambda i, j: (i, 0)),
            pl.BlockSpec((k, bn), lambda i, j: (0, j + cb)),
        ],
        out_specs=out_spec,
        scratch_shapes=[pltpu.VMEM((k, bn), BF16)],
        compiler_params=_params(2),
        name=name,
    )(a, w)


def _norm_mm_kernel(x_ref, lnw_ref, w_ref, o_ref):
    x = x_ref[...]
    h = (x * lax.rsqrt(jnp.mean(x * x, axis=-1, keepdims=True) + RMS_EPS) * lnw_ref[...]).astype(BF16)
    o_ref[...] = jnp.dot(h, w_ref[...].astype(BF16), preferred_element_type=F32).astype(o_ref.dtype)


def _norm_matmul(x, lnw, w, out_dtype, name, bm, bn):
    m, k = x.shape
    n = w.shape[1]
    return pl.pallas_call(
        _norm_mm_kernel,
        out_shape=jax.ShapeDtypeStruct((m, n), out_dtype),
        grid=(m // bm, n // bn),
        in_specs=[
            pl.BlockSpec((bm, k), lambda i, j: (i, 0)),
            pl.BlockSpec((1, k), lambda i, j: (0, 0)),
            pl.BlockSpec((k, bn), lambda i, j: (0, j)),
        ],
        out_specs=pl.BlockSpec((bm, bn), lambda i, j: (i, j)),
        compiler_params=_params(2),
        name=name,
    )(x, lnw.reshape(1, k), w)


def _mm_res_kernel(a_ref, w_ref, r_ref, o_ref):
    o_ref[...] = r_ref[...] + jnp.dot(a_ref[...], w_ref[...], preferred_element_type=F32)


def _matmul_residual(a, w, res, name, bm, bn):
    m, k = a.shape
    n = w.shape[1]
    return pl.pallas_call(
        _mm_res_kernel,
        out_shape=jax.ShapeDtypeStruct((m, n), F32),
        grid=(m // bm, n // bn),
        in_specs=[
            pl.BlockSpec((bm, k), lambda i, j: (i, 0)),
            pl.BlockSpec((k, bn), lambda i, j: (0, j)),
            pl.BlockSpec((bm, bn), lambda i, j: (i, j)),
        ],
        out_specs=pl.BlockSpec((bm, bn), lambda i, j: (i, j)),
        compiler_params=_params(2),
        name=name,
    )(a, w, res)


def _mm_res_norm_kernel(emit_sum, a_ref, w_ref, r_ref, lnw_ref, *out_refs):
    n = w_ref.shape[1]
    for rows in _row_tiles(a_ref.shape[0]):
        a = a_ref[rows, :]
        ys = [r_ref[rows, cols] + jnp.dot(a, w_ref[:, cols], preferred_element_type=F32) for cols in _col_tiles(n)]
        ssq = sum(jnp.sum(y * y, axis=-1, keepdims=True) for y in ys)
        scale = lax.rsqrt(ssq * (1.0 / n) + RMS_EPS)
        for cols, y in zip(_col_tiles(n), ys):
            if emit_sum:
                out_refs[0][rows, cols] = y
            out_refs[-1][rows, cols] = (y * scale * lnw_ref[:, cols]).astype(out_refs[-1].dtype)


def _matmul_residual_norm(a, w, res, lnw, norm_dtype, emit_sum, name, bm, single_buffer_w=False):
    m, k = a.shape
    n = w.shape[1]
    row = lambda i: (i, 0)
    w_spec = (pl.BlockSpec((k, n), lambda i: (0, 0), pipeline_mode=pl.Buffered(1)) if single_buffer_w
              else pl.BlockSpec((k, n), lambda i: (0, 0)))
    norm_shape = jax.ShapeDtypeStruct((m, n), norm_dtype)
    out_shape = (jax.ShapeDtypeStruct((m, n), F32), norm_shape) if emit_sum else norm_shape
    out_specs = (pl.BlockSpec((bm, n), row), pl.BlockSpec((bm, n), row)) if emit_sum else pl.BlockSpec((bm, n), row)
    return pl.pallas_call(
        functools.partial(_mm_res_norm_kernel, emit_sum),
        out_shape=out_shape,
        grid=(m // bm,),
        in_specs=[pl.BlockSpec((bm, k), row), w_spec, pl.BlockSpec((bm, n), row), pl.BlockSpec((1, n), lambda i: (0, 0))],
        out_specs=out_specs,
        compiler_params=_params(1),
        name=name,
    )(a, w, res, lnw.reshape(1, n))


def _convert_riders(rider_in, rider_out):
    for src, dst in zip(rider_in, rider_out):
        dst[...] = src[...].astype(dst.dtype)


def _rider_specs(riders, n_steps, step_of):
    specs, shapes = [], []
    for w in riders:
        rows, cols = w.shape
        assert rows % n_steps == 0 and (rows // n_steps) % BF16_SUBLANES == 0, (w.shape, n_steps)
        specs.append(pl.BlockSpec((rows // n_steps, cols), lambda i, j: (step_of(i, j), 0)))
        shapes.append(jax.ShapeDtypeStruct((rows, cols), BF16))
    return specs, shapes


def _swiglu_kernel(n_riders, a_ref, w1_ref, w3_ref, *rest):
    rider_in, (o_ref, *rider_out), (w1_s, w3_s) = rest[:n_riders], rest[n_riders:2 * n_riders + 1], rest[2 * n_riders + 1:]
    _convert_riders(rider_in, rider_out)
    w1_s[...] = w1_ref[...].astype(BF16)
    w3_s[...] = w3_ref[...].astype(BF16)
    for rows in _row_tiles(a_ref.shape[0]):
        a = a_ref[rows, :]
        u = jnp.dot(a, w1_s[...], preferred_element_type=F32)
        g = jnp.dot(a, w3_s[...], preferred_element_type=F32)
        o_ref[rows, :] = (_silu(u) * g).astype(o_ref.dtype)


def _swiglu(a, w1, w3, name, bm, bn, riders=()):
    m, k = a.shape
    n = w1.shape[1]
    nj = n // bn
    rider_specs, rider_shapes = _rider_specs(riders, (m // bm) * nj, lambda i, j: i * nj + j)
    outs = pl.pallas_call(
        functools.partial(_swiglu_kernel, len(riders)),
        scratch_shapes=[pltpu.VMEM((k, bn), BF16), pltpu.VMEM((k, bn), BF16)],
        out_shape=[jax.ShapeDtypeStruct((m, n), BF16)] + rider_shapes,
        grid=(m // bm, nj),
        in_specs=[
            pl.BlockSpec((bm, k), lambda i, j: (i, 0)),
            pl.BlockSpec((k, bn), lambda i, j: (0, j)),
            pl.BlockSpec((k, bn), lambda i, j: (0, j)),
        ] + rider_specs,
        out_specs=[pl.BlockSpec((bm, bn), lambda i, j: (i, j))] + rider_specs,
        compiler_params=_params(2),
        name=name,
    )(a, w1, w3, *riders)
    return outs[0], tuple(outs[1:])


def _merge_kernel(n_riders, h_ref, oa_ref, oh_ref, wga_ref, wgb_ref, wa_ref, wb_ref, *rest):
    rider_in, (o_ref, *rider_out) = rest[:n_riders], rest[n_riders:2 * n_riders + 1]
    wga_s, wgb_s, wa_s, wb_s = rest[2 * n_riders + 1:]
    _convert_riders(rider_in, rider_out)
    for src, dst in ((wga_ref, wga_s), (wgb_ref, wgb_s), (wa_ref, wa_s), (wb_ref, wb_s)):
        dst[...] = src[...].astype(BF16)
    for rows in _row_tiles(h_ref.shape[0]):
        h = h_ref[rows, :]
        ga = jnp.dot(h, wga_s[...], preferred_element_type=F32)
        gb = jnp.dot(h, wgb_s[...], preferred_element_type=F32)
        pa = jnp.dot(oa_ref[rows, :], wa_s[...], preferred_element_type=F32)
        pb = jnp.dot(oh_ref[rows, :], wb_s[...], preferred_element_type=F32)
        o_ref[rows, :] = (_sigmoid(ga) * pa + _sigmoid(gb) * pb).astype(o_ref.dtype)


def _merge(h, o_att, o_hg, w_in, gate_col0, wa, wb, name, bm, bn, riders=()):
    m, k = h.shape
    d = wa.shape[1]
    ga0 = gate_col0 // bn
    gb0 = (gate_col0 + d) // bn
    nj = d // bn
    rider_specs, rider_shapes = _rider_specs(riders, (m // bm) * nj, lambda i, j: i * nj + j)
    outs = pl.pallas_call(
        functools.partial(_merge_kernel, len(riders)),
        out_shape=[jax.ShapeDtypeStruct((m, d), BF16)] + rider_shapes,
        grid=(m // bm, nj),
        in_specs=[
            pl.BlockSpec((bm, k), lambda i, j: (i, 0)),
            pl.BlockSpec((bm, o_att.shape[1]), lambda i, j: (i, 0)),
            pl.BlockSpec((bm, o_hg.shape[1]), lambda i, j: (i, 0)),
            pl.BlockSpec((k, bn), lambda i, j: (0, j + ga0)),
            pl.BlockSpec((k, bn), lambda i, j: (0, j + gb0)),
            pl.BlockSpec((wa.shape[0], bn), lambda i, j: (0, j)),
            pl.BlockSpec((wb.shape[0], bn), lambda i, j: (0, j)),
        ] + rider_specs,
        out_specs=[pl.BlockSpec((bm, bn), lambda i, j: (i, j))] + rider_specs,
        scratch_shapes=[pltpu.VMEM((k, bn), BF16), pltpu.VMEM((k, bn), BF16),
                        pltpu.VMEM((wa.shape[0], bn), BF16), pltpu.VMEM((wb.shape[0], bn), BF16)],
        compiler_params=_params(2),
        name=name,
    )(h, o_att, o_hg, w_in, w_in, wa, wb, *riders)
    return outs[0], tuple(outs[1:])


def _mid_kernel(mg_ref, x_ref, kv_ref, wout_ref, wq_ref, wo_ref, lnc_ref, lnf_ref, x2_ref, hf_ref):
    d = x_ref.shape[1]
    row_tiles = _row_tiles(x_ref.shape[0])
    col_tiles = _col_tiles(d)

    def residual_norm(rows, lhs, w_ref, base_ref, ln_ref):
        ys = [base_ref[rows, cols] + jnp.dot(lhs, w_ref[:, cols], preferred_element_type=F32) for cols in col_tiles]
        ssq = sum(jnp.sum(y * y, axis=-1, keepdims=True) for y in ys)
        scale = lax.rsqrt(ssq * (1.0 / d) + RMS_EPS)
        for cols, y in zip(col_tiles, ys):
            x2_ref[rows, cols] = y
            hf_ref[rows, cols] = (y * scale * ln_ref[:, cols]).astype(hf_ref.dtype)

    for rows in row_tiles:
        residual_norm(rows, mg_ref[rows, :], wout_ref, x_ref, lnc_ref)
    qcs = [jnp.dot(hf_ref[rows, :], wq_ref[...], preferred_element_type=F32).astype(BF16) for rows in row_tiles]
    scores = [[lax.dot_general(qc[:, h * HEAD_DIM:(h + 1) * HEAD_DIM], kv_ref[:, h * HEAD_DIM:(h + 1) * HEAD_DIM],
                               (((1,), (1,)), ((), ())), preferred_element_type=F32) * (HEAD_DIM ** -0.5)
               for h in range(CROSS_HEADS)] for qc in qcs]
    ocs = []
    for per_head in scores:
        outs = []
        for h, s in enumerate(per_head):
            m = jnp.max(s, axis=-1, keepdims=True)
            p = jnp.exp(s - m)
            pn = (p * (1.0 / jnp.sum(p, axis=-1, keepdims=True))).astype(BF16)
            v = kv_ref[:, CROSS_WIDTH + h * HEAD_DIM:CROSS_WIDTH + (h + 1) * HEAD_DIM]
            outs.append(jnp.dot(pn, v, preferred_element_type=F32).astype(BF16))
        ocs.append(jnp.concatenate(outs, axis=-1))
    for rows, oc in zip(row_tiles, ocs):
        residual_norm(rows, oc, wo_ref, x2_ref, lnf_ref)


def _mid_block(merged, x, kvc, w_out, wq, wo, ln_cross, ln_ffn, name, bm):
    b, s, d = x.shape
    mlen = kvc.shape[1]
    tile = pl.BlockSpec((None, bm, d), lambda bi, i: (bi, i, 0))

    def resident(shape):
        return pl.BlockSpec(shape, lambda bi, i: (0,) * len(shape), pipeline_mode=pl.Buffered(1))

    return pl.pallas_call(
        _mid_kernel,
        out_shape=(jax.ShapeDtypeStruct((b, s, d), F32), jax.ShapeDtypeStruct((b, s, d), BF16)),
        grid=(b, s // bm),
        in_specs=[
            tile, tile,
            pl.BlockSpec((None, mlen, 2 * CROSS_WIDTH), lambda bi, i: (bi, 0, 0)),
            resident(w_out.shape), resident(wq.shape), resident(wo.shape),
            resident((1, d)), resident((1, d)),
        ],
        out_specs=(tile, tile),
        compiler_params=_params(2),
        name=name,
    )(merged, x, kvc, w_out, wq, wo, ln_cross.reshape(1, d), ln_ffn.reshape(1, d))


def _block_scores(q, kk):
    return lax.dot_general(q, kk, (((1,), (1,)), ((), ())), preferred_element_type=F32)


def _block_softmax_pv(s, vv, mask):
    scale = HEAD_DIM ** -0.5
    s = jnp.where(mask, s, -jnp.inf)
    m = jnp.max(s, axis=-1, keepdims=True)
    p = jnp.exp2((s - m) * (scale * LOG2_E))
    l = jnp.sum(p, axis=-1, keepdims=True)
    o = jnp.dot(p.astype(BF16), vv, preferred_element_type=F32) * (1.0 / l)
    return o, m * scale + jnp.log(l)


ATT_BLOCKS_IN_FLIGHT = 4


def _attn_kernel(q0, k0, v0, q1, k1, v1, q2, k2, v2, o_ref, st, og, ls):
    seq = q0.shape[0]
    blk = ATT_BLOCK
    for idx, ref in enumerate((q1, k1, v1, q2, k2, v2)):
        st[idx] = ref[...].astype(F32)

    qi = lax.broadcasted_iota(jnp.int32, (blk, 2 * blk), 0)
    kj = lax.broadcasted_iota(jnp.int32, (blk, 2 * blk), 1)
    dist = qi - kj + blk
    first_mask = (lax.broadcasted_iota(jnp.int32, (blk, blk), 0) >= lax.broadcasted_iota(jnp.int32, (blk, blk), 1))

    srcs = ((q0, k0, v0), (0, 1, 2), (3, 4, 5))

    def rows(dil, r, first_blk, count):
        start = first_blk * blk * dil + r
        if dil == 1:
            return pl.ds(start, count)
        return pl.ds(start, count, stride=dil)

    def load(g, which, rws):
        if g == 0:
            return srcs[0][which][rws, :]
        return st[srcs[g][which], rws, :].astype(BF16)

    tasks = []
    for g, (window, dil) in enumerate(ATT_GROUPS):
        band_mask = (dist >= 0) & (dist <= window // dil)
        for r in range(dil):
            for n in range(seq // dil // blk):
                if n == 0:
                    tasks.append((g, rows(dil, r, 0, blk), rows(dil, r, 0, blk), first_mask))
                else:
                    tasks.append((g, rows(dil, r, n, blk), rows(dil, r, n - 1, 2 * blk), band_mask))

    def finish(batch, scores):
        for (g, q_rows, k_rows, mask), s in zip(batch, scores):
            o, lse = _block_softmax_pv(s, load(g, 2, k_rows), mask)
            og[g, q_rows, :] = o
            ls[g, q_rows, :] = jnp.broadcast_to(lse, (blk, HEAD_DIM))

    pending = None
    for t0 in range(0, len(tasks), ATT_BLOCKS_IN_FLIGHT):
        batch = tasks[t0:t0 + ATT_BLOCKS_IN_FLIGHT]
        scores = [_block_scores(load(g, 0, q_rows), load(g, 1, k_rows)) for g, q_rows, k_rows, _ in batch]
        if pending is not None:
            finish(*pending)
        pending = (batch, scores)
    finish(*pending)

    for n in range(seq // blk):
        sl = pl.ds(n * blk, blk)
        l0, l1, l2 = ls[0, sl, :], ls[1, sl, :], ls[2, sl, :]
        mx = jnp.maximum(jnp.maximum(l0, l1), l2)
        e0, e1, e2 = jnp.exp(l0 - mx), jnp.exp(l1 - mx), jnp.exp(l2 - mx)
        inv = 1.0 / (e0 + e1 + e2)
        out = (e0 * inv) * og[0, sl, :] + (e1 * inv) * og[1, sl, :] + (e2 * inv) * og[2, sl, :]
        o_ref[sl, :] = out.astype(o_ref.dtype)


def _dilated_attention(qkv, b, name):
    s = qkv.shape[1] // b
    hpg = ATT_HEADS_PER_GROUP
    n_cols = ATT_WIDTH // HEAD_DIM

    def spec(which, g):
        return pl.BlockSpec((None, s, HEAD_DIM), lambda bi, h: (which * n_cols + g * hpg + h, bi, 0))

    in_specs = [spec(which, g) for g in range(N_ATT_GROUPS) for which in range(3)]
    return pl.pallas_call(
        _attn_kernel,
        out_shape=jax.ShapeDtypeStruct((b, s, ATT_OUT_WIDTH), BF16),
        grid=(b, hpg),
        in_specs=in_specs,
        out_specs=pl.BlockSpec((None, s, HEAD_DIM), lambda bi, h: (bi, 0, h)),
        scratch_shapes=[
            pltpu.VMEM((6, s, HEAD_DIM), F32),
            pltpu.VMEM((3, s, HEAD_DIM), F32),
            pltpu.VMEM((3, s, HEAD_DIM), F32),
        ],
        compiler_params=_params(2),
        name=name,
    )(*([qkv] * 9))


def _hgrn_kernel(layer, q_ref, f_ref, i_ref, g_ref, lbp_ref, nw_ref, o_ref, state, qa_s, ka_s, v_s, gs_s, rv_s):
    ts = q_ref.shape[0]
    c = HG_CHUNK
    n_chunks = ts // c

    @pl.when(pl.program_id(1) == 0)
    def _():
        state[...] = jnp.zeros_like(state)

    a = lbp_ref[...]
    e = jnp.exp(a - jnp.max(a, axis=0, keepdims=True))
    sm = e / jnp.sum(e, axis=0, keepdims=True)
    lb = jnp.sum(sm[: layer + 1], axis=0, keepdims=True)
    f_mid = 0.5 * (1.0 + lb)
    f_half = 0.5 * (1.0 - lb)
    nw = nw_ref[...]

    ri = lax.broadcasted_iota(jnp.int32, (c, c), 0)
    ci = lax.broadcasted_iota(jnp.int32, (c, c), 1)
    causal = ri >= ci
    tri = jnp.where(causal, 1.0, 0.0).astype(BF16)

    for n in range(n_chunks):
        rows = pl.ds(n * c, c)
        f = f_mid + f_half * jnp.tanh(0.5 * f_ref[rows, :])
        lf = jnp.log(f)
        hi = lf.astype(BF16)
        r1 = lf - hi.astype(F32)
        mid = r1.astype(BF16)
        lo = (r1 - mid.astype(F32)).astype(BF16)
        bcum = (jnp.dot(tri, hi, preferred_element_type=F32) + jnp.dot(tri, mid, preferred_element_type=F32)
                + jnp.dot(tri, lo, preferred_element_type=F32))
        b_mid = bcum[c // 2:c // 2 + 1, :]
        b_last = bcum[c - 1:c, :]
        qh = q_ref[rows, :]
        gh = g_ref[rows, :]
        wide = ((qa_s, (_silu(qh) * jnp.exp(bcum - b_mid)).astype(BF16)),
                (ka_s, ((1.0 - f) * jnp.exp(b_mid - bcum)).astype(BF16)),
                (v_s, i_ref[rows, :].astype(BF16)),
                (gs_s, _silu(gh) * nw))
        for dst, val in wide:
            for h in range(HG_HEADS):
                dst[h, rows, :] = val[:, h * HG_KEY:(h + 1) * HG_KEY]
        rv_s[0, n:n + 1, :] = jnp.exp(b_mid)
        rv_s[1, n:n + 1, :] = jnp.exp(b_last - b_mid)
        rv_s[2, n:n + 1, :] = jnp.exp(b_last)

    heads = [pl.ds(h * HG_KEY, HG_KEY) for h in range(HG_HEADS)]
    for n in range(n_chunks):
        rows = pl.ds(n * c, c)
        both = []
        for h, cols in enumerate(heads):
            rhs = jnp.concatenate([(state[h] * rv_s[0, n:n + 1, cols]).astype(BF16), ka_s[h, rows, :]], axis=0)
            both.append(lax.dot_general(qa_s[h, rows, :], rhs, (((1,), (1,)), ((), ())), preferred_element_type=F32))
        kv_t = [lax.dot_general(v_s[h, rows, :], ka_s[h, rows, :], (((0,), (0,)), ((), ())), preferred_element_type=F32)
                for h in range(HG_HEADS)]
        for h, cols in enumerate(heads):
            att = jnp.where(causal, both[h][:, HG_VAL:], 0.0).astype(BF16)
            o = both[h][:, :HG_VAL] + jnp.dot(att, v_s[h, rows, :], preferred_element_type=F32)
            o = o * lax.rsqrt(jnp.mean(o * o, axis=-1, keepdims=True) + RMS_EPS) * gs_s[h, rows, :]
            o_ref[rows, cols] = o.astype(o_ref.dtype)
        for h, cols in enumerate(heads):
            state[h] = rv_s[2, n:n + 1, cols] * state[h] + kv_t[h] * rv_s[1, n:n + 1, cols]


def _hgrn2(proj, col0, lower_bounds_param, norm_w, layer, name, ts=1024):
    b, s, _ = proj.shape
    cb = col0 // HG_WIDTH

    def spec(which):
        return pl.BlockSpec((None, ts, HG_WIDTH), lambda bi, t: (bi, t, cb + which))

    n_layers = lower_bounds_param.shape[0]
    return pl.pallas_call(
        functools.partial(_hgrn_kernel, layer),
        out_shape=jax.ShapeDtypeStruct((b, s, HG_WIDTH), BF16),
        grid=(b, s // ts),
        in_specs=[spec(0), spec(1), spec(2), spec(3),
                  pl.BlockSpec((n_layers, HG_WIDTH), lambda bi, t: (0, 0)),
                  pl.BlockSpec((1, HG_WIDTH), lambda bi, t: (0, 0))],
        out_specs=pl.BlockSpec((None, ts, HG_WIDTH), lambda bi, t: (bi, t, 0)),
        scratch_shapes=[
            pltpu.VMEM((HG_HEADS, HG_VAL, HG_KEY), F32),
            pltpu.VMEM((HG_HEADS, ts, HG_KEY), BF16),
            pltpu.VMEM((HG_HEADS, ts, HG_KEY), BF16),
            pltpu.VMEM((HG_HEADS, ts, HG_VAL), BF16),
            pltpu.VMEM((HG_HEADS, ts, HG_VAL), F32),
            pltpu.VMEM((3, ts // HG_CHUNK, HG_WIDTH), F32),
        ],
        compiler_params=_params(2),
        name=name,
    )(proj, proj, proj, proj, lower_bounds_param, jnp.tile(norm_w, HG_HEADS).reshape(1, HG_WIDTH))


def kernel(x, mem, ln_mix_w, w_in, hg_norm_w, hg_lower_bounds, w_branch_a, w_branch_b, w_out, ln_cross_w, ln_mem_w, wq_cross, wkv_cross, wo_cross, ln_ffn_w, w1, w3, w2, ln_final_w):
    b, s, d = x.shape
    mlen = mem.shape[1]
    t = b * s
    depth = w_in.shape[0]
    qkv_w = 3 * ATT_WIDTH
    hg_w = 4 * HG_WIDTH

    xf = x.reshape(t, d)
    memf = mem.reshape(b * mlen, d)
    for l in range(depth):
        h = _rmsnorm(xf, ln_mix_w[l], BF16, "rms_mix")
        qkv = _matmul(h, w_in[l], 0, qkv_w, BF16, "proj_qkv", bm=2048, bn=768, head_major=True)
        hg = _matmul(h, w_in[l], qkv_w, hg_w, F32, "proj_hg", bm=2048, bn=512)
        o_att = _dilated_attention(qkv, b, "dilated_attn")
        o_hg = _hgrn2(hg.reshape(b, s, hg_w), 0, hg_lower_bounds, hg_norm_w[l], l, "hgrn2")
        merged, (w_out_b, wq_b, wo_b) = _merge(
            h, o_att.reshape(t, ATT_OUT_WIDTH), o_hg.reshape(t, HG_WIDTH), w_in[l], qkv_w + hg_w,
            w_branch_a[l], w_branch_b[l], "gate_merge", bm=1024, bn=512, riders=(w_out[l], wq_cross[l], wo_cross[l]))
        kvc = _norm_matmul(memf, ln_mem_w[l], wkv_cross[l], BF16, "cross_kv", bm=512, bn=512)
        x2, hf = _mid_block(merged.reshape(b, s, d), xf.reshape(b, s, d), kvc.reshape(b, mlen, 2 * CROSS_WIDTH),
                            w_out_b, wq_b, wo_b, ln_cross_w[l], ln_ffn_w[l], "out_proj_cross", bm=512)
        xf, hf = x2.reshape(t, d), hf.reshape(t, d)
        act, (w2_b,) = _swiglu(hf, w1[l], w3[l], "ffn_up", bm=2048, bn=512, riders=(w2[l],))
        if l + 1 < depth:
            xf = _matmul_residual(act, w2_b, xf, "ffn_down", bm=512, bn=512)
        else:
            out = _matmul_residual_norm(act, w2_b, xf, ln_final_w, F32, False, "ffn_down_final", bm=512,
                                        single_buffer_w=True)
    return out.reshape(b, s, d)
```

```python
import functools

import jax
import jax.numpy as jnp
from jax import lax
from jax.experimental import pallas as pl
from jax.experimental.pallas import tpu as pltpu

F32 = jnp.float32
BF16 = jnp.bfloat16

HEAD_DIM = 128
ATT_GROUPS = ((128, 1), (512, 4), (2048, 16))
ATT_HEADS_PER_GROUP = 4
N_ATT_GROUPS = len(ATT_GROUPS)
ATT_WIDTH = N_ATT_GROUPS * ATT_HEADS_PER_GROUP * HEAD_DIM
ATT_OUT_WIDTH = ATT_HEADS_PER_GROUP * HEAD_DIM
ATT_BLOCK = 128
HG_HEADS = 8
HG_KEY = 128
HG_VAL = 128
HG_WIDTH = HG_HEADS * HG_KEY
HG_CHUNK = 64
CROSS_HEADS = 4
CROSS_WIDTH = CROSS_HEADS * HEAD_DIM
RMS_EPS = 1e-6
LOG2_E = 1.4426950408889634

V7X_VMEM_LIMIT_BYTES = 56 * 1024 * 1024
MXU_ROW_TILE = 256
MXU_COL_TILE = 512
BF16_SUBLANES = 16


def _sigmoid(x):
    return 0.5 * jnp.tanh(0.5 * x) + 0.5


def _silu(x):
    hx = 0.5 * x
    return hx * jnp.tanh(hx) + hx


def _params(n_axes):
    return pltpu.CompilerParams(
        dimension_semantics=("arbitrary",) * n_axes,
        vmem_limit_bytes=V7X_VMEM_LIMIT_BYTES,
    )


def _rms_kernel(x_ref, w_ref, o_ref):
    x = x_ref[...]
    ms = jnp.mean(x * x, axis=-1, keepdims=True)
    o_ref[...] = (x * lax.rsqrt(ms + RMS_EPS) * w_ref[...]).astype(o_ref.dtype)


def _rmsnorm(x, w, out_dtype, name, bm=512):
    m, d = x.shape
    return pl.pallas_call(
        _rms_kernel,
        out_shape=jax.ShapeDtypeStruct((m, d), out_dtype),
        grid=(m // bm,),
        in_specs=[pl.BlockSpec((bm, d), lambda i: (i, 0)), pl.BlockSpec((1, d), lambda i: (0, 0))],
        out_specs=pl.BlockSpec((bm, d), lambda i: (i, 0)),
        compiler_params=_params(1),
        name=name,
    )(x, w.reshape(1, d))


def _row_tiles(n_rows):
    tile = min(MXU_ROW_TILE, n_rows)
    return [pl.ds(r * tile, tile) for r in range(n_rows // tile)]


def _col_tiles(n_cols):
    tile = min(MXU_COL_TILE, n_cols)
    return [pl.ds(c * tile, tile) for c in range(n_cols // tile)]


def _mm_kernel(a_ref, w_ref, o_ref, w_s):
    w_s[...] = w_ref[...].astype(BF16)
    for rows in _row_tiles(a_ref.shape[0]):
        res = jnp.dot(a_ref[rows, :], w_s[...], preferred_element_type=F32).astype(o_ref.dtype)
        if len(o_ref.shape) == 2:
            o_ref[rows, :] = res
        else:
            for hd in range(o_ref.shape[0]):
                o_ref[hd, rows, :] = res[:, hd * HEAD_DIM:(hd + 1) * HEAD_DIM]


def _matmul(a, w, col0, n, out_dtype, name, bm, bn, head_major=False):
    m, k = a.shape
    cb = col0 // bn
    if head_major:
        out_shape = jax.ShapeDtypeStruct((n // HEAD_DIM, m, HEAD_DIM), out_dtype)
        out_spec = pl.BlockSpec((bn // HEAD_DIM, bm, HEAD_DIM), lambda i, j: (j, i, 0))
    else:
        out_shape = jax.ShapeDtypeStruct((m, n), out_dtype)
        out_spec = pl.BlockSpec((bm, bn), lambda i, j: (i, j))
    return pl.pallas_call(
        _mm_kernel,
        out_shape=out_shape,
        grid=(m // bm, n // bn),
        in_specs=[
            pl.BlockSpec((bm, k), lambda i, j: (i, 0)),
            pl.BlockSpec((k, bn), lambda i, j: (0, j + cb)),
        ],
        out_specs=out_spec,
        scratch_shapes=[pltpu.VMEM((k, bn), BF16)],
        compiler_params=_params(2),
        name=name,
    )(a, w)


def _norm_mm_kernel(x_ref, lnw_ref, w_ref, o_ref):
    x = x_ref[...]
    h = (x * lax.rsqrt(jnp.mean(x * x, axis=-1, keepdims=True) + RMS_EPS) * lnw_ref[...]).astype(BF16)
    o_ref[...] = jnp.dot(h, w_ref[...].astype(BF16), preferred_element_type=F32).astype(o_ref.dtype)


def _norm_matmul(x, lnw, w, out_dtype, name, bm, bn):
    m, k = x.shape
    n = w.shape[1]
    return pl.pallas_call(
        _norm_mm_kernel,
        out_shape=jax.ShapeDtypeStruct((m, n), out_dtype),
        grid=(m // bm, n // bn),
        in_specs=[
            pl.BlockSpec((bm, k), lambda i, j: (i, 0)),
            pl.BlockSpec((1, k), lambda i, j: (0, 0)),
            pl.BlockSpec((k, bn), lambda i, j: (0, j)),
        ],
        out_specs=pl.BlockSpec((bm, bn), lambda i, j: (i, j)),
        compiler_params=_params(2),
        name=name,
    )(x, lnw.reshape(1, k), w)


def _mm_res_kernel(a_ref, w_ref, r_ref, o_ref):
    o_ref[...] = r_ref[...] + jnp.dot(a_ref[...], w_ref[...], preferred_element_type=F32)


def _matmul_residual(a, w, res, name, bm, bn):
    m, k = a.shape
    n = w.shape[1]
    return pl.pallas_call(
        _mm_res_kernel,
        out_shape=jax.ShapeDtypeStruct((m, n), F32),
        grid=(m // bm, n // bn),
        in_specs=[
            pl.BlockSpec((bm, k), lambda i, j: (i, 0)),
            pl.BlockSpec((k, bn), lambda i, j: (0, j)),
            pl.BlockSpec((bm, bn), lambda i, j: (i, j)),
        ],
        out_specs=pl.BlockSpec((bm, bn), lambda i, j: (i, j)),
        compiler_params=_params(2),
        name=name,
    )(a, w, res)


def _mm_res_norm_kernel(emit_sum, a_ref, w_ref, r_ref, lnw_ref, *out_refs):
    n = w_ref.shape[1]
    for rows in _row_tiles(a_ref.shape[0]):
        a = a_ref[rows, :]
        ys = [r_ref[rows, cols] + jnp.dot(a, w_ref[:, cols], preferred_element_type=F32) for cols in _col_tiles(n)]
        ssq = sum(jnp.sum(y * y, axis=-1, keepdims=True) for y in ys)
        scale = lax.rsqrt(ssq * (1.0 / n) + RMS_EPS)
        for cols, y in zip(_col_tiles(n), ys):
            if emit_sum:
                out_refs[0][rows, cols] = y
            out_refs[-1][rows, cols] = (y * scale * lnw_ref[:, cols]).astype(out_refs[-1].dtype)


def _matmul_residual_norm(a, w, res, lnw, norm_dtype, emit_sum, name, bm, single_buffer_w=False):
    m, k = a.shape
    n = w.shape[1]
    row = lambda i: (i, 0)
    w_spec = (pl.BlockSpec((k, n), lambda i: (0, 0), pipeline_mode=pl.Buffered(1)) if single_buffer_w
              else pl.BlockSpec((k, n), lambda i: (0, 0)))
    norm_shape = jax.ShapeDtypeStruct((m, n), norm_dtype)
    out_shape = (jax.ShapeDtypeStruct((m, n), F32), norm_shape) if emit_sum else norm_shape
    out_specs = (pl.BlockSpec((bm, n), row), pl.BlockSpec((bm, n), row)) if emit_sum else pl.BlockSpec((bm, n), row)
    return pl.pallas_call(
        functools.partial(_mm_res_norm_kernel, emit_sum),
        out_shape=out_shape,
        grid=(m // bm,),
        in_specs=[pl.BlockSpec((bm, k), row), w_spec, pl.BlockSpec((bm, n), row), pl.BlockSpec((1, n), lambda i: (0, 0))],
        out_specs=out_specs,
        compiler_params=_params(1),
        name=name,
    )(a, w, res, lnw.reshape(1, n))


def _convert_riders(rider_in, rider_out):
    for src, dst in zip(rider_in, rider_out):
        dst[...] = src[...].astype(dst.dtype)


def _rider_specs(riders, n_steps, step_of):
    specs, shapes = [], []
    for w in riders:
        rows, cols = w.shape
        assert rows % n_steps == 0 and (rows // n_steps) % BF16_SUBLANES == 0, (w.shape, n_steps)
        specs.append(pl.BlockSpec((rows // n_steps, cols), lambda i, j: (step_of(i, j), 0)))
        shapes.append(jax.ShapeDtypeStruct((rows, cols), BF16))
    return specs, shapes


def _swiglu_kernel(n_riders, a_ref, w1_ref, w3_ref, *rest):
    rider_in, (o_ref, *rider_out), (w1_s, w3_s) = rest[:n_riders], rest[n_riders:2 * n_riders + 1], rest[2 * n_riders + 1:]
    _convert_riders(rider_in, rider_out)
    w1_s[...] = w1_ref[...].astype(BF16)
    w3_s[...] = w3_ref[...].astype(BF16)
    for rows in _row_tiles(a_ref.shape[0]):
        a = a_ref[rows, :]
        u = jnp.dot(a, w1_s[...], preferred_element_type=F32)
        g = jnp.dot(a, w3_s[...], preferred_element_type=F32)
        o_ref[rows, :] = (_silu(u) * g).astype(o_ref.dtype)


def _swiglu(a, w1, w3, name, bm, bn, riders=()):
    m, k = a.shape
    n = w1.shape[1]
    nj = n // bn
    rider_specs, rider_shapes = _rider_specs(riders, (m // bm) * nj, lambda i, j: i * nj + j)
    outs = pl.pallas_call(
        functools.partial(_swiglu_kernel, len(riders)),
        scratch_shapes=[pltpu.VMEM((k, bn), BF16), pltpu.VMEM((k, bn), BF16)],
        out_shape=[jax.ShapeDtypeStruct((m, n), BF16)] + rider_shapes,
        grid=(m // bm, nj),
        in_specs=[
            pl.BlockSpec((bm, k), lambda i, j: (i, 0)),
            pl.BlockSpec((k, bn), lambda i, j: (0, j)),
            pl.BlockSpec((k, bn), lambda i, j: (0, j)),
        ] + rider_specs,
        out_specs=[pl.BlockSpec((bm, bn), lambda i, j: (i, j))] + rider_specs,
        compiler_params=_params(2),
        name=name,
    )(a, w1, w3, *riders)
    return outs[0], tuple(outs[1:])


def _merge_kernel(n_riders, h_ref, oa_ref, oh_ref, wga_ref, wgb_ref, wa_ref, wb_ref, *rest):
    rider_in, (o_ref, *rider_out) = rest[:n_riders], rest[n_riders:2 * n_riders + 1]
    wga_s, wgb_s, wa_s, wb_s = rest[2 * n_riders + 1:]
    _convert_riders(rider_in, rider_out)
    for src, dst in ((wga_ref, wga_s), (wgb_ref, wgb_s), (wa_ref, wa_s), (wb_ref, wb_s)):
        dst[...] = src[...].astype(BF16)
    for rows in _row_tiles(h_ref.shape[0]):
        h = h_ref[rows, :]
        ga = jnp.dot(h, wga_s[...], preferred_element_type=F32)
        gb = jnp.dot(h, wgb_s[...], preferred_element_type=F32)
        pa = jnp.dot(oa_ref[rows, :], wa_s[...], preferred_element_type=F32)
        pb = jnp.dot(oh_ref[rows, :], wb_s[...], preferred_element_type=F32)
        o_ref[rows, :] = (_sigmoid(ga) * pa + _sigmoid(gb) * pb).astype(o_ref.dtype)


def _merge(h, o_att, o_hg, w_in, gate_col0, wa, wb, name, bm, bn, riders=()):
    m, k = h.shape
    d = wa.shape[1]
    ga0 = gate_col0 // bn
    gb0 = (gate_col0 + d) // bn
    nj = d // bn
    rider_specs, rider_shapes = _rider_specs(riders, (m // bm) * nj, lambda i, j: i * nj + j)
    outs = pl.pallas_call(
        functools.partial(_merge_kernel, len(riders)),
        out_shape=[jax.ShapeDtypeStruct((m, d), BF16)] + rider_shapes,
        grid=(m // bm, nj),
        in_specs=[
            pl.BlockSpec((bm, k), lambda i, j: (i, 0)),
            pl.BlockSpec((bm, o_att.shape[1]), lambda i, j: (i, 0)),
            pl.BlockSpec((bm, o_hg.shape[1]), lambda i, j: (i, 0)),
            pl.BlockSpec((k, bn), lambda i, j: (0, j + ga0)),
            pl.BlockSpec((k, bn), lambda i, j: (0, j + gb0)),
            pl.BlockSpec((wa.shape[0], bn), lambda i, j: (0, j)),
            pl.BlockSpec((wb.shape[0], bn), lambda i, j: (0, j)),
        ] + rider_specs,
        out_specs=[pl.BlockSpec((bm, bn), lambda i, j: (i, j))] + rider_specs,
        scratch_shapes=[pltpu.VMEM((k, bn), BF16), pltpu.VMEM((k, bn), BF16),
                        pltpu.VMEM((wa.shape[0], bn), BF16), pltpu.VMEM((wb.shape[0], bn), BF16)],
        compiler_params=_params(2),
        name=name,
    )(h, o_att, o_hg, w_in, w_in, wa, wb, *riders)
    return outs[0], tuple(outs[1:])


def _mid_kernel(mg_ref, x_ref, kv_ref, wout_ref, wq_ref, wo_ref, lnc_ref, lnf_ref, x2_ref, hf_ref):
    d = x_ref.shape[1]
    row_tiles = _row_tiles(x_ref.shape[0])
    col_tiles = _col_tiles(d)

    def residual_norm(rows, lhs, w_ref, base_ref, ln_ref):
        ys = [base_ref[rows, cols] + jnp.dot(lhs, w_ref[:, cols], preferred_element_type=F32) for cols in col_tiles]
        ssq = sum(jnp.sum(y * y, axis=-1, keepdims=True) for y in ys)
        scale = lax.rsqrt(ssq * (1.0 / d) + RMS_EPS)
        for cols, y in zip(col_tiles, ys):
            x2_ref[rows, cols] = y
            hf_ref[rows, cols] = (y * scale * ln_ref[:, cols]).astype(hf_ref.dtype)

    for rows in row_tiles:
        residual_norm(rows, mg_ref[rows, :], wout_ref, x_ref, lnc_ref)
    qcs = [jnp.dot(hf_ref[rows, :], wq_ref[...], preferred_element_type=F32).astype(BF16) for rows in row_tiles]
    scores = [[lax.dot_general(qc[:, h * HEAD_DIM:(h + 1) * HEAD_DIM], kv_ref[:, h * HEAD_DIM:(h + 1) * HEAD_DIM],
                               (((1,), (1,)), ((), ())), preferred_element_type=F32) * (HEAD_DIM ** -0.5)
               for h in range(CROSS_HEADS)] for qc in qcs]
    ocs = []
    for per_head in scores:
        outs = []
        for h, s in enumerate(per_head):
            m = jnp.max(s, axis=-1, keepdims=True)
            p = jnp.exp(s - m)
            pn = (p * (1.0 / jnp.sum(p, axis=-1, keepdims=True))).astype(BF16)
            v = kv_ref[:, CROSS_WIDTH + h * HEAD_DIM:CROSS_WIDTH + (h + 1) * HEAD_DIM]
            outs.append(jnp.dot(pn, v, preferred_element_type=F32).astype(BF16))
        ocs.append(jnp.concatenate(outs, axis=-1))
    for rows, oc in zip(row_tiles, ocs):
        residual_norm(rows, oc, wo_ref, x2_ref, lnf_ref)


def _mid_block(merged, x, kvc, w_out, wq, wo, ln_cross, ln_ffn, name, bm):
    b, s, d = x.shape
    mlen = kvc.shape[1]
    tile = pl.BlockSpec((None, bm, d), lambda bi, i: (bi, i, 0))

    def resident(shape):
        return pl.BlockSpec(shape, lambda bi, i: (0,) * len(shape), pipeline_mode=pl.Buffered(1))

    return pl.pallas_call(
        _mid_kernel,
        out_shape=(jax.ShapeDtypeStruct((b, s, d), F32), jax.ShapeDtypeStruct((b, s, d), BF16)),
        grid=(b, s // bm),
        in_specs=[
            tile, tile,
            pl.BlockSpec((None, mlen, 2 * CROSS_WIDTH), lambda bi, i: (bi, 0, 0)),
            resident(w_out.shape), resident(wq.shape), resident(wo.shape),
            resident((1, d)), resident((1, d)),
        ],
        out_specs=(tile, tile),
        compiler_params=_params(2),
        name=name,
    )(merged, x, kvc, w_out, wq, wo, ln_cross.reshape(1, d), ln_ffn.reshape(1, d))


def _block_scores(q, kk):
    return lax.dot_general(q, kk, (((1,), (1,)), ((), ())), preferred_element_type=F32)


def _block_softmax_pv(s, vv, mask):
    scale = HEAD_DIM ** -0.5
    s = jnp.where(mask, s, -jnp.inf)
    m = jnp.max(s, axis=-1, keepdims=True)
    p = jnp.exp2((s - m) * (scale * LOG2_E))
    l = jnp.sum(p, axis=-1, keepdims=True)
    o = jnp.dot(p.astype(BF16), vv, preferred_element_type=F32) * (1.0 / l)
    return o, m * scale + jnp.log(l)


ATT_BLOCKS_IN_FLIGHT = 4


def _attn_kernel(q0, k0, v0, q1, k1, v1, q2, k2, v2, o_ref, st, og, ls):
    seq = q0.shape[0]
    blk = ATT_BLOCK
    for idx, ref in enumerate((q1, k1, v1, q2, k2, v2)):
        st[idx] = ref[...].astype(F32)

    qi = lax.broadcasted_iota(jnp.int32, (blk, 2 * blk), 0)
    kj = lax.broadcasted_iota(jnp.int32, (blk, 2 * blk), 1)
    dist = qi - kj + blk
    first_mask = (lax.broadcasted_iota(jnp.int32, (blk, blk), 0) >= lax.broadcasted_iota(jnp.int32, (blk, blk), 1))

    srcs = ((q0, k0, v0), (0, 1, 2), (3, 4, 5))

    def rows(dil, r, first_blk, count):
        start = first_blk * blk * dil + r
        if dil == 1:
            return pl.ds(start, count)
        return pl.ds(start, count, stride=dil)

    def load(g, which, rws):
        if g == 0:
            return srcs[0][which][rws, :]
        return st[srcs[g][which], rws, :].astype(BF16)

    tasks = []
    for g, (window, dil) in enumerate(ATT_GROUPS):
        band_mask = (dist >= 0) & (dist <= window // dil)
        for r in range(dil):
            for n in range(seq // dil // blk):
                if n == 0:
                    tasks.append((g, rows(dil, r, 0, blk), rows(dil, r, 0, blk), first_mask))
                else:
                    tasks.append((g, rows(dil, r, n, blk), rows(dil, r, n - 1, 2 * blk), band_mask))

    def finish(batch, scores):
        for (g, q_rows, k_rows, mask), s in zip(batch, scores):
            o, lse = _block_softmax_pv(s, load(g, 2, k_rows), mask)
            og[g, q_rows, :] = o
            ls[g, q_rows, :] = jnp.broadcast_to(lse, (blk, HEAD_DIM))

    pending = None
    for t0 in range(0, len(tasks), ATT_BLOCKS_IN_FLIGHT):
        batch = tasks[t0:t0 + ATT_BLOCKS_IN_FLIGHT]
        scores = [_block_scores(load(g, 0, q_rows), load(g, 1, k_rows)) for g, q_rows, k_rows, _ in batch]
        if pending is not None:
            finish(*pending)
        pending = (batch, scores)
    finish(*pending)

    for n in range(seq // blk):
        sl = pl.ds(n * blk, blk)
        l0, l1, l2 = ls[0, sl, :], ls[1, sl, :], ls[2, sl, :]
        mx = jnp.maximum(jnp.maximum(l0, l1), l2)
        e0, e1, e2 = jnp.exp(l0 - mx), jnp.exp(l1 - mx), jnp.exp(l2 - mx)
        inv = 1.0 / (e0 + e1 + e2)
        out = (e0 * inv) * og[0, sl, :] + (e1 * inv) * og[1, sl, :] + (e2 * inv) * og[2, sl, :]
        o_ref[sl, :] = out.astype(o_ref.dtype)


def _dilated_attention(qkv, b, name):
    s = qkv.shape[1] // b
    hpg = ATT_HEADS_PER_GROUP
    n_cols = ATT_WIDTH // HEAD_DIM

    def spec(which, g):
        return pl.BlockSpec((None, s, HEAD_DIM), lambda bi, h: (which * n_cols + g * hpg + h, bi, 0))

    in_specs = [spec(which, g) for g in range(N_ATT_GROUPS) for which in range(3)]
    return pl.pallas_call(
        _attn_kernel,
        out_shape=jax.ShapeDtypeStruct((b, s, ATT_OUT_WIDTH), BF16),
        grid=(b, hpg),
        in_specs=in_specs,
        out_specs=pl.BlockSpec((None, s, HEAD_DIM), lambda bi, h: (bi, 0, h)),
        scratch_shapes=[
            pltpu.VMEM((6, s, HEAD_DIM), F32),
            pltpu.VMEM((3, s, HEAD_DIM), F32),
            pltpu.VMEM((3, s, HEAD_DIM), F32),
        ],
        compiler_params=_params(2),
        name=name,
    )(*([qkv] * 9))


def _hg_prep_kernel(layer, a_ref, wa_ref, wb_ref, lbp_ref, nw_ref, qa_ref, ka_ref, v_ref, gs_ref, rv_ref, wa_s, wb_s):
    j = pl.program_id(1)
    c = HG_CHUNK
    bm = a_ref.shape[0]
    heads_per_tile = wa_ref.shape[1] // HG_KEY
    n_tiles_per_kind = HG_WIDTH // wa_ref.shape[1]

    def cast_weights():
        wa_s[...] = wa_ref[...].astype(BF16)
        wb_s[...] = wb_ref[...].astype(BF16)

    def put(dst, rows, val):
        for hd in range(heads_per_tile):
            dst[hd, rows, :] = val[:, hd * HG_KEY:(hd + 1) * HG_KEY].astype(dst.dtype)

    def two_dots(r0):
        x = a_ref[pl.ds(r0, MXU_ROW_TILE), :]
        return (jnp.dot(x, wa_s[...], preferred_element_type=F32), jnp.dot(x, wb_s[...], preferred_element_type=F32))

    @pl.when(j < n_tiles_per_kind)
    def _():
        p = lbp_ref[...]
        e = jnp.exp(p - jnp.max(p, axis=0, keepdims=True))
        sm = e / jnp.sum(e, axis=0, keepdims=True)
        lb = jnp.sum(sm[: layer + 1], axis=0, keepdims=True)
        f_mid = 0.5 * (1.0 + lb)
        f_half = 0.5 * (1.0 - lb)
        col_in_part = jnp.bitwise_and(lax.broadcasted_iota(jnp.int32, (c, 3 * c), 1), c - 1)
        tri3 = jnp.where(lax.broadcasted_iota(jnp.int32, (c, 3 * c), 0) >= col_in_part, 1.0, 0.0).astype(BF16)
        cast_weights()

        def finish(r0, q, f_pre):
            f = f_mid + f_half * jnp.tanh(0.5 * f_pre)
            lf = jnp.log(f)
            hi = lf.astype(BF16)
            r1 = lf - hi.astype(F32)
            mid = r1.astype(BF16)
            lo = (r1 - mid.astype(F32)).astype(BF16)
            for k0 in range(0, MXU_ROW_TILE, c):
                sl = slice(k0, k0 + c)
                bcum = jnp.dot(tri3, jnp.concatenate([hi[sl], mid[sl], lo[sl]], axis=0), preferred_element_type=F32)
                b_mid = bcum[c // 2:c // 2 + 1, :]
                b_last = bcum[c - 1:c, :]
                rows = pl.ds(r0 + k0, c)
                put(qa_ref, rows, _silu(q[sl]) * jnp.exp(bcum - b_mid))
                put(ka_ref, rows, (1.0 - f[sl]) * jnp.exp(b_mid - bcum))
                n = (r0 + k0) // c
                rv_ref[0, n:n + 1, :] = jnp.exp(b_mid)
                rv_ref[1, n:n + 1, :] = jnp.exp(b_last - b_mid)
                rv_ref[2, n:n + 1, :] = jnp.exp(b_last)

        pending = None
        for r0 in range(0, bm, MXU_ROW_TILE):
            dots = two_dots(r0)
            if pending is not None:
                finish(*pending)
            pending = (r0,) + dots
        finish(*pending)

    @pl.when(j >= n_tiles_per_kind)
    def _():
        cast_weights()
        for r0 in range(0, bm, MXU_ROW_TILE):
            i_proj, g_proj = two_dots(r0)
            rows = pl.ds(r0, MXU_ROW_TILE)
            put(v_ref, rows, i_proj)
            put(gs_ref, rows, _silu(g_proj) * nw_ref[...])


def _hg_prep(h, w_in, col0, lower_bounds_param, norm_w, layer, name, bm, bn):
    t, k = h.shape
    base = col0 // bn
    nk = HG_WIDTH // bn
    hpt = bn // HG_KEY
    n_layers = lower_bounds_param.shape[0]
    head_shape = jax.ShapeDtypeStruct((HG_HEADS, t, HG_KEY), BF16)
    first = lambda j: jnp.minimum(j, nk - 1)
    second = lambda j: jnp.maximum(j - nk, 0)
    head_block = lambda sel: pl.BlockSpec((hpt, bm, HG_KEY), lambda i, j: (sel(j), i, 0))
    return pl.pallas_call(
        functools.partial(_hg_prep_kernel, layer),
        out_shape=[head_shape, head_shape, head_shape, head_shape,
                   jax.ShapeDtypeStruct((3, t // HG_CHUNK, HG_WIDTH), F32)],
        grid=(t // bm, 2 * nk),
        in_specs=[
            pl.BlockSpec((bm, k), lambda i, j: (i, 0)),
            pl.BlockSpec((k, bn), lambda i, j: (0, base + j + nk * (j // nk))),
            pl.BlockSpec((k, bn), lambda i, j: (0, base + nk + j + nk * (j // nk))),
            pl.BlockSpec((n_layers, bn), lambda i, j: (0, first(j))),
            pl.BlockSpec((1, bn), lambda i, j: (0, 0)),
        ],
        out_specs=[head_block(first), head_block(first), head_block(second), head_block(second),
                   pl.BlockSpec((3, bm // HG_CHUNK, bn), lambda i, j: (0, i, first(j)))],
        scratch_shapes=[pltpu.VMEM((k, bn), BF16), pltpu.VMEM((k, bn), BF16)],
        compiler_params=_params(2),
        name=name,
    )(h, w_in, w_in, lower_bounds_param, jnp.tile(norm_w, hpt).reshape(1, bn))


def _hgrn_scan_kernel(qa_ref, ka_ref, v_ref, gs_ref, rv_ref, o_ref, state):
    ts = qa_ref.shape[1]
    c = HG_CHUNK

    @pl.when(pl.program_id(1) == 0)
    def _():
        state[...] = jnp.zeros_like(state)

    causal = lax.broadcasted_iota(jnp.int32, (c, c), 0) >= lax.broadcasted_iota(jnp.int32, (c, c), 1)
    heads = [pl.ds(h * HG_KEY, HG_KEY) for h in range(HG_HEADS)]
    for n in range(ts // c):
        rows = pl.ds(n * c, c)
        both = []
        for h, cols in enumerate(heads):
            rhs = jnp.concatenate([(state[h] * rv_ref[0, n:n + 1, cols]).astype(BF16), ka_ref[h, rows, :]], axis=0)
            both.append(lax.dot_general(qa_ref[h, rows, :], rhs, (((1,), (1,)), ((), ())), preferred_element_type=F32))
        kv_t = [lax.dot_general(v_ref[h, rows, :], ka_ref[h, rows, :], (((0,), (0,)), ((), ())), preferred_element_type=F32)
                for h in range(HG_HEADS)]
        for h, cols in enumerate(heads):
            att = jnp.where(causal, both[h][:, HG_VAL:], 0.0).astype(BF16)
            o = both[h][:, :HG_VAL] + jnp.dot(att, v_ref[h, rows, :], preferred_element_type=F32)
            o = o * lax.rsqrt(jnp.mean(o * o, axis=-1, keepdims=True) + RMS_EPS) * gs_ref[h, rows, :].astype(F32)
            o_ref[rows, cols] = o.astype(o_ref.dtype)
        for h, cols in enumerate(heads):
            state[h] = rv_ref[2, n:n + 1, cols] * state[h] + kv_t[h] * rv_ref[1, n:n + 1, cols]


def _hgrn_scan(qa, ka, v, gs, rv, b, name, ts):
    s = qa.shape[1] // b
    nt = s // ts
    head_block = pl.BlockSpec((HG_HEADS, ts, HG_KEY), lambda bi, t: (0, bi * nt + t, 0))
    return pl.pallas_call(
        _hgrn_scan_kernel,
        out_shape=jax.ShapeDtypeStruct((b, s, HG_WIDTH), BF16),
        grid=(b, nt),
        in_specs=[head_block, head_block, head_block, head_block,
                  pl.BlockSpec((3, ts // HG_CHUNK, HG_WIDTH), lambda bi, t: (0, bi * nt + t, 0))],
        out_specs=pl.BlockSpec((None, ts, HG_WIDTH), lambda bi, t: (bi, t, 0)),
        scratch_shapes=[pltpu.VMEM((HG_HEADS, HG_VAL, HG_KEY), F32)],
        compiler_params=_params(2),
        name=name,
    )(qa, ka, v, gs, rv)


def kernel(x, mem, ln_mix_w, w_in, hg_norm_w, hg_lower_bounds, w_branch_a, w_branch_b, w_out, ln_cross_w, ln_mem_w, wq_cross, wkv_cross, wo_cross, ln_ffn_w, w1, w3, w2, ln_final_w):
    b, s, d = x.shape
    mlen = mem.shape[1]
    t = b * s
    depth = w_in.shape[0]
    qkv_w = 3 * ATT_WIDTH
    hg_w = 4 * HG_WIDTH

    xf = x.reshape(t, d)
    memf = mem.reshape(b * mlen, d)
    for l in range(depth):
        h = _rmsnorm(xf, ln_mix_w[l], BF16, "rms_mix")
        qkv = _matmul(h, w_in[l], 0, qkv_w, BF16, "proj_qkv", bm=2048, bn=768, head_major=True)
        qa, ka, v, gate, rv = _hg_prep(h, w_in[l], qkv_w, hg_lower_bounds, hg_norm_w[l], l, "proj_hg_prep", bm=1024, bn=512)
        o_att = _dilated_attention(qkv, b, "dilated_attn")
        o_hg = _hgrn_scan(qa, ka, v, gate, rv, b, "hgrn2_scan", ts=1024)
        merged, (w_out_b, wq_b, wo_b) = _merge(
            h, o_att.reshape(t, ATT_OUT_WIDTH), o_hg.reshape(t, HG_WIDTH), w_in[l], qkv_w + hg_w,
            w_branch_a[l], w_branch_b[l], "gate_merge", bm=1024, bn=512, riders=(w_out[l], wq_cross[l], wo_cross[l]))
        kvc = _norm_matmul(memf, ln_mem_w[l], wkv_cross[l], BF16, "cross_kv", bm=512, bn=512)
        x2, hf = _mid_block(merged.reshape(b, s, d), xf.reshape(b, s, d), kvc.reshape(b, mlen, 2 * CROSS_WIDTH),
                            w_out_b, wq_b, wo_b, ln_cross_w[l], ln_ffn_w[l], "out_proj_cross", bm=512)
        xf, hf = x2.reshape(t, d), hf.reshape(t, d)
        act, (w2_b,) = _swiglu(hf, w1[l], w3[l], "ffn_up", bm=2048, bn=512, riders=(w2[l],))
        if l + 1 < depth:
            xf = _matmul_residual(act, w2_b, xf, "ffn_down", bm=512, bn=512)
        else:
            out = _matmul_residual_norm(act, w2_b, xf, ln_final_w, F32, False, "ffn_down_final", bm=512,
                                        single_buffer_w=True)
    return out.reshape(b, s, d)
```

```python
import functools

import jax
import jax.numpy as jnp
from jax import lax
from jax.experimental import pallas as pl
from jax.experimental.pallas import tpu as pltpu

F32 = jnp.float32
BF16 = jnp.bfloat16

HEAD_DIM = 128
ATT_GROUPS = ((128, 1), (512, 4), (2048, 16))
ATT_HEADS_PER_GROUP = 4
N_ATT_GROUPS = len(ATT_GROUPS)
ATT_WIDTH = N_ATT_GROUPS * ATT_HEADS_PER_GROUP * HEAD_DIM
ATT_OUT_WIDTH = ATT_HEADS_PER_GROUP * HEAD_DIM
ATT_BLOCK = 128
HG_HEADS = 8
HG_KEY = 128
HG_VAL = 128
HG_WIDTH = HG_HEADS * HG_KEY
HG_CHUNK = 64
CROSS_HEADS = 4
CROSS_WIDTH = CROSS_HEADS * HEAD_DIM
RMS_EPS = 1e-6
LOG2_E = 1.4426950408889634

V7X_VMEM_LIMIT_BYTES = 56 * 1024 * 1024
MXU_ROW_TILE = 256
MXU_COL_TILE = 512
BF16_SUBLANES = 16


def _sigmoid(x):
    return 0.5 * jnp.tanh(0.5 * x) + 0.5


def _silu(x):
    hx = 0.5 * x
    return hx * jnp.tanh(hx) + hx


def _params(n_axes):
    return pltpu.CompilerParams(
        dimension_semantics=("arbitrary",) * n_axes,
        vmem_limit_bytes=V7X_VMEM_LIMIT_BYTES,
    )


def _rms_kernel(x_ref, w_ref, o_ref):
    x = x_ref[...]
    ms = jnp.mean(x * x, axis=-1, keepdims=True)
    o_ref[...] = (x * lax.rsqrt(ms + RMS_EPS) * w_ref[...]).astype(o_ref.dtype)


def _rmsnorm(x, w, out_dtype, name, bm=512):
    m, d = x.shape
    return pl.pallas_call(
        _rms_kernel,
        out_shape=jax.ShapeDtypeStruct((m, d), out_dtype),
        grid=(m // bm,),
        in_specs=[pl.BlockSpec((bm, d), lambda i: (i, 0)), pl.BlockSpec((1, d), lambda i: (0, 0))],
        out_specs=pl.BlockSpec((bm, d), lambda i: (i, 0)),
        compiler_params=_params(1),
        name=name,
    )(x, w.reshape(1, d))


def _row_tiles(n_rows):
    tile = min(MXU_ROW_TILE, n_rows)
    return [pl.ds(r * tile, tile) for r in range(n_rows // tile)]


def _col_tiles(n_cols):
    tile = min(MXU_COL_TILE, n_cols)
    return [pl.ds(c * tile, tile) for c in range(n_cols // tile)]


def _mm_kernel(a_ref, w_ref, o_ref, w_s):
    w_s[...] = w_ref[...].astype(BF16)
    for rows in _row_tiles(a_ref.shape[0]):
        res = jnp.dot(a_ref[rows, :], w_s[...], preferred_element_type=F32).astype(o_ref.dtype)
        if len(o_ref.shape) == 2:
            o_ref[rows, :] = res
        else:
            for hd in range(o_ref.shape[0]):
                o_ref[hd, rows, :] = res[:, hd * HEAD_DIM:(hd + 1) * HEAD_DIM]


def _matmul(a, w, col0, n, out_dtype, name, bm, bn, head_major=False):
    m, k = a.shape
    cb = col0 // bn
    if head_major:
        out_shape = jax.ShapeDtypeStruct((n // HEAD_DIM, m, HEAD_DIM), out_dtype)
        out_spec = pl.BlockSpec((bn // HEAD_DIM, bm, HEAD_DIM), lambda i, j: (j, i, 0))
    else:
        out_shape = jax.ShapeDtypeStruct((m, n), out_dtype)
        out_spec = pl.BlockSpec((bm, bn), lambda i, j: (i, j))
    return pl.pallas_call(
        _mm_kernel,
        out_shape=out_shape,
        grid=(m // bm, n // bn),
        in_specs=[
            pl.BlockSpec((bm, k), lambda i, j: (i, 0)),
            pl.BlockSpec((k, bn), lambda i, j: (0, j + cb)),
        ],
        out_specs=out_spec,
        scratch_shapes=[pltpu.VMEM((k, bn), BF16)],
        compiler_params=_params(2),
        name=name,
    )(a, w)


def _mm_res_kernel(a_ref, w_ref, r_ref, o_ref):
    o_ref[...] = r_ref[...] + jnp.dot(a_ref[...], w_ref[...], preferred_element_type=F32)


def _matmul_residual(a, w, res, name, bm, bn):
    m, k = a.shape
    n = w.shape[1]
    return pl.pallas_call(
        _mm_res_kernel,
        out_shape=jax.ShapeDtypeStruct((m, n), F32),
        grid=(m // bm, n // bn),
        in_specs=[
            pl.BlockSpec((bm, k), lambda i, j: (i, 0)),
            pl.BlockSpec((k, bn), lambda i, j: (0, j)),
            pl.BlockSpec((bm, bn), lambda i, j: (i, j)),
        ],
        out_specs=pl.BlockSpec((bm, bn), lambda i, j: (i, j)),
        compiler_params=_params(2),
        name=name,
    )(a, w, res)


def _mm_res_norm_kernel(emit_sum, a_ref, w_ref, r_ref, lnw_ref, *out_refs):
    n = w_ref.shape[1]
    for rows in _row_tiles(a_ref.shape[0]):
        a = a_ref[rows, :]
        ys = [r_ref[rows, cols] + jnp.dot(a, w_ref[:, cols], preferred_element_type=F32) for cols in _col_tiles(n)]
        ssq = sum(jnp.sum(y * y, axis=-1, keepdims=True) for y in ys)
        scale = lax.rsqrt(ssq * (1.0 / n) + RMS_EPS)
        for cols, y in zip(_col_tiles(n), ys):
            if emit_sum:
                out_refs[0][rows, cols] = y
            out_refs[-1][rows, cols] = (y * scale * lnw_ref[:, cols]).astype(out_refs[-1].dtype)


def _matmul_residual_norm(a, w, res, lnw, norm_dtype, emit_sum, name, bm, single_buffer_w=False):
    m, k = a.shape
    n = w.shape[1]
    row = lambda i: (i, 0)
    w_spec = (pl.BlockSpec((k, n), lambda i: (0, 0), pipeline_mode=pl.Buffered(1)) if single_buffer_w
              else pl.BlockSpec((k, n), lambda i: (0, 0)))
    norm_shape = jax.ShapeDtypeStruct((m, n), norm_dtype)
    out_shape = (jax.ShapeDtypeStruct((m, n), F32), norm_shape) if emit_sum else norm_shape
    out_specs = (pl.BlockSpec((bm, n), row), pl.BlockSpec((bm, n), row)) if emit_sum else pl.BlockSpec((bm, n), row)
    return pl.pallas_call(
        functools.partial(_mm_res_norm_kernel, emit_sum),
        out_shape=out_shape,
        grid=(m // bm,),
        in_specs=[pl.BlockSpec((bm, k), row), w_spec, pl.BlockSpec((bm, n), row), pl.BlockSpec((1, n), lambda i: (0, 0))],
        out_specs=out_specs,
        compiler_params=_params(1),
        name=name,
    )(a, w, res, lnw.reshape(1, n))


def _convert_riders(rider_in, rider_out):
    for src, dst in zip(rider_in, rider_out):
        dst[...] = src[...].astype(dst.dtype)


def _rider_specs(riders, n_steps, step_of):
    specs, shapes = [], []
    for w in riders:
        rows, cols = w.shape
        assert rows % n_steps == 0 and (rows // n_steps) % BF16_SUBLANES == 0, (w.shape, n_steps)
        specs.append(pl.BlockSpec((rows // n_steps, cols), lambda i, j: (step_of(i, j), 0)))
        shapes.append(jax.ShapeDtypeStruct((rows, cols), BF16))
    return specs, shapes


def _swiglu_kernel(n_riders, a_ref, w1_ref, w3_ref, *rest):
    rider_in, (o_ref, *rider_out), (w1_s, w3_s) = rest[:n_riders], rest[n_riders:2 * n_riders + 1], rest[2 * n_riders + 1:]
    _convert_riders(rider_in, rider_out)
    w1_s[...] = w1_ref[...].astype(BF16)
    w3_s[...] = w3_ref[...].astype(BF16)
    for rows in _row_tiles(a_ref.shape[0]):
        a = a_ref[rows, :]
        u = jnp.dot(a, w1_s[...], preferred_element_type=F32)
        g = jnp.dot(a, w3_s[...], preferred_element_type=F32)
        o_ref[rows, :] = (_silu(u) * g).astype(o_ref.dtype)


def _swiglu(a, w1, w3, name, bm, bn, riders=()):
    m, k = a.shape
    n = w1.shape[1]
    nj = n // bn
    rider_specs, rider_shapes = _rider_specs(riders, (m // bm) * nj, lambda i, j: i * nj + j)
    outs = pl.pallas_call(
        functools.partial(_swiglu_kernel, len(riders)),
        scratch_shapes=[pltpu.VMEM((k, bn), BF16), pltpu.VMEM((k, bn), BF16)],
        out_shape=[jax.ShapeDtypeStruct((m, n), BF16)] + rider_shapes,
        grid=(m // bm, nj),
        in_specs=[
            pl.BlockSpec((bm, k), lambda i, j: (i, 0)),
            pl.BlockSpec((k, bn), lambda i, j: (0, j)),
            pl.BlockSpec((k, bn), lambda i, j: (0, j)),
        ] + rider_specs,
        out_specs=[pl.BlockSpec((bm, bn), lambda i, j: (i, j))] + rider_specs,
        compiler_params=_params(2),
        name=name,
    )(a, w1, w3, *riders)
    return outs[0], tuple(outs[1:])


def _merge_kernel(n_riders, h_ref, oa_ref, oh_ref, wga_ref, wgb_ref, wa_ref, wb_ref, *rest):
    rider_in, (o_ref, *rider_out) = rest[:n_riders], rest[n_riders:2 * n_riders + 1]
    wga_s, wgb_s, wa_s, wb_s = rest[2 * n_riders + 1:]
    _convert_riders(rider_in, rider_out)
    for src, dst in ((wga_ref, wga_s), (wgb_ref, wgb_s), (wa_ref, wa_s), (wb_ref, wb_s)):
        dst[...] = src[...].astype(BF16)
    for rows in _row_tiles(h_ref.shape[0]):
        h = h_ref[rows, :]
        ga = jnp.dot(h, wga_s[...], preferred_element_type=F32)
        gb = jnp.dot(h, wgb_s[...], preferred_element_type=F32)
        pa = jnp.dot(oa_ref[rows, :], wa_s[...], preferred_element_type=F32)
        pb = jnp.dot(oh_ref[rows, :], wb_s[...], preferred_element_type=F32)
        o_ref[rows, :] = (_sigmoid(ga) * pa + _sigmoid(gb) * pb).astype(o_ref.dtype)


def _merge(h, o_att, o_hg, w_in, gate_col0, wa, wb, name, bm, bn, riders=()):
    m, k = h.shape
    d = wa.shape[1]
    ga0 = gate_col0 // bn
    gb0 = (gate_col0 + d) // bn
    nj = d // bn
    rider_specs, rider_shapes = _rider_specs(riders, (m // bm) * nj, lambda i, j: i * nj + j)
    outs = pl.pallas_call(
        functools.partial(_merge_kernel, len(riders)),
        out_shape=[jax.ShapeDtypeStruct((m, d), BF16)] + rider_shapes,
        grid=(m // bm, nj),
        in_specs=[
            pl.BlockSpec((bm, k), lambda i, j: (i, 0)),
            pl.BlockSpec((bm, o_att.shape[1]), lambda i, j: (i, 0)),
            pl.BlockSpec((bm, o_hg.shape[1]), lambda i, j: (i, 0)),
            pl.BlockSpec((k, bn), lambda i, j: (0, j + ga0)),
            pl.BlockSpec((k, bn), lambda i, j: (0, j + gb0)),
            pl.BlockSpec((wa.shape[0], bn), lambda i, j: (0, j)),
            pl.BlockSpec((wb.shape[0], bn), lambda i, j: (0, j)),
        ] + rider_specs,
        out_specs=[pl.BlockSpec((bm, bn), lambda i, j: (i, j))] + rider_specs,
        scratch_shapes=[pltpu.VMEM((k, bn), BF16), pltpu.VMEM((k, bn), BF16),
                        pltpu.VMEM((wa.shape[0], bn), BF16), pltpu.VMEM((wb.shape[0], bn), BF16)],
        compiler_params=_params(2),
        name=name,
    )(h, o_att, o_hg, w_in, w_in, wa, wb, *riders)
    return outs[0], tuple(outs[1:])


def _mid_kernel(mg_ref, x_ref, mem_ref, wout_ref, wq_ref, wkv_ref, wo_ref, lnc_ref, lnm_ref, lnf_ref, x2_ref, hf_ref, kv_ref):
    d = x_ref.shape[1]

    @pl.when(pl.program_id(1) == 0)
    def _():
        mem = mem_ref[...]
        mn = (mem * lax.rsqrt(jnp.mean(mem * mem, axis=-1, keepdims=True) + RMS_EPS) * lnm_ref[...]).astype(BF16)
        kv_ref[...] = jnp.dot(mn, wkv_ref[...], preferred_element_type=F32).astype(kv_ref.dtype)

    row_tiles = _row_tiles(x_ref.shape[0])
    col_tiles = _col_tiles(d)

    def residual_norm(rows, lhs, w_ref, base_ref, ln_ref):
        ys = [base_ref[rows, cols] + jnp.dot(lhs, w_ref[:, cols], preferred_element_type=F32) for cols in col_tiles]
        ssq = sum(jnp.sum(y * y, axis=-1, keepdims=True) for y in ys)
        scale = lax.rsqrt(ssq * (1.0 / d) + RMS_EPS)
        for cols, y in zip(col_tiles, ys):
            x2_ref[rows, cols] = y
            hf_ref[rows, cols] = (y * scale * ln_ref[:, cols]).astype(hf_ref.dtype)

    for rows in row_tiles:
        residual_norm(rows, mg_ref[rows, :], wout_ref, x_ref, lnc_ref)
    qcs = [jnp.dot(hf_ref[rows, :], wq_ref[...], preferred_element_type=F32).astype(BF16) for rows in row_tiles]
    scores = [[lax.dot_general(qc[:, h * HEAD_DIM:(h + 1) * HEAD_DIM], kv_ref[:, h * HEAD_DIM:(h + 1) * HEAD_DIM],
                               (((1,), (1,)), ((), ())), preferred_element_type=F32) * (HEAD_DIM ** -0.5)
               for h in range(CROSS_HEADS)] for qc in qcs]
    ocs = []
    for per_head in scores:
        outs = []
        for h, s in enumerate(per_head):
            m = jnp.max(s, axis=-1, keepdims=True)
            p = jnp.exp(s - m)
            pn = (p * (1.0 / jnp.sum(p, axis=-1, keepdims=True))).astype(BF16)
            v = kv_ref[:, CROSS_WIDTH + h * HEAD_DIM:CROSS_WIDTH + (h + 1) * HEAD_DIM]
            outs.append(jnp.dot(pn, v, preferred_element_type=F32).astype(BF16))
        ocs.append(jnp.concatenate(outs, axis=-1))
    for rows, oc in zip(row_tiles, ocs):
        residual_norm(rows, oc, wo_ref, x2_ref, lnf_ref)


def _mid_block(merged, x, mem, w_out, wq, wkv, wo, ln_cross, ln_mem, ln_ffn, name, bm):
    b, s, d = x.shape
    mlen = mem.shape[1]
    tile = pl.BlockSpec((None, bm, d), lambda bi, i: (bi, i, 0))

    def resident(shape):
        return pl.BlockSpec(shape, lambda bi, i: (0,) * len(shape), pipeline_mode=pl.Buffered(1))

    return pl.pallas_call(
        _mid_kernel,
        out_shape=(jax.ShapeDtypeStruct((b, s, d), F32), jax.ShapeDtypeStruct((b, s, d), BF16)),
        grid=(b, s // bm),
        in_specs=[
            tile, tile,
            pl.BlockSpec((None, mlen, d), lambda bi, i: (bi, 0, 0)),
            resident(w_out.shape), resident(wq.shape), resident(wkv.shape), resident(wo.shape),
            resident((1, d)), resident((1, d)), resident((1, d)),
        ],
        out_specs=(tile, tile),
        scratch_shapes=[pltpu.VMEM((mlen, wkv.shape[1]), BF16)],
        compiler_params=_params(2),
        name=name,
    )(merged, x, mem, w_out, wq, wkv, wo, ln_cross.reshape(1, d), ln_mem.reshape(1, d), ln_ffn.reshape(1, d))


def _block_scores(q, kk):
    return lax.dot_general(q, kk, (((1,), (1,)), ((), ())), preferred_element_type=F32)


def _block_softmax_pv(s, vv, mask):
    scale = HEAD_DIM ** -0.5
    s = jnp.where(mask, s, -jnp.inf)
    m = jnp.max(s, axis=-1, keepdims=True)
    p = jnp.exp2((s - m) * (scale * LOG2_E))
    l = jnp.sum(p, axis=-1, keepdims=True)
    o = jnp.dot(p.astype(BF16), vv, preferred_element_type=F32) * (1.0 / l)
    return o, m * scale + jnp.log(l)


ATT_BLOCKS_IN_FLIGHT = 4


def _attn_kernel(q0, k0, v0, q1, k1, v1, q2, k2, v2, o_ref, st, og, ls):
    seq = q0.shape[0]
    blk = ATT_BLOCK
    for idx, ref in enumerate((q1, k1, v1, q2, k2, v2)):
        st[idx] = ref[...].astype(F32)

    qi = lax.broadcasted_iota(jnp.int32, (blk, 2 * blk), 0)
    kj = lax.broadcasted_iota(jnp.int32, (blk, 2 * blk), 1)
    dist = qi - kj + blk
    first_mask = (lax.broadcasted_iota(jnp.int32, (blk, blk), 0) >= lax.broadcasted_iota(jnp.int32, (blk, blk), 1))

    srcs = ((q0, k0, v0), (0, 1, 2), (3, 4, 5))

    def rows(dil, r, first_blk, count):
        start = first_blk * blk * dil + r
        if dil == 1:
            return pl.ds(start, count)
        return pl.ds(start, count, stride=dil)

    def load(g, which, rws):
        if g == 0:
            return srcs[0][which][rws, :]
        return st[srcs[g][which], rws, :].astype(BF16)

    tasks = []
    for g, (window, dil) in enumerate(ATT_GROUPS):
        band_mask = (dist >= 0) & (dist <= window // dil)
        for r in range(dil):
            for n in range(seq // dil // blk):
                if n == 0:
                    tasks.append((g, rows(dil, r, 0, blk), rows(dil, r, 0, blk), first_mask))
                else:
                    tasks.append((g, rows(dil, r, n, blk), rows(dil, r, n - 1, 2 * blk), band_mask))

    def finish(batch, scores):
        for (g, q_rows, k_rows, mask), s in zip(batch, scores):
            o, lse = _block_softmax_pv(s, load(g, 2, k_rows), mask)
            og[g, q_rows, :] = o
            ls[g, q_rows, :] = jnp.broadcast_to(lse, (blk, HEAD_DIM))

    pending = None
    for t0 in range(0, len(tasks), ATT_BLOCKS_IN_FLIGHT):
        batch = tasks[t0:t0 + ATT_BLOCKS_IN_FLIGHT]
        scores = [_block_scores(load(g, 0, q_rows), load(g, 1, k_rows)) for g, q_rows, k_rows, _ in batch]
        if pending is not None:
            finish(*pending)
        pending = (batch, scores)
    finish(*pending)

    for n in range(seq // blk):
        sl = pl.ds(n * blk, blk)
        l0, l1, l2 = ls[0, sl, :], ls[1, sl, :], ls[2, sl, :]
        mx = jnp.maximum(jnp.maximum(l0, l1), l2)
        e0, e1, e2 = jnp.exp(l0 - mx), jnp.exp(l1 - mx), jnp.exp(l2 - mx)
        inv = 1.0 / (e0 + e1 + e2)
        out = (e0 * inv) * og[0, sl, :] + (e1 * inv) * og[1, sl, :] + (e2 * inv) * og[2, sl, :]
        o_ref[sl, :] = out.astype(o_ref.dtype)


def _dilated_attention(qkv, b, name):
    s = qkv.shape[1] // b
    hpg = ATT_HEADS_PER_GROUP
    n_cols = ATT_WIDTH // HEAD_DIM

    def spec(which, g):
        return pl.BlockSpec((None, s, HEAD_DIM), lambda bi, h: (which * n_cols + g * hpg + h, bi, 0))

    in_specs = [spec(which, g) for g in range(N_ATT_GROUPS) for which in range(3)]
    return pl.pallas_call(
        _attn_kernel,
        out_shape=jax.ShapeDtypeStruct((b, s, ATT_OUT_WIDTH), BF16),
        grid=(b, hpg),
        in_specs=in_specs,
        out_specs=pl.BlockSpec((None, s, HEAD_DIM), lambda bi, h: (bi, 0, h)),
        scratch_shapes=[
            pltpu.VMEM((6, s, HEAD_DIM), F32),
            pltpu.VMEM((3, s, HEAD_DIM), F32),
            pltpu.VMEM((3, s, HEAD_DIM), F32),
        ],
        compiler_params=_params(2),
        name=name,
    )(*([qkv] * 9))


def _hg_prep_kernel(layer, a_ref, wa_ref, wb_ref, lbp_ref, nw_ref, qa_ref, ka_ref, v_ref, gs_ref, rv_ref, wa_s, wb_s):
    j = pl.program_id(1)
    c = HG_CHUNK
    bm = a_ref.shape[0]
    rt = MXU_ROW_TILE
    heads_per_tile = wa_ref.shape[1] // HG_KEY
    n_tiles_per_kind = HG_WIDTH // wa_ref.shape[1]

    def cast_weights():
        wa_s[...] = wa_ref[...].astype(BF16)
        wb_s[...] = wb_ref[...].astype(BF16)

    def put(dst, rows, val):
        for hd in range(heads_per_tile):
            dst[hd, rows, :] = val[:, hd * HG_KEY:(hd + 1) * HG_KEY].astype(dst.dtype)

    def two_dots(r0):
        x = a_ref[pl.ds(r0, rt), :]
        return (jnp.dot(x, wa_s[...], preferred_element_type=F32), jnp.dot(x, wb_s[...], preferred_element_type=F32))

    @pl.when(j < n_tiles_per_kind)
    def _():
        p = lbp_ref[...]
        e = jnp.exp(p - jnp.max(p, axis=0, keepdims=True))
        sm = e / jnp.sum(e, axis=0, keepdims=True)
        lb = jnp.sum(sm[: layer + 1], axis=0, keepdims=True)
        f_mid = 0.5 * (1.0 + lb)
        f_half = 0.5 * (1.0 - lb)
        col_in_part = jnp.bitwise_and(lax.broadcasted_iota(jnp.int32, (c, 3 * c), 1), c - 1)
        tri3 = jnp.where(lax.broadcasted_iota(jnp.int32, (c, 3 * c), 0) >= col_in_part, 1.0, 0.0).astype(BF16)
        cast_weights()

        def finish(r0, q, f_pre):
            f = f_mid + f_half * jnp.tanh(0.5 * f_pre)
            lf = jnp.log(f)
            hi = lf.astype(BF16)
            r1 = lf - hi.astype(F32)
            mid = r1.astype(BF16)
            lo = (r1 - mid.astype(F32)).astype(BF16)
            for k0 in range(0, rt, c):
                sl = slice(k0, k0 + c)
                bcum = jnp.dot(tri3, jnp.concatenate([hi[sl], mid[sl], lo[sl]], axis=0), preferred_element_type=F32)
                b_mid = bcum[c // 2:c // 2 + 1, :]
                b_last = bcum[c - 1:c, :]
                rows = pl.ds(r0 + k0, c)
                put(qa_ref, rows, _silu(q[sl]) * jnp.exp(bcum - b_mid))
                put(ka_ref, rows, (1.0 - f[sl]) * jnp.exp(b_mid - bcum))
                n = (r0 + k0) // c
                rv_ref[0, n:n + 1, :] = jnp.exp(b_mid)
                rv_ref[1, n:n + 1, :] = jnp.exp(b_last - b_mid)
                rv_ref[2, n:n + 1, :] = jnp.exp(b_last)

        pending = None
        for r0 in range(0, bm, rt):
            dots = two_dots(r0)
            if pending is not None:
                finish(*pending)
            pending = (r0,) + dots
        finish(*pending)

    @pl.when(j >= n_tiles_per_kind)
    def _():
        cast_weights()
        for r0 in range(0, bm, rt):
            i_proj, g_proj = two_dots(r0)
            rows = pl.ds(r0, rt)
            put(v_ref, rows, i_proj)
            put(gs_ref, rows, _silu(g_proj) * nw_ref[...])


def _hg_prep(h, w_in, col0, lower_bounds_param, norm_w, layer, name, bm, bn):
    t, k = h.shape
    base = col0 // bn
    nk = HG_WIDTH // bn
    hpt = bn // HG_KEY
    n_layers = lower_bounds_param.shape[0]
    head_shape = jax.ShapeDtypeStruct((HG_HEADS, t, HG_KEY), BF16)
    first = lambda j: jnp.minimum(j, nk - 1)
    second = lambda j: jnp.maximum(j - nk, 0)
    head_block = lambda sel: pl.BlockSpec((hpt, bm, HG_KEY), lambda i, j: (sel(j), i, 0))
    return pl.pallas_call(
        functools.partial(_hg_prep_kernel, layer),
        out_shape=[head_shape, head_shape, head_shape, head_shape,
                   jax.ShapeDtypeStruct((3, t // HG_CHUNK, HG_WIDTH), F32)],
        grid=(t // bm, 2 * nk),
        in_specs=[
            pl.BlockSpec((bm, k), lambda i, j: (i, 0)),
            pl.BlockSpec((k, bn), lambda i, j: (0, base + j + nk * (j // nk))),
            pl.BlockSpec((k, bn), lambda i, j: (0, base + nk + j + nk * (j // nk))),
            pl.BlockSpec((n_layers, bn), lambda i, j: (0, first(j))),
            pl.BlockSpec((1, bn), lambda i, j: (0, 0)),
        ],
        out_specs=[head_block(first), head_block(first), head_block(second), head_block(second),
                   pl.BlockSpec((3, bm // HG_CHUNK, bn), lambda i, j: (0, i, first(j)))],
        scratch_shapes=[pltpu.VMEM((k, bn), BF16), pltpu.VMEM((k, bn), BF16)],
        compiler_params=_params(2),
        name=name,
    )(h, w_in, w_in, lower_bounds_param, jnp.tile(norm_w, hpt).reshape(1, bn))


def _hgrn_scan_kernel(qa_ref, ka_ref, v_ref, gs_ref, rv_ref, o_ref, state):
    ts = qa_ref.shape[1]
    c = HG_CHUNK

    @pl.when(pl.program_id(1) == 0)
    def _():
        state[...] = jnp.zeros_like(state)

    causal = lax.broadcasted_iota(jnp.int32, (c, c), 0) >= lax.broadcasted_iota(jnp.int32, (c, c), 1)
    heads = [pl.ds(h * HG_KEY, HG_KEY) for h in range(HG_HEADS)]
    for n in range(ts // c):
        rows = pl.ds(n * c, c)
        both = []
        for h, cols in enumerate(heads):
            rhs = jnp.concatenate([(state[h] * rv_ref[0, n:n + 1, cols]).astype(BF16), ka_ref[h, rows, :]], axis=0)
            both.append(lax.dot_general(qa_ref[h, rows, :], rhs, (((1,), (1,)), ((), ())), preferred_element_type=F32))
        kv_t = [lax.dot_general(v_ref[h, rows, :], ka_ref[h, rows, :], (((0,), (0,)), ((), ())), preferred_element_type=F32)
                for h in range(HG_HEADS)]
        for h, cols in enumerate(heads):
            att = jnp.where(causal, both[h][:, HG_VAL:], 0.0).astype(BF16)
            o = both[h][:, :HG_VAL] + jnp.dot(att, v_ref[h, rows, :], preferred_element_type=F32)
            o = o * lax.rsqrt(jnp.mean(o * o, axis=-1, keepdims=True) + RMS_EPS) * gs_ref[h, rows, :].astype(F32)
            o_ref[rows, cols] = o.astype(o_ref.dtype)
        for h, cols in enumerate(heads):
            state[h] = rv_ref[2, n:n + 1, cols] * state[h] + kv_t[h] * rv_ref[1, n:n + 1, cols]


def _hgrn_scan(qa, ka, v, gs, rv, b, name, ts):
    s = qa.shape[1] // b
    nt = s // ts
    head_block = pl.BlockSpec((HG_HEADS, ts, HG_KEY), lambda bi, t: (0, bi * nt + t, 0))
    return pl.pallas_call(
        _hgrn_scan_kernel,
        out_shape=jax.ShapeDtypeStruct((b, s, HG_WIDTH), BF16),
        grid=(b, nt),
        in_specs=[head_block, head_block, head_block, head_block,
                  pl.BlockSpec((3, ts // HG_CHUNK, HG_WIDTH), lambda bi, t: (0, bi * nt + t, 0))],
        out_specs=pl.BlockSpec((None, ts, HG_WIDTH), lambda bi, t: (bi, t, 0)),
        scratch_shapes=[pltpu.VMEM((HG_HEADS, HG_VAL, HG_KEY), F32)],
        compiler_params=_params(2),
        name=name,
    )(qa, ka, v, gs, rv)


def kernel(x, mem, ln_mix_w, w_in, hg_norm_w, hg_lower_bounds, w_branch_a, w_branch_b, w_out, ln_cross_w, ln_mem_w, wq_cross, wkv_cross, wo_cross, ln_ffn_w, w1, w3, w2, ln_final_w):
    b, s, d = x.shape
    t = b * s
    depth = w_in.shape[0]
    qkv_w = 3 * ATT_WIDTH
    hg_w = 4 * HG_WIDTH

    xf = x.reshape(t, d)
    for l in range(depth):
        h = _rmsnorm(xf, ln_mix_w[l], BF16, "rms_mix", bm=1024)
        qkv = _matmul(h, w_in[l], 0, qkv_w, BF16, "proj_qkv", bm=2048, bn=768, head_major=True)
        qa, ka, v, gate, rv = _hg_prep(h, w_in[l], qkv_w, hg_lower_bounds, hg_norm_w[l], l, "proj_hg_prep", bm=1024, bn=512)
        o_att = _dilated_attention(qkv, b, "dilated_attn")
        o_hg = _hgrn_scan(qa, ka, v, gate, rv, b, "hgrn2_scan", ts=1024)
        merged, (w_out_b, wq_b, wkv_b, wo_b) = _merge(
            h, o_att.reshape(t, ATT_OUT_WIDTH), o_hg.reshape(t, HG_WIDTH), w_in[l], qkv_w + hg_w,
            w_branch_a[l], w_branch_b[l], "gate_merge", bm=1024, bn=512,
            riders=(w_out[l], wq_cross[l], wkv_cross[l], wo_cross[l]))
        x2, hf = _mid_block(merged.reshape(b, s, d), xf.reshape(b, s, d), mem, w_out_b, wq_b, wkv_b, wo_b,
                            ln_cross_w[l], ln_mem_w[l], ln_ffn_w[l], "out_proj_cross", bm=512)
        xf, hf = x2.reshape(t, d), hf.reshape(t, d)
        act, (w2_b,) = _swiglu(hf, w1[l], w3[l], "ffn_up", bm=2048, bn=512, riders=(w2[l],))
        if l + 1 < depth:
            xf = _matmul_residual(act, w2_b, xf, "ffn_down", bm=512, bn=512)
        else:
            out = _matmul_residual_norm(act, w2_b, xf, ln_final_w, F32, False, "ffn_down_final", bm=512,
                                        single_buffer_w=True)
    return out.reshape(b, s, d)
```

```python
import functools

import jax
import jax.numpy as jnp
from jax import lax
from jax.experimental import pallas as pl
from jax.experimental.pallas import tpu as pltpu

F32 = jnp.float32
BF16 = jnp.bfloat16

HEAD_DIM = 128
ATT_GROUPS = ((128, 1), (512, 4), (2048, 16))
ATT_HEADS_PER_GROUP = 4
N_ATT_GROUPS = len(ATT_GROUPS)
ATT_WIDTH = N_ATT_GROUPS * ATT_HEADS_PER_GROUP * HEAD_DIM
ATT_OUT_WIDTH = ATT_HEADS_PER_GROUP * HEAD_DIM
ATT_BLOCK = 128
HG_HEADS = 8
HG_KEY = 128
HG_VAL = 128
HG_WIDTH = HG_HEADS * HG_KEY
HG_CHUNK = 64
CROSS_HEADS = 4
CROSS_WIDTH = CROSS_HEADS * HEAD_DIM
RMS_EPS = 1e-6
LOG2_E = 1.4426950408889634

V7X_VMEM_LIMIT_BYTES = 56 * 1024 * 1024
MXU_ROW_TILE = 256
MXU_COL_TILE = 512
BF16_SUBLANES = 16

RMS_ROWS = 1024
PROJ_QKV_TILE = (2048, 768)
HG_PREP_TILE = (1024, 512)
GATE_MERGE_TILE = (1024, 512)
MID_ROWS = 512
FFN_UP_TILE = (2048, 512)
FFN_DOWN_TILE = (512, 512)
HGRN_SCAN_ROWS = 1024


def _sigmoid(x):
    return 0.5 * jnp.tanh(0.5 * x) + 0.5


def _silu(x):
    hx = 0.5 * x
    return hx * jnp.tanh(hx) + hx


def _params(n_axes):
    return pltpu.CompilerParams(
        dimension_semantics=("arbitrary",) * n_axes,
        vmem_limit_bytes=V7X_VMEM_LIMIT_BYTES,
    )


def _rms_kernel(x_ref, w_ref, o_ref):
    x = x_ref[...]
    ms = jnp.mean(x * x, axis=-1, keepdims=True)
    o_ref[...] = (x * lax.rsqrt(ms + RMS_EPS) * w_ref[...]).astype(o_ref.dtype)


def _rmsnorm(x, w, out_dtype, name, bm=512):
    m, d = x.shape
    return pl.pallas_call(
        _rms_kernel,
        out_shape=jax.ShapeDtypeStruct((m, d), out_dtype),
        grid=(m // bm,),
        in_specs=[pl.BlockSpec((bm, d), lambda i: (i, 0)), pl.BlockSpec((1, d), lambda i: (0, 0))],
        out_specs=pl.BlockSpec((bm, d), lambda i: (i, 0)),
        compiler_params=_params(1),
        name=name,
    )(x, w.reshape(1, d))


def _row_tiles(n_rows):
    tile = min(MXU_ROW_TILE, n_rows)
    return [pl.ds(r * tile, tile) for r in range(n_rows // tile)]


def _col_tiles(n_cols):
    tile = min(MXU_COL_TILE, n_cols)
    return [pl.ds(c * tile, tile) for c in range(n_cols // tile)]


def _mm_kernel(a_ref, w_ref, o_ref, w_s):
    w_s[...] = w_ref[...].astype(BF16)
    for rows in _row_tiles(a_ref.shape[0]):
        res = jnp.dot(a_ref[rows, :], w_s[...], preferred_element_type=F32).astype(o_ref.dtype)
        if len(o_ref.shape) == 2:
            o_ref[rows, :] = res
        else:
            for hd in range(o_ref.shape[0]):
                o_ref[hd, rows, :] = res[:, hd * HEAD_DIM:(hd + 1) * HEAD_DIM]


def _matmul(a, w, col0, n, out_dtype, name, bm, bn, head_major=False):
    m, k = a.shape
    cb = col0 // bn
    if head_major:
        out_shape = jax.ShapeDtypeStruct((n // HEAD_DIM, m, HEAD_DIM), out_dtype)
        out_spec = pl.BlockSpec((bn // HEAD_DIM, bm, HEAD_DIM), lambda i, j: (j, i, 0))
    else:
        out_shape = jax.ShapeDtypeStruct((m, n), out_dtype)
        out_spec = pl.BlockSpec((bm, bn), lambda i, j: (i, j))
    return pl.pallas_call(
        _mm_kernel,
        out_shape=out_shape,
        grid=(m // bm, n // bn),
        in_specs=[
            pl.BlockSpec((bm, k), lambda i, j: (i, 0)),
            pl.BlockSpec((k, bn), lambda i, j: (0, j + cb)),
        ],
        out_specs=out_spec,
        scratch_shapes=[pltpu.VMEM((k, bn), BF16)],
        compiler_params=_params(2),
        name=name,
    )(a, w)


def _mm_res_kernel(a_ref, w_ref, r_ref, o_ref):
    o_ref[...] = r_ref[...] + jnp.dot(a_ref[...], w_ref[...], preferred_element_type=F32)


def _matmul_residual(a, w, res, name, bm, bn):
    m, k = a.shape
    n = w.shape[1]
    return pl.pallas_call(
        _mm_res_kernel,
        out_shape=jax.ShapeDtypeStruct((m, n), F32),
        grid=(m // bm, n // bn),
        in_specs=[
            pl.BlockSpec((bm, k), lambda i, j: (i, 0)),
            pl.BlockSpec((k, bn), lambda i, j: (0, j)),
            pl.BlockSpec((bm, bn), lambda i, j: (i, j)),
        ],
        out_specs=pl.BlockSpec((bm, bn), lambda i, j: (i, j)),
        compiler_params=_params(2),
        name=name,
    )(a, w, res)


def _mm_res_norm_kernel(emit_sum, a_ref, w_ref, r_ref, lnw_ref, *out_refs):
    n = w_ref.shape[1]
    for rows in _row_tiles(a_ref.shape[0]):
        a = a_ref[rows, :]
        ys = [r_ref[rows, cols] + jnp.dot(a, w_ref[:, cols], preferred_element_type=F32) for cols in _col_tiles(n)]
        ssq = sum(jnp.sum(y * y, axis=-1, keepdims=True) for y in ys)
        scale = lax.rsqrt(ssq * (1.0 / n) + RMS_EPS)
        for cols, y in zip(_col_tiles(n), ys):
            if emit_sum:
                out_refs[0][rows, cols] = y
            out_refs[-1][rows, cols] = (y * scale * lnw_ref[:, cols]).astype(out_refs[-1].dtype)


def _matmul_residual_norm(a, w, res, lnw, norm_dtype, emit_sum, name, bm, single_buffer_w=False):
    m, k = a.shape
    n = w.shape[1]
    row = lambda i: (i, 0)
    w_spec = (pl.BlockSpec((k, n), lambda i: (0, 0), pipeline_mode=pl.Buffered(1)) if single_buffer_w
              else pl.BlockSpec((k, n), lambda i: (0, 0)))
    norm_shape = jax.ShapeDtypeStruct((m, n), norm_dtype)
    out_shape = (jax.ShapeDtypeStruct((m, n), F32), norm_shape) if emit_sum else norm_shape
    out_specs = (pl.BlockSpec((bm, n), row), pl.BlockSpec((bm, n), row)) if emit_sum else pl.BlockSpec((bm, n), row)
    return pl.pallas_call(
        functools.partial(_mm_res_norm_kernel, emit_sum),
        out_shape=out_shape,
        grid=(m // bm,),
        in_specs=[pl.BlockSpec((bm, k), row), w_spec, pl.BlockSpec((bm, n), row), pl.BlockSpec((1, n), lambda i: (0, 0))],
        out_specs=out_specs,
        compiler_params=_params(1),
        name=name,
    )(a, w, res, lnw.reshape(1, n))


def _convert_riders(rider_in, rider_out):
    for src, dst in zip(rider_in, rider_out):
        dst[...] = src[...].astype(dst.dtype)


def _rider_specs(riders, n_steps, step_of):
    specs, shapes = [], []
    for w in riders:
        rows, cols = w.shape
        assert rows % n_steps == 0 and (rows // n_steps) % BF16_SUBLANES == 0, (w.shape, n_steps)
        specs.append(pl.BlockSpec((rows // n_steps, cols), lambda i, j: (step_of(i, j), 0)))
        shapes.append(jax.ShapeDtypeStruct((rows, cols), BF16))
    return specs, shapes


def _swiglu_kernel(n_riders, a_ref, w1_ref, w3_ref, *rest):
    rider_in, (o_ref, *rider_out), (w1_s, w3_s) = rest[:n_riders], rest[n_riders:2 * n_riders + 1], rest[2 * n_riders + 1:]
    _convert_riders(rider_in, rider_out)
    w1_s[...] = w1_ref[...].astype(BF16)
    w3_s[...] = w3_ref[...].astype(BF16)
    for rows in _row_tiles(a_ref.shape[0]):
        a = a_ref[rows, :]
        u = jnp.dot(a, w1_s[...], preferred_element_type=F32)
        g = jnp.dot(a, w3_s[...], preferred_element_type=F32)
        o_ref[rows, :] = (_silu(u) * g).astype(o_ref.dtype)


def _swiglu(a, w1, w3, name, bm, bn, riders=()):
    m, k = a.shape
    n = w1.shape[1]
    nj = n // bn
    rider_specs, rider_shapes = _rider_specs(riders, (m // bm) * nj, lambda i, j: i * nj + j)
    outs = pl.pallas_call(
        functools.partial(_swiglu_kernel, len(riders)),
        scratch_shapes=[pltpu.VMEM((k, bn), BF16), pltpu.VMEM((k, bn), BF16)],
        out_shape=[jax.ShapeDtypeStruct((m, n), BF16)] + rider_shapes,
        grid=(m // bm, nj),
        in_specs=[
            pl.BlockSpec((bm, k), lambda i, j: (i, 0)),
            pl.BlockSpec((k, bn), lambda i, j: (0, j)),
            pl.BlockSpec((k, bn), lambda i, j: (0, j)),
        ] + rider_specs,
        out_specs=[pl.BlockSpec((bm, bn), lambda i, j: (i, j))] + rider_specs,
        compiler_params=_params(2),
        name=name,
    )(a, w1, w3, *riders)
    return outs[0], tuple(outs[1:])


def _merge_kernel(n_riders, h_ref, oa_ref, oh_ref, wga_ref, wgb_ref, wa_ref, wb_ref, *rest):
    rider_in, (o_ref, *rider_out) = rest[:n_riders], rest[n_riders:2 * n_riders + 1]
    wga_s, wgb_s, wa_s, wb_s = rest[2 * n_riders + 1:]
    _convert_riders(rider_in, rider_out)
    for src, dst in ((wga_ref, wga_s), (wgb_ref, wgb_s), (wa_ref, wa_s), (wb_ref, wb_s)):
        dst[...] = src[...].astype(BF16)
    for rows in _row_tiles(h_ref.shape[0]):
        h = h_ref[rows, :]
        ga = jnp.dot(h, wga_s[...], preferred_element_type=F32)
        gb = jnp.dot(h, wgb_s[...], preferred_element_type=F32)
        pa = jnp.dot(oa_ref[rows, :], wa_s[...], preferred_element_type=F32)
        pb = jnp.dot(oh_ref[rows, :], wb_s[...], preferred_element_type=F32)
        o_ref[rows, :] = (_sigmoid(ga) * pa + _sigmoid(gb) * pb).astype(o_ref.dtype)


def _merge(h, o_att, o_hg, w_in, gate_col0, wa, wb, name, bm, bn, riders=()):
    m, k = h.shape
    d = wa.shape[1]
    ga0 = gate_col0 // bn
    gb0 = (gate_col0 + d) // bn
    nj = d // bn
    rider_specs, rider_shapes = _rider_specs(riders, (m // bm) * nj, lambda i, j: i * nj + j)
    outs = pl.pallas_call(
        functools.partial(_merge_kernel, len(riders)),
        out_shape=[jax.ShapeDtypeStruct((m, d), BF16)] + rider_shapes,
        grid=(m // bm, nj),
        in_specs=[
            pl.BlockSpec((bm, k), lambda i, j: (i, 0)),
            pl.BlockSpec((bm, o_att.shape[1]), lambda i, j: (i, 0)),
            pl.BlockSpec((bm, o_hg.shape[1]), lambda i, j: (i, 0)),
            pl.BlockSpec((k, bn), lambda i, j: (0, j + ga0)),
            pl.BlockSpec((k, bn), lambda i, j: (0, j + gb0)),
            pl.BlockSpec((wa.shape[0], bn), lambda i, j: (0, j)),
            pl.BlockSpec((wb.shape[0], bn), lambda i, j: (0, j)),
        ] + rider_specs,
        out_specs=[pl.BlockSpec((bm, bn), lambda i, j: (i, j))] + rider_specs,
        scratch_shapes=[pltpu.VMEM((k, bn), BF16), pltpu.VMEM((k, bn), BF16),
                        pltpu.VMEM((wa.shape[0], bn), BF16), pltpu.VMEM((wb.shape[0], bn), BF16)],
        compiler_params=_params(2),
        name=name,
    )(h, o_att, o_hg, w_in, w_in, wa, wb, *riders)
    return outs[0], tuple(outs[1:])


def _mid_kernel(mg_ref, x_ref, mem_ref, wout_ref, wq_ref, wkv_ref, wo_ref, lnc_ref, lnm_ref, lnf_ref, x2_ref, hf_ref, kv_ref):
    d = x_ref.shape[1]

    @pl.when(pl.program_id(1) == 0)
    def _():
        mem = mem_ref[...]
        mn = (mem * lax.rsqrt(jnp.mean(mem * mem, axis=-1, keepdims=True) + RMS_EPS) * lnm_ref[...]).astype(BF16)
        kv_ref[...] = jnp.dot(mn, wkv_ref[...], preferred_element_type=F32).astype(kv_ref.dtype)

    row_tiles = _row_tiles(x_ref.shape[0])
    col_tiles = _col_tiles(d)

    def residual_norm(rows, lhs, w_ref, base_ref, ln_ref):
        ys = [base_ref[rows, cols] + jnp.dot(lhs, w_ref[:, cols], preferred_element_type=F32) for cols in col_tiles]
        ssq = sum(jnp.sum(y * y, axis=-1, keepdims=True) for y in ys)
        scale = lax.rsqrt(ssq * (1.0 / d) + RMS_EPS)
        for cols, y in zip(col_tiles, ys):
            x2_ref[rows, cols] = y
            hf_ref[rows, cols] = (y * scale * ln_ref[:, cols]).astype(hf_ref.dtype)

    for rows in row_tiles:
        residual_norm(rows, mg_ref[rows, :], wout_ref, x_ref, lnc_ref)
    qcs = [jnp.dot(hf_ref[rows, :], wq_ref[...], preferred_element_type=F32).astype(BF16) for rows in row_tiles]
    scores = [[lax.dot_general(qc[:, h * HEAD_DIM:(h + 1) * HEAD_DIM], kv_ref[:, h * HEAD_DIM:(h + 1) * HEAD_DIM],
                               (((1,), (1,)), ((), ())), preferred_element_type=F32) * (HEAD_DIM ** -0.5)
               for h in range(CROSS_HEADS)] for qc in qcs]
    ocs = []
    for per_head in scores:
        outs = []
        for h, s in enumerate(per_head):
            m = jnp.max(s, axis=-1, keepdims=True)
            p = jnp.exp(s - m)
            pn = (p * (1.0 / jnp.sum(p, axis=-1, keepdims=True))).astype(BF16)
            v = kv_ref[:, CROSS_WIDTH + h * HEAD_DIM:CROSS_WIDTH + (h + 1) * HEAD_DIM]
            outs.append(jnp.dot(pn, v, preferred_element_type=F32).astype(BF16))
        ocs.append(jnp.concatenate(outs, axis=-1))
    for rows, oc in zip(row_tiles, ocs):
        residual_norm(rows, oc, wo_ref, x2_ref, lnf_ref)


def _mid_block(merged, x, mem, w_out, wq, wkv, wo, ln_cross, ln_mem, ln_ffn, name, bm):
    b, s, d = x.shape
    mlen = mem.shape[1]
    tile = pl.BlockSpec((None, bm, d), lambda bi, i: (bi, i, 0))

    def resident(shape):
        return pl.BlockSpec(shape, lambda bi, i: (0,) * len(shape), pipeline_mode=pl.Buffered(1))

    return pl.pallas_call(
        _mid_kernel,
        out_shape=(jax.ShapeDtypeStruct((b, s, d), F32), jax.ShapeDtypeStruct((b, s, d), BF16)),
        grid=(b, s // bm),
        in_specs=[
            tile, tile,
            pl.BlockSpec((None, mlen, d), lambda bi, i: (bi, 0, 0)),
            resident(w_out.shape), resident(wq.shape), resident(wkv.shape), resident(wo.shape),
            resident((1, d)), resident((1, d)), resident((1, d)),
        ],
        out_specs=(tile, tile),
        scratch_shapes=[pltpu.VMEM((mlen, wkv.shape[1]), BF16)],
        compiler_params=_params(2),
        name=name,
    )(merged, x, mem, w_out, wq, wkv, wo, ln_cross.reshape(1, d), ln_mem.reshape(1, d), ln_ffn.reshape(1, d))


def _block_scores(q, kk):
    return lax.dot_general(q, kk, (((1,), (1,)), ((), ())), preferred_element_type=F32)


def _block_softmax_pv(s, vv, mask):
    scale = HEAD_DIM ** -0.5
    s = jnp.where(mask, s, -jnp.inf)
    m = jnp.max(s, axis=-1, keepdims=True)
    p = jnp.exp2((s - m) * (scale * LOG2_E))
    l = jnp.sum(p, axis=-1, keepdims=True)
    o = jnp.dot(p.astype(BF16), vv, preferred_element_type=F32) * (1.0 / l)
    return o, m * scale + jnp.log(l)


ATT_BLOCKS_IN_FLIGHT = 4


def _attn_kernel(q0, k0, v0, q1, k1, v1, q2, k2, v2, o_ref, st, og, ls):
    seq = q0.shape[0]
    blk = ATT_BLOCK
    for idx, ref in enumerate((q1, k1, v1, q2, k2, v2)):
        st[idx] = ref[...].astype(F32)

    qi = lax.broadcasted_iota(jnp.int32, (blk, 2 * blk), 0)
    kj = lax.broadcasted_iota(jnp.int32, (blk, 2 * blk), 1)
    dist = qi - kj + blk
    first_mask = (lax.broadcasted_iota(jnp.int32, (blk, blk), 0) >= lax.broadcasted_iota(jnp.int32, (blk, blk), 1))

    srcs = ((q0, k0, v0), (0, 1, 2), (3, 4, 5))

    def rows(dil, r, first_blk, count):
        start = first_blk * blk * dil + r
        if dil == 1:
            return pl.ds(start, count)
        return pl.ds(start, count, stride=dil)

    def load(g, which, rws):
        if g == 0:
            return srcs[0][which][rws, :]
        return st[srcs[g][which], rws, :].astype(BF16)

    per_group = []
    for g, (window, dil) in enumerate(ATT_GROUPS):
        band_mask = (dist >= 0) & (dist <= window // dil)
        group_tasks = []
        for r in range(dil):
            for n in range(seq // dil // blk):
                if n == 0:
                    group_tasks.append((g, rows(dil, r, 0, blk), rows(dil, r, 0, blk), first_mask))
                else:
                    group_tasks.append((g, rows(dil, r, n, blk), rows(dil, r, n - 1, 2 * blk), band_mask))
        per_group.append(group_tasks)
    tasks = [t for trio in zip(*per_group) for t in trio]

    def finish(batch, scores):
        for (g, q_rows, k_rows, mask), s in zip(batch, scores):
            o, lse = _block_softmax_pv(s, load(g, 2, k_rows), mask)
            og[g, q_rows, :] = o
            ls[g, q_rows, :] = jnp.broadcast_to(lse, (blk, HEAD_DIM))

    pending = None
    for t0 in range(0, len(tasks), ATT_BLOCKS_IN_FLIGHT):
        batch = tasks[t0:t0 + ATT_BLOCKS_IN_FLIGHT]
        scores = [_block_scores(load(g, 0, q_rows), load(g, 1, k_rows)) for g, q_rows, k_rows, _ in batch]
        if pending is not None:
            finish(*pending)
        pending = (batch, scores)
    finish(*pending)

    for n in range(seq // blk):
        sl = pl.ds(n * blk, blk)
        l0, l1, l2 = ls[0, sl, :], ls[1, sl, :], ls[2, sl, :]
        mx = jnp.maximum(jnp.maximum(l0, l1), l2)
        e0, e1, e2 = jnp.exp(l0 - mx), jnp.exp(l1 - mx), jnp.exp(l2 - mx)
        inv = 1.0 / (e0 + e1 + e2)
        out = (e0 * inv) * og[0, sl, :] + (e1 * inv) * og[1, sl, :] + (e2 * inv) * og[2, sl, :]
        o_ref[sl, :] = out.astype(o_ref.dtype)


def _dilated_attention(qkv, b, name):
    s = qkv.shape[1] // b
    hpg = ATT_HEADS_PER_GROUP
    n_cols = ATT_WIDTH // HEAD_DIM

    def spec(which, g):
        return pl.BlockSpec((None, s, HEAD_DIM), lambda bi, h: (which * n_cols + g * hpg + h, bi, 0))

    in_specs = [spec(which, g) for g in range(N_ATT_GROUPS) for which in range(3)]
    return pl.pallas_call(
        _attn_kernel,
        out_shape=jax.ShapeDtypeStruct((b, s, ATT_OUT_WIDTH), BF16),
        grid=(b, hpg),
        in_specs=in_specs,
        out_specs=pl.BlockSpec((None, s, HEAD_DIM), lambda bi, h: (bi, 0, h)),
        scratch_shapes=[
            pltpu.VMEM((6, s, HEAD_DIM), F32),
            pltpu.VMEM((3, s, HEAD_DIM), F32),
            pltpu.VMEM((3, s, HEAD_DIM), F32),
        ],
        compiler_params=_params(2),
        name=name,
    )(*([qkv] * 9))


def _hg_prep_kernel(layer, a_ref, wa_ref, wb_ref, lbp_ref, nw_ref, qa_ref, ka_ref, v_ref, gs_ref, rv_ref, wa_s, wb_s):
    j = pl.program_id(1)
    c = HG_CHUNK
    bm = a_ref.shape[0]
    rt = MXU_ROW_TILE
    heads_per_tile = wa_ref.shape[1] // HG_KEY
    n_tiles_per_kind = HG_WIDTH // wa_ref.shape[1]

    def cast_weights():
        wa_s[...] = wa_ref[...].astype(BF16)
        wb_s[...] = wb_ref[...].astype(BF16)

    def put(dst, rows, val):
        for hd in range(heads_per_tile):
            dst[hd, rows, :] = val[:, hd * HG_KEY:(hd + 1) * HG_KEY].astype(dst.dtype)

    def two_dots(r0):
        x = a_ref[pl.ds(r0, rt), :]
        return (jnp.dot(x, wa_s[...], preferred_element_type=F32), jnp.dot(x, wb_s[...], preferred_element_type=F32))

    @pl.when(j < n_tiles_per_kind)
    def _():
        p = lbp_ref[...]
        e = jnp.exp(p - jnp.max(p, axis=0, keepdims=True))
        sm = e / jnp.sum(e, axis=0, keepdims=True)
        lb = jnp.sum(sm[: layer + 1], axis=0, keepdims=True)
        f_mid = 0.5 * (1.0 + lb)
        f_half = 0.5 * (1.0 - lb)
        col_in_part = jnp.bitwise_and(lax.broadcasted_iota(jnp.int32, (c, 3 * c), 1), c - 1)
        tri3 = jnp.where(lax.broadcasted_iota(jnp.int32, (c, 3 * c), 0) >= col_in_part, 1.0, 0.0).astype(BF16)
        cast_weights()

        def finish(r0, q, f_pre):
            f = f_mid + f_half * jnp.tanh(0.5 * f_pre)
            lf = jnp.log(f)
            hi = lf.astype(BF16)
            r1 = lf - hi.astype(F32)
            mid = r1.astype(BF16)
            lo = (r1 - mid.astype(F32)).astype(BF16)
            for k0 in range(0, rt, c):
                sl = slice(k0, k0 + c)
                bcum = jnp.dot(tri3, jnp.concatenate([hi[sl], mid[sl], lo[sl]], axis=0), preferred_element_type=F32)
                b_mid = bcum[c // 2:c // 2 + 1, :]
                b_last = bcum[c - 1:c, :]
                rows = pl.ds(r0 + k0, c)
                put(qa_ref, rows, _silu(q[sl]) * jnp.exp(bcum - b_mid))
                put(ka_ref, rows, (1.0 - f[sl]) * jnp.exp(b_mid - bcum))
                n = (r0 + k0) // c
                rv_ref[0, n:n + 1, :] = jnp.exp(b_mid)
                rv_ref[1, n:n + 1, :] = jnp.exp(b_last - b_mid)
                rv_ref[2, n:n + 1, :] = jnp.exp(b_last)

        pending = None
        for r0 in range(0, bm, rt):
            dots = two_dots(r0)
            if pending is not None:
                finish(*pending)
            pending = (r0,) + dots
        finish(*pending)

    @pl.when(j >= n_tiles_per_kind)
    def _():
        cast_weights()
        for r0 in range(0, bm, rt):
            i_proj, g_proj = two_dots(r0)
            rows = pl.ds(r0, rt)
            put(v_ref, rows, i_proj)
            put(gs_ref, rows, _silu(g_proj) * nw_ref[...])


def _hg_prep(h, w_in, col0, lower_bounds_param, norm_w, layer, name, bm, bn):
    t, k = h.shape
    base = col0 // bn
    nk = HG_WIDTH // bn
    hpt = bn // HG_KEY
    n_layers = lower_bounds_param.shape[0]
    head_shape = jax.ShapeDtypeStruct((HG_HEADS, t, HG_KEY), BF16)
    first = lambda j: jnp.minimum(j, nk - 1)
    second = lambda j: jnp.maximum(j - nk, 0)
    head_block = lambda sel: pl.BlockSpec((hpt, bm, HG_KEY), lambda i, j: (sel(j), i, 0))
    return pl.pallas_call(
        functools.partial(_hg_prep_kernel, layer),
        out_shape=[head_shape, head_shape, head_shape, head_shape,
                   jax.ShapeDtypeStruct((3, t // HG_CHUNK, HG_WIDTH), F32)],
        grid=(t // bm, 2 * nk),
        in_specs=[
            pl.BlockSpec((bm, k), lambda i, j: (i, 0)),
            pl.BlockSpec((k, bn), lambda i, j: (0, base + j + nk * (j // nk))),
            pl.BlockSpec((k, bn), lambda i, j: (0, base + nk + j + nk * (j // nk))),
            pl.BlockSpec((n_layers, bn), lambda i, j: (0, first(j))),
            pl.BlockSpec((1, bn), lambda i, j: (0, 0)),
        ],
        out_specs=[head_block(first), head_block(first), head_block(second), head_block(second),
                   pl.BlockSpec((3, bm // HG_CHUNK, bn), lambda i, j: (0, i, first(j)))],
        scratch_shapes=[pltpu.VMEM((k, bn), BF16), pltpu.VMEM((k, bn), BF16)],
        compiler_params=_params(2),
        name=name,
    )(h, w_in, w_in, lower_bounds_param, jnp.tile(norm_w, hpt).reshape(1, bn))


def _hgrn_scan_kernel(qa_ref, ka_ref, v_ref, gs_ref, rv_ref, o_ref, state):
    ts = qa_ref.shape[1]
    c = HG_CHUNK

    @pl.when(pl.program_id(1) == 0)
    def _():
        state[...] = jnp.zeros_like(state)

    causal = lax.broadcasted_iota(jnp.int32, (c, c), 0) >= lax.broadcasted_iota(jnp.int32, (c, c), 1)
    heads = [pl.ds(h * HG_KEY, HG_KEY) for h in range(HG_HEADS)]
    for n in range(ts // c):
        rows = pl.ds(n * c, c)
        both = []
        for h, cols in enumerate(heads):
            rhs = jnp.concatenate([(state[h] * rv_ref[0, n:n + 1, cols]).astype(BF16), ka_ref[h, rows, :]], axis=0)
            both.append(lax.dot_general(qa_ref[h, rows, :], rhs, (((1,), (1,)), ((), ())), preferred_element_type=F32))
        kv_t = [lax.dot_general(v_ref[h, rows, :], ka_ref[h, rows, :], (((0,), (0,)), ((), ())), preferred_element_type=F32)
                for h in range(HG_HEADS)]
        for h, cols in enumerate(heads):
            att = jnp.where(causal, both[h][:, HG_VAL:], 0.0).astype(BF16)
            o = both[h][:, :HG_VAL] + jnp.dot(att, v_ref[h, rows, :], preferred_element_type=F32)
            o = o * lax.rsqrt(jnp.mean(o * o, axis=-1, keepdims=True) + RMS_EPS) * gs_ref[h, rows, :].astype(F32)
            o_ref[rows, cols] = o.astype(o_ref.dtype)
        for h, cols in enumerate(heads):
            state[h] = rv_ref[2, n:n + 1, cols] * state[h] + kv_t[h] * rv_ref[1, n:n + 1, cols]


def _hgrn_scan(qa, ka, v, gs, rv, b, name, ts):
    s = qa.shape[1] // b
    nt = s // ts
    head_block = pl.BlockSpec((HG_HEADS, ts, HG_KEY), lambda bi, t: (0, bi * nt + t, 0))
    return pl.pallas_call(
        _hgrn_scan_kernel,
        out_shape=jax.ShapeDtypeStruct((b, s, HG_WIDTH), BF16),
        grid=(b, nt),
        in_specs=[head_block, head_block, head_block, head_block,
                  pl.BlockSpec((3, ts // HG_CHUNK, HG_WIDTH), lambda bi, t: (0, bi * nt + t, 0))],
        out_specs=pl.BlockSpec((None, ts, HG_WIDTH), lambda bi, t: (bi, t, 0)),
        scratch_shapes=[pltpu.VMEM((HG_HEADS, HG_VAL, HG_KEY), F32)],
        compiler_params=_params(2),
        name=name,
    )(qa, ka, v, gs, rv)


def kernel(x, mem, ln_mix_w, w_in, hg_norm_w, hg_lower_bounds, w_branch_a, w_branch_b, w_out, ln_cross_w, ln_mem_w, wq_cross, wkv_cross, wo_cross, ln_ffn_w, w1, w3, w2, ln_final_w):
    b, s, d = x.shape
    t = b * s
    depth = w_in.shape[0]
    qkv_w = 3 * ATT_WIDTH
    hg_w = 4 * HG_WIDTH
    assert w_in.shape[2] == qkv_w + hg_w + 2 * d, w_in.shape
    assert s % HGRN_SCAN_ROWS == 0 and s % MID_ROWS == 0 and s % (ATT_BLOCK * ATT_GROUPS[-1][1]) == 0, s
    for rows in (RMS_ROWS, PROJ_QKV_TILE[0], HG_PREP_TILE[0], GATE_MERGE_TILE[0], FFN_UP_TILE[0], FFN_DOWN_TILE[0]):
        assert t % rows == 0, (t, rows)

    xf = x.reshape(t, d)
    for l in range(depth):
        h = _rmsnorm(xf, ln_mix_w[l], BF16, "rms_mix", bm=RMS_ROWS)
        qkv = _matmul(h, w_in[l], 0, qkv_w, BF16, "proj_qkv", *PROJ_QKV_TILE, head_major=True)
        qa, ka, v, gate, rv = _hg_prep(h, w_in[l], qkv_w, hg_lower_bounds, hg_norm_w[l], l, "proj_hg_prep", *HG_PREP_TILE)
        o_att = _dilated_attention(qkv, b, "dilated_attn")
        o_hg = _hgrn_scan(qa, ka, v, gate, rv, b, "hgrn2_scan", ts=HGRN_SCAN_ROWS)
        merged, (w_out_b, wq_b, wkv_b, wo_b) = _merge(
            h, o_att.reshape(t, ATT_OUT_WIDTH), o_hg.reshape(t, HG_WIDTH), w_in[l], qkv_w + hg_w,
            w_branch_a[l], w_branch_b[l], "gate_merge", *GATE_MERGE_TILE,
            riders=(w_out[l], wq_cross[l], wkv_cross[l], wo_cross[l]))
        x2, hf = _mid_block(merged.reshape(b, s, d), xf.reshape(b, s, d), mem, w_out_b, wq_b, wkv_b, wo_b,
                            ln_cross_w[l], ln_mem_w[l], ln_ffn_w[l], "out_proj_cross", bm=MID_ROWS)
        xf, hf = x2.reshape(t, d), hf.reshape(t, d)
        act, (w2_b,) = _swiglu(hf, w1[l], w3[l], "ffn_up", *FFN_UP_TILE, riders=(w2[l],))
        if l + 1 < depth:
            xf = _matmul_residual(act, w2_b, xf, "ffn_down", *FFN_DOWN_TILE)
        else:
            out = _matmul_residual_norm(act, w2_b, xf, ln_final_w, F32, False, "ffn_down_final", bm=FFN_DOWN_TILE[0],
                                        single_buffer_w=True)
    return out.reshape(b, s, d)
```

```python
import functools

import jax
import jax.numpy as jnp
from jax import lax
from jax.experimental import pallas as pl
from jax.experimental.pallas import tpu as pltpu

F32 = jnp.float32
BF16 = jnp.bfloat16

HEAD_DIM = 128
ATT_GROUPS = ((128, 1), (512, 4), (2048, 16))
ATT_HEADS_PER_GROUP = 4
N_ATT_GROUPS = len(ATT_GROUPS)
ATT_WIDTH = N_ATT_GROUPS * ATT_HEADS_PER_GROUP * HEAD_DIM
ATT_OUT_WIDTH = ATT_HEADS_PER_GROUP * HEAD_DIM
ATT_BLOCK = 128
HG_HEADS = 8
HG_KEY = 128
HG_VAL = 128
HG_WIDTH = HG_HEADS * HG_KEY
HG_CHUNK = 64
CROSS_HEADS = 4
CROSS_WIDTH = CROSS_HEADS * HEAD_DIM
RMS_EPS = 1e-6
LOG2_E = 1.4426950408889634

V7X_VMEM_LIMIT_BYTES = 56 * 1024 * 1024
MXU_ROW_TILE = 256
MXU_ROW_TILE_MULTI_DOT = 128
MXU_COL_TILE = 512
BF16_SUBLANES = 16

RMS_ROWS = 1024
PROJ_QKV_TILE = (2048, 768)
HG_PREP_TILE = (1024, 512)
GATE_MERGE_TILE = (1024, 512)
MID_ROWS = 512
FFN_UP_TILE = (2048, 512)
FFN_DOWN_TILE = (512, 512)
HGRN_SCAN_ROWS = 1024


def _sigmoid(x):
    return 0.5 * jnp.tanh(0.5 * x) + 0.5


def _silu(x):
    hx = 0.5 * x
    return hx * jnp.tanh(hx) + hx


def _params(n_axes):
    return pltpu.CompilerParams(
        dimension_semantics=("arbitrary",) * n_axes,
        vmem_limit_bytes=V7X_VMEM_LIMIT_BYTES,
    )


def _rms_kernel(x_ref, w_ref, o_ref):
    x = x_ref[...]
    ms = jnp.mean(x * x, axis=-1, keepdims=True)
    o_ref[...] = (x * lax.rsqrt(ms + RMS_EPS) * w_ref[...]).astype(o_ref.dtype)


def _rmsnorm(x, w, out_dtype, name, bm=512):
    m, d = x.shape
    return pl.pallas_call(
        _rms_kernel,
        out_shape=jax.ShapeDtypeStruct((m, d), out_dtype),
        grid=(m // bm,),
        in_specs=[pl.BlockSpec((bm, d), lambda i: (i, 0)), pl.BlockSpec((1, d), lambda i: (0, 0))],
        out_specs=pl.BlockSpec((bm, d), lambda i: (i, 0)),
        compiler_params=_params(1),
        name=name,
    )(x, w.reshape(1, d))


def _row_tiles(n_rows, tile=MXU_ROW_TILE):
    tile = min(tile, n_rows)
    return [pl.ds(r * tile, tile) for r in range(n_rows // tile)]


def _col_tiles(n_cols):
    tile = min(MXU_COL_TILE, n_cols)
    return [pl.ds(c * tile, tile) for c in range(n_cols // tile)]


def _mm_kernel(a_ref, w_ref, o_ref, w_s):
    w_s[...] = w_ref[...].astype(BF16)
    for rows in _row_tiles(a_ref.shape[0]):
        res = jnp.dot(a_ref[rows, :], w_s[...], preferred_element_type=F32).astype(o_ref.dtype)
        if len(o_ref.shape) == 2:
            o_ref[rows, :] = res
        else:
            for hd in range(o_ref.shape[0]):
                o_ref[hd, rows, :] = res[:, hd * HEAD_DIM:(hd + 1) * HEAD_DIM]


def _matmul(a, w, col0, n, out_dtype, name, bm, bn, head_major=False):
    m, k = a.shape
    cb = col0 // bn
    if head_major:
        out_shape = jax.ShapeDtypeStruct((n // HEAD_DIM, m, HEAD_DIM), out_dtype)
        out_spec = pl.BlockSpec((bn // HEAD_DIM, bm, HEAD_DIM), lambda i, j: (j, i, 0))
    else:
        out_shape = jax.ShapeDtypeStruct((m, n), out_dtype)
        out_spec = pl.BlockSpec((bm, bn), lambda i, j: (i, j))
    return pl.pallas_call(
        _mm_kernel,
        out_shape=out_shape,
        grid=(m // bm, n // bn),
        in_specs=[
            pl.BlockSpec((bm, k), lambda i, j: (i, 0)),
            pl.BlockSpec((k, bn), lambda i, j: (0, j + cb)),
        ],
        out_specs=out_spec,
        scratch_shapes=[pltpu.VMEM((k, bn), BF16)],
        compiler_params=_params(2),
        name=name,
    )(a, w)


def _mm_res_kernel(a_ref, w_ref, r_ref, o_ref):
    o_ref[...] = r_ref[...] + jnp.dot(a_ref[...], w_ref[...], preferred_element_type=F32)


def _matmul_residual(a, w, res, name, bm, bn):
    m, k = a.shape
    n = w.shape[1]
    return pl.pallas_call(
        _mm_res_kernel,
        out_shape=jax.ShapeDtypeStruct((m, n), F32),
        grid=(m // bm, n // bn),
        in_specs=[
            pl.BlockSpec((bm, k), lambda i, j: (i, 0)),
            pl.BlockSpec((k, bn), lambda i, j: (0, j)),
            pl.BlockSpec((bm, bn), lambda i, j: (i, j)),
        ],
        out_specs=pl.BlockSpec((bm, bn), lambda i, j: (i, j)),
        compiler_params=_params(2),
        name=name,
    )(a, w, res)


def _mm_res_norm_kernel(emit_sum, a_ref, w_ref, r_ref, lnw_ref, *out_refs):
    n = w_ref.shape[1]
    for rows in _row_tiles(a_ref.shape[0]):
        a = a_ref[rows, :]
        ys = [r_ref[rows, cols] + jnp.dot(a, w_ref[:, cols], preferred_element_type=F32) for cols in _col_tiles(n)]
        ssq = sum(jnp.sum(y * y, axis=-1, keepdims=True) for y in ys)
        scale = lax.rsqrt(ssq * (1.0 / n) + RMS_EPS)
        for cols, y in zip(_col_tiles(n), ys):
            if emit_sum:
                out_refs[0][rows, cols] = y
            out_refs[-1][rows, cols] = (y * scale * lnw_ref[:, cols]).astype(out_refs[-1].dtype)


def _matmul_residual_norm(a, w, res, lnw, norm_dtype, emit_sum, name, bm, single_buffer_w=False):
    m, k = a.shape
    n = w.shape[1]
    row = lambda i: (i, 0)
    w_spec = (pl.BlockSpec((k, n), lambda i: (0, 0), pipeline_mode=pl.Buffered(1)) if single_buffer_w
              else pl.BlockSpec((k, n), lambda i: (0, 0)))
    norm_shape = jax.ShapeDtypeStruct((m, n), norm_dtype)
    out_shape = (jax.ShapeDtypeStruct((m, n), F32), norm_shape) if emit_sum else norm_shape
    out_specs = (pl.BlockSpec((bm, n), row), pl.BlockSpec((bm, n), row)) if emit_sum else pl.BlockSpec((bm, n), row)
    return pl.pallas_call(
        functools.partial(_mm_res_norm_kernel, emit_sum),
        out_shape=out_shape,
        grid=(m // bm,),
        in_specs=[pl.BlockSpec((bm, k), row), w_spec, pl.BlockSpec((bm, n), row), pl.BlockSpec((1, n), lambda i: (0, 0))],
        out_specs=out_specs,
        compiler_params=_params(1),
        name=name,
    )(a, w, res, lnw.reshape(1, n))


def _convert_riders(rider_in, rider_out):
    for src, dst in zip(rider_in, rider_out):
        dst[...] = src[...].astype(dst.dtype)


def _rider_specs(riders, n_steps, step_of):
    specs, shapes = [], []
    for w in riders:
        rows, cols = w.shape
        assert rows % n_steps == 0 and (rows // n_steps) % BF16_SUBLANES == 0, (w.shape, n_steps)
        specs.append(pl.BlockSpec((rows // n_steps, cols), lambda i, j: (step_of(i, j), 0)))
        shapes.append(jax.ShapeDtypeStruct((rows, cols), BF16))
    return specs, shapes


def _swiglu_kernel(n_riders, a_ref, w1_ref, w3_ref, *rest):
    rider_in, (o_ref, *rider_out), (w1_s, w3_s) = rest[:n_riders], rest[n_riders:2 * n_riders + 1], rest[2 * n_riders + 1:]
    _convert_riders(rider_in, rider_out)
    w1_s[...] = w1_ref[...].astype(BF16)
    w3_s[...] = w3_ref[...].astype(BF16)
    for rows in _row_tiles(a_ref.shape[0], MXU_ROW_TILE_MULTI_DOT):
        a = a_ref[rows, :]
        u = jnp.dot(a, w1_s[...], preferred_element_type=F32)
        g = jnp.dot(a, w3_s[...], preferred_element_type=F32)
        o_ref[rows, :] = (_silu(u) * g).astype(o_ref.dtype)


def _swiglu(a, w1, w3, name, bm, bn, riders=()):
    m, k = a.shape
    n = w1.shape[1]
    nj = n // bn
    rider_specs, rider_shapes = _rider_specs(riders, (m // bm) * nj, lambda i, j: i * nj + j)
    outs = pl.pallas_call(
        functools.partial(_swiglu_kernel, len(riders)),
        scratch_shapes=[pltpu.VMEM((k, bn), BF16), pltpu.VMEM((k, bn), BF16)],
        out_shape=[jax.ShapeDtypeStruct((m, n), BF16)] + rider_shapes,
        grid=(m // bm, nj),
        in_specs=[
            pl.BlockSpec((bm, k), lambda i, j: (i, 0)),
            pl.BlockSpec((k, bn), lambda i, j: (0, j)),
            pl.BlockSpec((k, bn), lambda i, j: (0, j)),
        ] + rider_specs,
        out_specs=[pl.BlockSpec((bm, bn), lambda i, j: (i, j))] + rider_specs,
        compiler_params=_params(2),
        name=name,
    )(a, w1, w3, *riders)
    return outs[0], tuple(outs[1:])


def _merge_kernel(n_riders, h_ref, oa_ref, oh_ref, wga_ref, wgb_ref, wa_ref, wb_ref, *rest):
    rider_in, (o_ref, *rider_out) = rest[:n_riders], rest[n_riders:2 * n_riders + 1]
    wga_s, wgb_s, wa_s, wb_s = rest[2 * n_riders + 1:]
    _convert_riders(rider_in, rider_out)

    @pl.when(pl.program_id(1) == 0)
    def _():
        for src, dst in ((wga_ref, wga_s), (wgb_ref, wgb_s), (wa_ref, wa_s), (wb_ref, wb_s)):
            dst[...] = src[...].astype(BF16)

    for rows in _row_tiles(h_ref.shape[0], MXU_ROW_TILE_MULTI_DOT):
        h = h_ref[rows, :]
        ga = jnp.dot(h, wga_s[...], preferred_element_type=F32)
        gb = jnp.dot(h, wgb_s[...], preferred_element_type=F32)
        pa = jnp.dot(oa_ref[rows, :], wa_s[...], preferred_element_type=F32)
        pb = jnp.dot(oh_ref[rows, :], wb_s[...], preferred_element_type=F32)
        o_ref[rows, :] = (_sigmoid(ga) * pa + _sigmoid(gb) * pb).astype(o_ref.dtype)


def _merge(h, o_att, o_hg, w_in, gate_col0, wa, wb, name, bm, bn, riders=()):
    m, k = h.shape
    d = wa.shape[1]
    ga0 = gate_col0 // bn
    gb0 = (gate_col0 + d) // bn
    ni = m // bm
    rider_specs, rider_shapes = _rider_specs(riders, (d // bn) * ni, lambda j, i: j * ni + i)
    outs = pl.pallas_call(
        functools.partial(_merge_kernel, len(riders)),
        out_shape=[jax.ShapeDtypeStruct((m, d), BF16)] + rider_shapes,
        grid=(d // bn, ni),
        in_specs=[
            pl.BlockSpec((bm, k), lambda j, i: (i, 0)),
            pl.BlockSpec((bm, o_att.shape[1]), lambda j, i: (i, 0)),
            pl.BlockSpec((bm, o_hg.shape[1]), lambda j, i: (i, 0)),
            pl.BlockSpec((k, bn), lambda j, i: (0, j + ga0)),
            pl.BlockSpec((k, bn), lambda j, i: (0, j + gb0)),
            pl.BlockSpec((wa.shape[0], bn), lambda j, i: (0, j)),
            pl.BlockSpec((wb.shape[0], bn), lambda j, i: (0, j)),
        ] + rider_specs,
        out_specs=[pl.BlockSpec((bm, bn), lambda j, i: (i, j))] + rider_specs,
        scratch_shapes=[pltpu.VMEM((k, bn), BF16), pltpu.VMEM((k, bn), BF16),
                        pltpu.VMEM((wa.shape[0], bn), BF16), pltpu.VMEM((wb.shape[0], bn), BF16)],
        compiler_params=_params(2),
        name=name,
    )(h, o_att, o_hg, w_in, w_in, wa, wb, *riders)
    return outs[0], tuple(outs[1:])


def _mid_kernel(mg_ref, x_ref, mem_ref, wout_ref, wq_ref, wkv_ref, wo_ref, lnc_ref, lnm_ref, lnf_ref, x2_ref, hf_ref, kv_ref):
    d = x_ref.shape[1]

    @pl.when(pl.program_id(1) == 0)
    def _():
        mem = mem_ref[...]
        mn = (mem * lax.rsqrt(jnp.mean(mem * mem, axis=-1, keepdims=True) + RMS_EPS) * lnm_ref[...]).astype(BF16)
        kv_ref[...] = jnp.dot(mn, wkv_ref[...], preferred_element_type=F32).astype(kv_ref.dtype)

    row_tiles = _row_tiles(x_ref.shape[0])
    col_tiles = _col_tiles(d)

    def residual_norm(rows, lhs, w_ref, base_ref, ln_ref):
        ys = [base_ref[rows, cols] + jnp.dot(lhs, w_ref[:, cols], preferred_element_type=F32) for cols in col_tiles]
        ssq = sum(jnp.sum(y * y, axis=-1, keepdims=True) for y in ys)
        scale = lax.rsqrt(ssq * (1.0 / d) + RMS_EPS)
        for cols, y in zip(col_tiles, ys):
            x2_ref[rows, cols] = y
            hf_ref[rows, cols] = (y * scale * ln_ref[:, cols]).astype(hf_ref.dtype)

    for rows in row_tiles:
        residual_norm(rows, mg_ref[rows, :], wout_ref, x_ref, lnc_ref)
    qcs = [jnp.dot(hf_ref[rows, :], wq_ref[...], preferred_element_type=F32).astype(BF16) for rows in row_tiles]
    scores = [[lax.dot_general(qc[:, h * HEAD_DIM:(h + 1) * HEAD_DIM], kv_ref[:, h * HEAD_DIM:(h + 1) * HEAD_DIM],
                               (((1,), (1,)), ((), ())), preferred_element_type=F32) * (HEAD_DIM ** -0.5)
               for h in range(CROSS_HEADS)] for qc in qcs]
    ocs = []
    for per_head in scores:
        outs = []
        for h, s in enumerate(per_head):
            m = jnp.max(s, axis=-1, keepdims=True)
            p = jnp.exp(s - m)
            pn = (p * (1.0 / jnp.sum(p, axis=-1, keepdims=True))).astype(BF16)
            v = kv_ref[:, CROSS_WIDTH + h * HEAD_DIM:CROSS_WIDTH + (h + 1) * HEAD_DIM]
            outs.append(jnp.dot(pn, v, preferred_element_type=F32).astype(BF16))
        ocs.append(jnp.concatenate(outs, axis=-1))
    for rows, oc in zip(row_tiles, ocs):
        residual_norm(rows, oc, wo_ref, x2_ref, lnf_ref)


def _mid_block(merged, x, mem, w_out, wq, wkv, wo, ln_cross, ln_mem, ln_ffn, name, bm):
    b, s, d = x.shape
    mlen = mem.shape[1]
    tile = pl.BlockSpec((None, bm, d), lambda bi, i: (bi, i, 0))

    def resident(shape):
        return pl.BlockSpec(shape, lambda bi, i: (0,) * len(shape), pipeline_mode=pl.Buffered(1))

    return pl.pallas_call(
        _mid_kernel,
        out_shape=(jax.ShapeDtypeStruct((b, s, d), F32), jax.ShapeDtypeStruct((b, s, d), BF16)),
        grid=(b, s // bm),
        in_specs=[
            tile, tile,
            pl.BlockSpec((None, mlen, d), lambda bi, i: (bi, 0, 0)),
            resident(w_out.shape), resident(wq.shape), resident(wkv.shape), resident(wo.shape),
            resident((1, d)), resident((1, d)), resident((1, d)),
        ],
        out_specs=(tile, tile),
        scratch_shapes=[pltpu.VMEM((mlen, wkv.shape[1]), BF16)],
        compiler_params=_params(2),
        name=name,
    )(merged, x, mem, w_out, wq, wkv, wo, ln_cross.reshape(1, d), ln_mem.reshape(1, d), ln_ffn.reshape(1, d))


def _block_scores(q, kk):
    return lax.dot_general(q, kk, (((1,), (1,)), ((), ())), preferred_element_type=F32)


def _block_softmax_pv(s, vv, mask):
    scale = HEAD_DIM ** -0.5
    s = jnp.where(mask, s, -jnp.inf)
    m = jnp.max(s, axis=-1, keepdims=True)
    p = jnp.exp2((s - m) * (scale * LOG2_E))
    l = jnp.sum(p, axis=-1, keepdims=True)
    o = jnp.dot(p.astype(BF16), vv, preferred_element_type=F32) * (1.0 / l)
    return o, m * scale + jnp.log(l)


ATT_BLOCKS_IN_FLIGHT = 4


def _attn_kernel(q0, k0, v0, q1, k1, v1, q2, k2, v2, o_ref, st, og, ls):
    seq = q0.shape[0]
    blk = ATT_BLOCK
    for idx, ref in enumerate((q1, k1, v1, q2, k2, v2)):
        st[idx] = ref[...].astype(F32)

    qi = lax.broadcasted_iota(jnp.int32, (blk, 2 * blk), 0)
    kj = lax.broadcasted_iota(jnp.int32, (blk, 2 * blk), 1)
    dist = qi - kj + blk
    first_mask = (lax.broadcasted_iota(jnp.int32, (blk, blk), 0) >= lax.broadcasted_iota(jnp.int32, (blk, blk), 1))

    srcs = ((q0, k0, v0), (0, 1, 2), (3, 4, 5))

    def rows(dil, r, first_blk, count):
        start = first_blk * blk * dil + r
        if dil == 1:
            return pl.ds(start, count)
        return pl.ds(start, count, stride=dil)

    def load(g, which, rws):
        if g == 0:
            return srcs[0][which][rws, :]
        return st[srcs[g][which], rws, :].astype(BF16)

    per_group = []
    for g, (window, dil) in enumerate(ATT_GROUPS):
        band_mask = (dist >= 0) & (dist <= window // dil)
        group_tasks = []
        for r in range(dil):
            for n in range(seq // dil // blk):
                if n == 0:
                    group_tasks.append((g, rows(dil, r, 0, blk), rows(dil, r, 0, blk), first_mask))
                else:
                    group_tasks.append((g, rows(dil, r, n, blk), rows(dil, r, n - 1, 2 * blk), band_mask))
        per_group.append(group_tasks)
    tasks = [t for trio in zip(*per_group) for t in trio]

    def finish(batch, scores):
        for (g, q_rows, k_rows, mask), s in zip(batch, scores):
            o, lse = _block_softmax_pv(s, load(g, 2, k_rows), mask)
            og[g, q_rows, :] = o
            ls[g, q_rows, :] = jnp.broadcast_to(lse, (blk, HEAD_DIM))

    pending = None
    for t0 in range(0, len(tasks), ATT_BLOCKS_IN_FLIGHT):
        batch = tasks[t0:t0 + ATT_BLOCKS_IN_FLIGHT]
        scores = [_block_scores(load(g, 0, q_rows), load(g, 1, k_rows)) for g, q_rows, k_rows, _ in batch]
        if pending is not None:
            finish(*pending)
        pending = (batch, scores)
    finish(*pending)

    for n in range(seq // blk):
        sl = pl.ds(n * blk, blk)
        l0, l1, l2 = ls[0, sl, :], ls[1, sl, :], ls[2, sl, :]
        mx = jnp.maximum(jnp.maximum(l0, l1), l2)
        e0, e1, e2 = jnp.exp(l0 - mx), jnp.exp(l1 - mx), jnp.exp(l2 - mx)
        inv = 1.0 / (e0 + e1 + e2)
        out = (e0 * inv) * og[0, sl, :] + (e1 * inv) * og[1, sl, :] + (e2 * inv) * og[2, sl, :]
        o_ref[sl, :] = out.astype(o_ref.dtype)


def _dilated_attention(qkv, b, name):
    s = qkv.shape[1] // b
    hpg = ATT_HEADS_PER_GROUP
    n_cols = ATT_WIDTH // HEAD_DIM

    def spec(which, g):
        return pl.BlockSpec((None, s, HEAD_DIM), lambda bi, h: (which * n_cols + g * hpg + h, bi, 0))

    in_specs = [spec(which, g) for g in range(N_ATT_GROUPS) for which in range(3)]
    return pl.pallas_call(
        _attn_kernel,
        out_shape=jax.ShapeDtypeStruct((b, s, ATT_OUT_WIDTH), BF16),
        grid=(b, hpg),
        in_specs=in_specs,
        out_specs=pl.BlockSpec((None, s, HEAD_DIM), lambda bi, h: (bi, 0, h)),
        scratch_shapes=[
            pltpu.VMEM((6, s, HEAD_DIM), F32),
            pltpu.VMEM((3, s, HEAD_DIM), F32),
            pltpu.VMEM((3, s, HEAD_DIM), F32),
        ],
        compiler_params=_params(2),
        name=name,
    )(*([qkv] * 9))


def _hg_prep_kernel(layer, a_ref, wa_ref, wb_ref, lbp_ref, nw_ref, qa_ref, ka_ref, v_ref, gs_ref, rv_ref, wa_s, wb_s):
    j = pl.program_id(1)
    c = HG_CHUNK
    bm = a_ref.shape[0]
    rt = MXU_ROW_TILE
    heads_per_tile = wa_ref.shape[1] // HG_KEY
    n_tiles_per_kind = HG_WIDTH // wa_ref.shape[1]

    def cast_weights():
        wa_s[...] = wa_ref[...].astype(BF16)
        wb_s[...] = wb_ref[...].astype(BF16)

    def put(dst, rows, val):
        for hd in range(heads_per_tile):
            dst[hd, rows, :] = val[:, hd * HG_KEY:(hd + 1) * HG_KEY].astype(dst.dtype)

    def two_dots(r0):
        x = a_ref[pl.ds(r0, rt), :]
        return (jnp.dot(x, wa_s[...], preferred_element_type=F32), jnp.dot(x, wb_s[...], preferred_element_type=F32))

    @pl.when(j < n_tiles_per_kind)
    def _():
        p = lbp_ref[...]
        e = jnp.exp(p - jnp.max(p, axis=0, keepdims=True))
        sm = e / jnp.sum(e, axis=0, keepdims=True)
        lb = jnp.sum(sm[: layer + 1], axis=0, keepdims=True)
        f_mid = 0.5 * (1.0 + lb)
        f_half = 0.5 * (1.0 - lb)
        col_in_part = jnp.bitwise_and(lax.broadcasted_iota(jnp.int32, (c, 3 * c), 1), c - 1)
        tri3 = jnp.where(lax.broadcasted_iota(jnp.int32, (c, 3 * c), 0) >= col_in_part, 1.0, 0.0).astype(BF16)
        cast_weights()

        def finish(r0, q, f_pre):
            f = f_mid + f_half * jnp.tanh(0.5 * f_pre)
            lf = jnp.log(f)
            hi = lf.astype(BF16)
            r1 = lf - hi.astype(F32)
            mid = r1.astype(BF16)
            lo = (r1 - mid.astype(F32)).astype(BF16)
            for k0 in range(0, rt, c):
                sl = slice(k0, k0 + c)
                bcum = jnp.dot(tri3, jnp.concatenate([hi[sl], mid[sl], lo[sl]], axis=0), preferred_element_type=F32)
                b_mid = bcum[c // 2:c // 2 + 1, :]
                b_last = bcum[c - 1:c, :]
                rows = pl.ds(r0 + k0, c)
                put(qa_ref, rows, _silu(q[sl]) * jnp.exp(bcum - b_mid))
                put(ka_ref, rows, (1.0 - f[sl]) * jnp.exp(b_mid - bcum))
                n = (r0 + k0) // c
                rv_ref[0, n:n + 1, :] = jnp.exp(b_mid)
                rv_ref[1, n:n + 1, :] = jnp.exp(b_last - b_mid)
                rv_ref[2, n:n + 1, :] = jnp.exp(b_last)

        pending = None
        for r0 in range(0, bm, rt):
            dots = two_dots(r0)
            if pending is not None:
                finish(*pending)
            pending = (r0,) + dots
        finish(*pending)

    @pl.when(j >= n_tiles_per_kind)
    def _():
        cast_weights()
        for r0 in range(0, bm, rt):
            i_proj, g_proj = two_dots(r0)
            rows = pl.ds(r0, rt)
            put(v_ref, rows, i_proj)
            put(gs_ref, rows, _silu(g_proj) * nw_ref[...])


def _hg_prep(h, w_in, col0, lower_bounds_param, norm_w, layer, name, bm, bn):
    t, k = h.shape
    base = col0 // bn
    nk = HG_WIDTH // bn
    hpt = bn // HG_KEY
    n_layers = lower_bounds_param.shape[0]
    head_shape = jax.ShapeDtypeStruct((HG_HEADS, t, HG_KEY), BF16)
    first = lambda j: jnp.minimum(j, nk - 1)
    second = lambda j: jnp.maximum(j - nk, 0)
    head_block = lambda sel: pl.BlockSpec((hpt, bm, HG_KEY), lambda i, j: (sel(j), i, 0))
    return pl.pallas_call(
        functools.partial(_hg_prep_kernel, layer),
        out_shape=[head_shape, head_shape, head_shape, head_shape,
                   jax.ShapeDtypeStruct((3, t // HG_CHUNK, HG_WIDTH), F32)],
        grid=(t // bm, 2 * nk),
        in_specs=[
            pl.BlockSpec((bm, k), lambda i, j: (i, 0)),
            pl.BlockSpec((k, bn), lambda i, j: (0, base + j + nk * (j // nk))),
            pl.BlockSpec((k, bn), lambda i, j: (0, base + nk + j + nk * (j // nk))),
            pl.BlockSpec((n_layers, bn), lambda i, j: (0, first(j))),
            pl.BlockSpec((1, bn), lambda i, j: (0, 0)),
        ],
        out_specs=[head_block(first), head_block(first), head_block(second), head_block(second),
                   pl.BlockSpec((3, bm // HG_CHUNK, bn), lambda i, j: (0, i, first(j)))],
        scratch_shapes=[pltpu.VMEM((k, bn), BF16), pltpu.VMEM((k, bn), BF16)],
        compiler_params=_params(2),
        name=name,
    )(h, w_in, w_in, lower_bounds_param, jnp.tile(norm_w, hpt).reshape(1, bn))


def _hgrn_scan_kernel(qa_ref, ka_ref, v_ref, gs_ref, rv_ref, o_ref, state):
    ts = qa_ref.shape[1]
    c = HG_CHUNK

    @pl.when(pl.program_id(1) == 0)
    def _():
        state[...] = jnp.zeros_like(state)

    causal = lax.broadcasted_iota(jnp.int32, (c, c), 0) >= lax.broadcasted_iota(jnp.int32, (c, c), 1)
    heads = [pl.ds(h * HG_KEY, HG_KEY) for h in range(HG_HEADS)]
    for n in range(ts // c):
        rows = pl.ds(n * c, c)
        both = []
        for h, cols in enumerate(heads):
            rhs = jnp.concatenate([(state[h] * rv_ref[0, n:n + 1, cols]).astype(BF16), ka_ref[h, rows, :]], axis=0)
            both.append(lax.dot_general(qa_ref[h, rows, :], rhs, (((1,), (1,)), ((), ())), preferred_element_type=F32))
        kv_t = [lax.dot_general(v_ref[h, rows, :], ka_ref[h, rows, :], (((0,), (0,)), ((), ())), preferred_element_type=F32)
                for h in range(HG_HEADS)]
        for h, cols in enumerate(heads):
            att = jnp.where(causal, both[h][:, HG_VAL:], 0.0).astype(BF16)
            o = both[h][:, :HG_VAL] + jnp.dot(att, v_ref[h, rows, :], preferred_element_type=F32)
            o = o * lax.rsqrt(jnp.mean(o * o, axis=-1, keepdims=True) + RMS_EPS) * gs_ref[h, rows, :].astype(F32)
            o_ref[rows, cols] = o.astype(o_ref.dtype)
        for h, cols in enumerate(heads):
            state[h] = rv_ref[2, n:n + 1, cols] * state[h] + kv_t[h] * rv_ref[1, n:n + 1, cols]


def _hgrn_scan(qa, ka, v, gs, rv, b, name, ts):
    s = qa.shape[1] // b
    nt = s // ts
    head_block = pl.BlockSpec((HG_HEADS, ts, HG_KEY), lambda bi, t: (0, bi * nt + t, 0))
    return pl.pallas_call(
        _hgrn_scan_kernel,
        out_shape=jax.ShapeDtypeStruct((b, s, HG_WIDTH), BF16),
        grid=(b, nt),
        in_specs=[head_block, head_block, head_block, head_block,
                  pl.BlockSpec((3, ts // HG_CHUNK, HG_WIDTH), lambda bi, t: (0, bi * nt + t, 0))],
        out_specs=pl.BlockSpec((None, ts, HG_WIDTH), lambda bi, t: (bi, t, 0)),
        scratch_shapes=[pltpu.VMEM((HG_HEADS, HG_VAL, HG_KEY), F32)],
        compiler_params=_params(2),
        name=name,
    )(qa, ka, v, gs, rv)


def kernel(x, mem, ln_mix_w, w_in, hg_norm_w, hg_lower_bounds, w_branch_a, w_branch_b, w_out, ln_cross_w, ln_mem_w, wq_cross, wkv_cross, wo_cross, ln_ffn_w, w1, w3, w2, ln_final_w):
    b, s, d = x.shape
    t = b * s
    depth = w_in.shape[0]
    qkv_w = 3 * ATT_WIDTH
    hg_w = 4 * HG_WIDTH
    assert w_in.shape[2] == qkv_w + hg_w + 2 * d, w_in.shape
    assert s % HGRN_SCAN_ROWS == 0 and s % MID_ROWS == 0 and s % (ATT_BLOCK * ATT_GROUPS[-1][1]) == 0, s
    for rows in (RMS_ROWS, PROJ_QKV_TILE[0], HG_PREP_TILE[0], GATE_MERGE_TILE[0], FFN_UP_TILE[0], FFN_DOWN_TILE[0]):
        assert t % rows == 0, (t, rows)

    xf = x.reshape(t, d)
    for l in range(depth):
        h = _rmsnorm(xf, ln_mix_w[l], BF16, "rms_mix", bm=RMS_ROWS)
        qkv = _matmul(h, w_in[l], 0, qkv_w, BF16, "proj_qkv", *PROJ_QKV_TILE, head_major=True)
        qa, ka, v, gate, rv = _hg_prep(h, w_in[l], qkv_w, hg_lower_bounds, hg_norm_w[l], l, "proj_hg_prep", *HG_PREP_TILE)
        o_att = _dilated_attention(qkv, b, "dilated_attn")
        o_hg = _hgrn_scan(qa, ka, v, gate, rv, b, "hgrn2_scan", ts=HGRN_SCAN_ROWS)
        merged, (w_out_b, wq_b, wkv_b, wo_b) = _merge(
            h, o_att.reshape(t, ATT_OUT_WIDTH), o_hg.reshape(t, HG_WIDTH), w_in[l], qkv_w + hg_w,
            w_branch_a[l], w_branch_b[l], "gate_merge", *GATE_MERGE_TILE,
            riders=(w_out[l], wq_cross[l], wkv_cross[l], wo_cross[l]))
        x2, hf = _mid_block(merged.reshape(b, s, d), xf.reshape(b, s, d), mem, w_out_b, wq_b, wkv_b, wo_b,
                            ln_cross_w[l], ln_mem_w[l], ln_ffn_w[l], "out_proj_cross", bm=MID_ROWS)
        xf, hf = x2.reshape(t, d), hf.reshape(t, d)
        act, (w2_b,) = _swiglu(hf, w1[l], w3[l], "ffn_up", *FFN_UP_TILE, riders=(w2[l],))
        if l + 1 < depth:
            xf = _matmul_residual(act, w2_b, xf, "ffn_down", *FFN_DOWN_TILE)
        else:
            out = _matmul_residual_norm(act, w2_b, xf, ln_final_w, F32, False, "ffn_down_final", bm=FFN_DOWN_TILE[0],
                                        single_buffer_w=True)
    return out.reshape(b, s, d)
```

```python
import functools

import jax
import jax.numpy as jnp
from jax import lax
from jax.experimental import pallas as pl
from jax.experimental.pallas import tpu as pltpu

F32 = jnp.float32
BF16 = jnp.bfloat16

HEAD_DIM = 128
ATT_GROUPS = ((128, 1), (512, 4), (2048, 16))
ATT_HEADS_PER_GROUP = 4
N_ATT_GROUPS = len(ATT_GROUPS)
ATT_WIDTH = N_ATT_GROUPS * ATT_HEADS_PER_GROUP * HEAD_DIM
ATT_OUT_WIDTH = ATT_HEADS_PER_GROUP * HEAD_DIM
ATT_BLOCK = 128
HG_HEADS = 8
HG_KEY = 128
HG_VAL = 128
HG_WIDTH = HG_HEADS * HG_KEY
HG_CHUNK = 64
CROSS_HEADS = 4
CROSS_WIDTH = CROSS_HEADS * HEAD_DIM
RMS_EPS = 1e-6
LOG2_E = 1.4426950408889634

V7X_VMEM_LIMIT_BYTES = 56 * 1024 * 1024
MXU_ROW_TILE = 256
MXU_COL_TILE = 512
BF16_SUBLANES = 16

RMS_ROWS = 1024
PROJ_QKV_TILE = (2048, 768)
HG_PREP_TILE = (1024, 512)
GATE_MERGE_TILE = (1024, 512)
MID_ROWS = 512
FFN_UP_TILE = (2048, 512)
FFN_DOWN_TILE = (512, 512)


def _sigmoid(x):
    return 0.5 * jnp.tanh(0.5 * x) + 0.5


def _silu(x):
    hx = 0.5 * x
    return hx * jnp.tanh(hx) + hx


def _params(n_axes):
    return pltpu.CompilerParams(
        dimension_semantics=("arbitrary",) * n_axes,
        vmem_limit_bytes=V7X_VMEM_LIMIT_BYTES,
    )


def _rms_kernel(x_ref, w_ref, o_ref):
    x = x_ref[...]
    ms = jnp.mean(x * x, axis=-1, keepdims=True)
    o_ref[...] = (x * lax.rsqrt(ms + RMS_EPS) * w_ref[...]).astype(o_ref.dtype)


def _rmsnorm(x, w, out_dtype, name, bm=512):
    m, d = x.shape
    return pl.pallas_call(
        _rms_kernel,
        out_shape=jax.ShapeDtypeStruct((m, d), out_dtype),
        grid=(m // bm,),
        in_specs=[pl.BlockSpec((bm, d), lambda i: (i, 0)), pl.BlockSpec((1, d), lambda i: (0, 0))],
        out_specs=pl.BlockSpec((bm, d), lambda i: (i, 0)),
        compiler_params=_params(1),
        name=name,
    )(x, w.reshape(1, d))


def _row_tiles(n_rows):
    tile = min(MXU_ROW_TILE, n_rows)
    return [pl.ds(r * tile, tile) for r in range(n_rows // tile)]


def _col_tiles(n_cols):
    tile = min(MXU_COL_TILE, n_cols)
    return [pl.ds(c * tile, tile) for c in range(n_cols // tile)]


def _mm_kernel(a_ref, w_ref, o_ref, w_s):
    w_s[...] = w_ref[...].astype(BF16)
    for rows in _row_tiles(a_ref.shape[0]):
        res = jnp.dot(a_ref[rows, :], w_s[...], preferred_element_type=F32).astype(o_ref.dtype)
        if len(o_ref.shape) == 2:
            o_ref[rows, :] = res
        else:
            for hd in range(o_ref.shape[0]):
                o_ref[hd, rows, :] = res[:, hd * HEAD_DIM:(hd + 1) * HEAD_DIM]


def _matmul(a, w, col0, n, out_dtype, name, bm, bn, head_major=False):
    m, k = a.shape
    cb = col0 // bn
    if head_major:
        out_shape = jax.ShapeDtypeStruct((n // HEAD_DIM, m, HEAD_DIM), out_dtype)
        out_spec = pl.BlockSpec((bn // HEAD_DIM, bm, HEAD_DIM), lambda i, j: (j, i, 0))
    else:
        out_shape = jax.ShapeDtypeStruct((m, n), out_dtype)
        out_spec = pl.BlockSpec((bm, bn), lambda i, j: (i, j))
    return pl.pallas_call(
        _mm_kernel,
        out_shape=out_shape,
        grid=(m // bm, n // bn),
        in_specs=[
            pl.BlockSpec((bm, k), lambda i, j: (i, 0)),
            pl.BlockSpec((k, bn), lambda i, j: (0, j + cb)),
        ],
        out_specs=out_spec,
        scratch_shapes=[pltpu.VMEM((k, bn), BF16)],
        compiler_params=_params(2),
        name=name,
    )(a, w)


def _mm_res_kernel(a_ref, w_ref, r_ref, o_ref):
    o_ref[...] = r_ref[...] + jnp.dot(a_ref[...], w_ref[...], preferred_element_type=F32)


def _matmul_residual(a, w, res, name, bm, bn):
    m, k = a.shape
    n = w.shape[1]
    return pl.pallas_call(
        _mm_res_kernel,
        out_shape=jax.ShapeDtypeStruct((m, n), F32),
        grid=(m // bm, n // bn),
        in_specs=[
            pl.BlockSpec((bm, k), lambda i, j: (i, 0)),
            pl.BlockSpec((k, bn), lambda i, j: (0, j)),
            pl.BlockSpec((bm, bn), lambda i, j: (i, j)),
        ],
        out_specs=pl.BlockSpec((bm, bn), lambda i, j: (i, j)),
        compiler_params=_params(2),
        name=name,
    )(a, w, res)


def _mm_res_norm_kernel(emit_sum, a_ref, w_ref, r_ref, lnw_ref, *out_refs):
    n = w_ref.shape[1]
    for rows in _row_tiles(a_ref.shape[0]):
        a = a_ref[rows, :]
        ys = [r_ref[rows, cols] + jnp.dot(a, w_ref[:, cols], preferred_element_type=F32) for cols in _col_tiles(n)]
        ssq = sum(jnp.sum(y * y, axis=-1, keepdims=True) for y in ys)
        scale = lax.rsqrt(ssq * (1.0 / n) + RMS_EPS)
        for cols, y in zip(_col_tiles(n), ys):
            if emit_sum:
                out_refs[0][rows, cols] = y
            out_refs[-1][rows, cols] = (y * scale * lnw_ref[:, cols]).astype(out_refs[-1].dtype)


def _matmul_residual_norm(a, w, res, lnw, norm_dtype, emit_sum, name, bm, single_buffer_w=False):
    m, k = a.shape
    n = w.shape[1]
    row = lambda i: (i, 0)
    w_spec = (pl.BlockSpec((k, n), lambda i: (0, 0), pipeline_mode=pl.Buffered(1)) if single_buffer_w
              else pl.BlockSpec((k, n), lambda i: (0, 0)))
    norm_shape = jax.ShapeDtypeStruct((m, n), norm_dtype)
    out_shape = (jax.ShapeDtypeStruct((m, n), F32), norm_shape) if emit_sum else norm_shape
    out_specs = (pl.BlockSpec((bm, n), row), pl.BlockSpec((bm, n), row)) if emit_sum else pl.BlockSpec((bm, n), row)
    return pl.pallas_call(
        functools.partial(_mm_res_norm_kernel, emit_sum),
        out_shape=out_shape,
        grid=(m // bm,),
        in_specs=[pl.BlockSpec((bm, k), row), w_spec, pl.BlockSpec((bm, n), row), pl.BlockSpec((1, n), lambda i: (0, 0))],
        out_specs=out_specs,
        compiler_params=_params(1),
        name=name,
    )(a, w, res, lnw.reshape(1, n))


def _convert_riders(rider_in, rider_out):
    for src, dst in zip(rider_in, rider_out):
        dst[...] = src[...].astype(dst.dtype)


def _rider_specs(riders, n_steps, step_of):
    specs, shapes = [], []
    for w in riders:
        rows, cols = w.shape
        assert rows % n_steps == 0 and (rows // n_steps) % BF16_SUBLANES == 0, (w.shape, n_steps)
        specs.append(pl.BlockSpec((rows // n_steps, cols), lambda i, j: (step_of(i, j), 0)))
        shapes.append(jax.ShapeDtypeStruct((rows, cols), BF16))
    return specs, shapes


def _swiglu_kernel(n_riders, a_ref, w1_ref, w3_ref, *rest):
    rider_in, (o_ref, *rider_out), (w1_s, w3_s) = rest[:n_riders], rest[n_riders:2 * n_riders + 1], rest[2 * n_riders + 1:]
    _convert_riders(rider_in, rider_out)
    w1_s[...] = w1_ref[...].astype(BF16)
    w3_s[...] = w3_ref[...].astype(BF16)
    for rows in _row_tiles(a_ref.shape[0]):
        a = a_ref[rows, :]
        u = jnp.dot(a, w1_s[...], preferred_element_type=F32)
        g = jnp.dot(a, w3_s[...], preferred_element_type=F32)
        o_ref[rows, :] = (_silu(u) * g).astype(o_ref.dtype)


def _swiglu(a, w1, w3, name, bm, bn, riders=()):
    m, k = a.shape
    n = w1.shape[1]
    nj = n // bn
    rider_specs, rider_shapes = _rider_specs(riders, (m // bm) * nj, lambda i, j: i * nj + j)
    outs = pl.pallas_call(
        functools.partial(_swiglu_kernel, len(riders)),
        scratch_shapes=[pltpu.VMEM((k, bn), BF16), pltpu.VMEM((k, bn), BF16)],
        out_shape=[jax.ShapeDtypeStruct((m, n), BF16)] + rider_shapes,
        grid=(m // bm, nj),
        in_specs=[
            pl.BlockSpec((bm, k), lambda i, j: (i, 0)),
            pl.BlockSpec((k, bn), lambda i, j: (0, j)),
            pl.BlockSpec((k, bn), lambda i, j: (0, j)),
        ] + rider_specs,
        out_specs=[pl.BlockSpec((bm, bn), lambda i, j: (i, j))] + rider_specs,
        compiler_params=_params(2),
        name=name,
    )(a, w1, w3, *riders)
    return outs[0], tuple(outs[1:])


def _merge_kernel(n_riders, h_ref, oa_ref, oh_ref, wga_ref, wgb_ref, wa_ref, wb_ref, *rest):
    rider_in, (o_ref, *rider_out) = rest[:n_riders], rest[n_riders:2 * n_riders + 1]
    wga_s, wgb_s, wa_s, wb_s = rest[2 * n_riders + 1:]
    _convert_riders(rider_in, rider_out)

    @pl.when(pl.program_id(1) == 0)
    def _():
        for src, dst in ((wga_ref, wga_s), (wgb_ref, wgb_s), (wa_ref, wa_s), (wb_ref, wb_s)):
            dst[...] = src[...].astype(BF16)

    for rows in _row_tiles(h_ref.shape[0]):
        h = h_ref[rows, :]
        ga = jnp.dot(h, wga_s[...], preferred_element_type=F32)
        gb = jnp.dot(h, wgb_s[...], preferred_element_type=F32)
        pa = jnp.dot(oa_ref[rows, :], wa_s[...], preferred_element_type=F32)
        pb = jnp.dot(oh_ref[rows, :], wb_s[...], preferred_element_type=F32)
        o_ref[rows, :] = (_sigmoid(ga) * pa + _sigmoid(gb) * pb).astype(o_ref.dtype)


def _merge(h, o_att, o_hg, w_in, gate_col0, wa, wb, name, bm, bn, riders=()):
    m, k = h.shape
    d = wa.shape[1]
    ga0 = gate_col0 // bn
    gb0 = (gate_col0 + d) // bn
    ni = m // bm
    rider_specs, rider_shapes = _rider_specs(riders, (d // bn) * ni, lambda j, i: j * ni + i)
    outs = pl.pallas_call(
        functools.partial(_merge_kernel, len(riders)),
        out_shape=[jax.ShapeDtypeStruct((m, d), BF16)] + rider_shapes,
        grid=(d // bn, ni),
        in_specs=[
            pl.BlockSpec((bm, k), lambda j, i: (i, 0)),
            pl.BlockSpec((bm, o_att.shape[1]), lambda j, i: (i, 0)),
            pl.BlockSpec((bm, o_hg.shape[1]), lambda j, i: (i, 0)),
            pl.BlockSpec((k, bn), lambda j, i: (0, j + ga0)),
            pl.BlockSpec((k, bn), lambda j, i: (0, j + gb0)),
            pl.BlockSpec((wa.shape[0], bn), lambda j, i: (0, j)),
            pl.BlockSpec((wb.shape[0], bn), lambda j, i: (0, j)),
        ] + rider_specs,
        out_specs=[pl.BlockSpec((bm, bn), lambda j, i: (i, j))] + rider_specs,
        scratch_shapes=[pltpu.VMEM((k, bn), BF16), pltpu.VMEM((k, bn), BF16),
                        pltpu.VMEM((wa.shape[0], bn), BF16), pltpu.VMEM((wb.shape[0], bn), BF16)],
        compiler_params=_params(2),
        name=name,
    )(h, o_att, o_hg, w_in, w_in, wa, wb, *riders)
    return outs[0], tuple(outs[1:])


def _mid_kernel(mg_ref, x_ref, mem_ref, wout_ref, wq_ref, wkv_ref, wo_ref, lnc_ref, lnm_ref, lnf_ref, x2_ref, hf_ref, kv_ref):
    d = x_ref.shape[1]

    @pl.when(pl.program_id(1) == 0)
    def _():
        mem = mem_ref[...]
        mn = (mem * lax.rsqrt(jnp.mean(mem * mem, axis=-1, keepdims=True) + RMS_EPS) * lnm_ref[...]).astype(BF16)
        kv_ref[...] = jnp.dot(mn, wkv_ref[...], preferred_element_type=F32).astype(kv_ref.dtype)

    row_tiles = _row_tiles(x_ref.shape[0])
    col_tiles = _col_tiles(d)

    def residual_norm(rows, lhs, w_ref, base_ref, ln_ref):
        ys = [base_ref[rows, cols] + jnp.dot(lhs, w_ref[:, cols], preferred_element_type=F32) for cols in col_tiles]
        ssq = sum(jnp.sum(y * y, axis=-1, keepdims=True) for y in ys)
        scale = lax.rsqrt(ssq * (1.0 / d) + RMS_EPS)
        for cols, y in zip(col_tiles, ys):
            x2_ref[rows, cols] = y
            hf_ref[rows, cols] = (y * scale * ln_ref[:, cols]).astype(hf_ref.dtype)

    for rows in row_tiles:
        residual_norm(rows, mg_ref[rows, :], wout_ref, x_ref, lnc_ref)
    qcs = [jnp.dot(hf_ref[rows, :], wq_ref[...], preferred_element_type=F32).astype(BF16) for rows in row_tiles]
    scores = [[lax.dot_general(qc[:, h * HEAD_DIM:(h + 1) * HEAD_DIM], kv_ref[:, h * HEAD_DIM:(h + 1) * HEAD_DIM],
                               (((1,), (1,)), ((), ())), preferred_element_type=F32) * (HEAD_DIM ** -0.5)
               for h in range(CROSS_HEADS)] for qc in qcs]
    ocs = []
    for per_head in scores:
        outs = []
        for h, s in enumerate(per_head):
            m = jnp.max(s, axis=-1, keepdims=True)
            p = jnp.exp(s - m)
            pn = (p * (1.0 / jnp.sum(p, axis=-1, keepdims=True))).astype(BF16)
            v = kv_ref[:, CROSS_WIDTH + h * HEAD_DIM:CROSS_WIDTH + (h + 1) * HEAD_DIM]
            outs.append(jnp.dot(pn, v, preferred_element_type=F32).astype(BF16))
        ocs.append(jnp.concatenate(outs, axis=-1))
    for rows, oc in zip(row_tiles, ocs):
        residual_norm(rows, oc, wo_ref, x2_ref, lnf_ref)


def _mid_block(merged, x, mem, w_out, wq, wkv, wo, ln_cross, ln_mem, ln_ffn, name, bm):
    b, s, d = x.shape
    mlen = mem.shape[1]
    tile = pl.BlockSpec((None, bm, d), lambda bi, i: (bi, i, 0))

    def resident(shape):
        return pl.BlockSpec(shape, lambda bi, i: (0,) * len(shape), pipeline_mode=pl.Buffered(1))

    return pl.pallas_call(
        _mid_kernel,
        out_shape=(jax.ShapeDtypeStruct((b, s, d), F32), jax.ShapeDtypeStruct((b, s, d), BF16)),
        grid=(b, s // bm),
        in_specs=[
            tile, tile,
            pl.BlockSpec((None, mlen, d), lambda bi, i: (bi, 0, 0)),
            resident(w_out.shape), resident(wq.shape), resident(wkv.shape), resident(wo.shape),
            resident((1, d)), resident((1, d)), resident((1, d)),
        ],
        out_specs=(tile, tile),
        scratch_shapes=[pltpu.VMEM((mlen, wkv.shape[1]), BF16)],
        compiler_params=_params(2),
        name=name,
    )(merged, x, mem, w_out, wq, wkv, wo, ln_cross.reshape(1, d), ln_mem.reshape(1, d), ln_ffn.reshape(1, d))


def _block_scores(q, kk):
    return lax.dot_general(q, kk, (((1,), (1,)), ((), ())), preferred_element_type=F32)


def _block_softmax_pv(s, vv, mask):
    scale = HEAD_DIM ** -0.5
    s = jnp.where(mask, s, -jnp.inf)
    m = jnp.max(s, axis=-1, keepdims=True)
    p = jnp.exp2((s - m) * (scale * LOG2_E))
    l = jnp.sum(p, axis=-1, keepdims=True)
    o = jnp.dot(p.astype(BF16), vv, preferred_element_type=F32) * (1.0 / l)
    return o, m * scale + jnp.log(l)


ATT_BLOCKS_IN_FLIGHT = 8


def _attn_program(q0, k0, v0, q1, k1, v1, q2, k2, v2, o_ref, st, og, ls):
    seq = q0.shape[0]
    blk = ATT_BLOCK

    def stage_inputs():
        for idx, ref in enumerate((q1, k1, v1, q2, k2, v2)):
            st[idx] = ref[...].astype(F32)

    qi = lax.broadcasted_iota(jnp.int32, (blk, 2 * blk), 0)
    kj = lax.broadcasted_iota(jnp.int32, (blk, 2 * blk), 1)
    dist = qi - kj + blk
    first_mask = (lax.broadcasted_iota(jnp.int32, (blk, blk), 0) >= lax.broadcasted_iota(jnp.int32, (blk, blk), 1))

    srcs = ((q0, k0, v0), (0, 1, 2), (3, 4, 5))

    def rows(dil, r, first_blk, count):
        start = first_blk * blk * dil + r
        if dil == 1:
            return pl.ds(start, count)
        return pl.ds(start, count, stride=dil)

    def load(g, which, rws):
        if g == 0:
            return srcs[0][which][rws, :]
        return st[srcs[g][which], rws, :].astype(BF16)

    per_group = []
    for g, (window, dil) in enumerate(ATT_GROUPS):
        band_mask = (dist >= 0) & (dist <= window // dil)
        group_tasks = []
        for r in range(dil):
            for n in range(seq // dil // blk):
                if n == 0:
                    group_tasks.append((g, rows(dil, r, 0, blk), rows(dil, r, 0, blk), first_mask))
                else:
                    group_tasks.append((g, rows(dil, r, n, blk), rows(dil, r, n - 1, 2 * blk), band_mask))
        per_group.append(group_tasks)
    tasks = [t for trio in zip(*per_group) for t in trio]

    def finish(batch, scores):
        for (g, q_rows, k_rows, mask), s in zip(batch, scores):
            o, lse = _block_softmax_pv(s, load(g, 2, k_rows), mask)
            og[g, q_rows, :] = o
            ls[g, q_rows, :] = jnp.broadcast_to(lse, (blk, HEAD_DIM))

    pending = []

    def pipeline_step(batch):
        def run():
            scores = [_block_scores(load(g, 0, q_rows), load(g, 1, k_rows)) for g, q_rows, k_rows, _ in batch]
            if pending:
                finish(*pending.pop())
            pending.append((batch, scores))
        return run

    def merge_groups():
        finish(*pending.pop())
        for n in range(seq // blk):
            sl = pl.ds(n * blk, blk)
            l0, l1, l2 = ls[0, sl, :], ls[1, sl, :], ls[2, sl, :]
            mx = jnp.maximum(jnp.maximum(l0, l1), l2)
            e0, e1, e2 = jnp.exp(l0 - mx), jnp.exp(l1 - mx), jnp.exp(l2 - mx)
            inv = 1.0 / (e0 + e1 + e2)
            out = (e0 * inv) * og[0, sl, :] + (e1 * inv) * og[1, sl, :] + (e2 * inv) * og[2, sl, :]
            o_ref[sl, :] = out.astype(o_ref.dtype)

    steps = [pipeline_step(tasks[t0:t0 + ATT_BLOCKS_IN_FLIGHT]) for t0 in range(0, len(tasks), ATT_BLOCKS_IN_FLIGHT)]
    return [stage_inputs] + steps + [merge_groups]


def _hg_prep_kernel(layer, a_ref, wa_ref, wb_ref, lbp_ref, nw_ref, qa_ref, ka_ref, v_ref, gs_ref, rv_ref, wa_s, wb_s):
    j = pl.program_id(1)
    c = HG_CHUNK
    bm = a_ref.shape[0]
    rt = MXU_ROW_TILE
    heads_per_tile = wa_ref.shape[1] // HG_KEY
    n_tiles_per_kind = HG_WIDTH // wa_ref.shape[1]

    def cast_weights():
        wa_s[...] = wa_ref[...].astype(BF16)
        wb_s[...] = wb_ref[...].astype(BF16)

    def put(dst, rows, val):
        for hd in range(heads_per_tile):
            dst[hd, rows, :] = val[:, hd * HG_KEY:(hd + 1) * HG_KEY].astype(dst.dtype)

    def two_dots(r0):
        x = a_ref[pl.ds(r0, rt), :]
        return (jnp.dot(x, wa_s[...], preferred_element_type=F32), jnp.dot(x, wb_s[...], preferred_element_type=F32))

    @pl.when(j < n_tiles_per_kind)
    def _():
        p = lbp_ref[...]
        e = jnp.exp(p - jnp.max(p, axis=0, keepdims=True))
        sm = e / jnp.sum(e, axis=0, keepdims=True)
        lb = jnp.sum(sm[: layer + 1], axis=0, keepdims=True)
        f_mid = 0.5 * (1.0 + lb)
        f_half = 0.5 * (1.0 - lb)
        col_in_part = jnp.bitwise_and(lax.broadcasted_iota(jnp.int32, (c, 3 * c), 1), c - 1)
        tri3 = jnp.where(lax.broadcasted_iota(jnp.int32, (c, 3 * c), 0) >= col_in_part, 1.0, 0.0).astype(BF16)
        cast_weights()

        def finish(r0, q, f_pre):
            f = f_mid + f_half * jnp.tanh(0.5 * f_pre)
            lf = jnp.log(f)
            hi = lf.astype(BF16)
            r1 = lf - hi.astype(F32)
            mid = r1.astype(BF16)
            lo = (r1 - mid.astype(F32)).astype(BF16)
            for k0 in range(0, rt, c):
                sl = slice(k0, k0 + c)
                bcum = jnp.dot(tri3, jnp.concatenate([hi[sl], mid[sl], lo[sl]], axis=0), preferred_element_type=F32)
                b_mid = bcum[c // 2:c // 2 + 1, :]
                b_last = bcum[c - 1:c, :]
                rows = pl.ds(r0 + k0, c)
                put(qa_ref, rows, _silu(q[sl]) * jnp.exp(bcum - b_mid))
                put(ka_ref, rows, (1.0 - f[sl]) * jnp.exp(b_mid - bcum))
                n = (r0 + k0) // c
                rv_ref[0, n:n + 1, :] = jnp.exp(b_mid)
                rv_ref[1, n:n + 1, :] = jnp.exp(b_last - b_mid)
                rv_ref[2, n:n + 1, :] = jnp.exp(b_last)

        pending = None
        for r0 in range(0, bm, rt):
            dots = two_dots(r0)
            if pending is not None:
                finish(*pending)
            pending = (r0,) + dots
        finish(*pending)

    @pl.when(j >= n_tiles_per_kind)
    def _():
        cast_weights()
        for r0 in range(0, bm, rt):
            i_proj, g_proj = two_dots(r0)
            rows = pl.ds(r0, rt)
            put(v_ref, rows, i_proj)
            put(gs_ref, rows, _silu(g_proj) * nw_ref[...])


def _hg_prep(h, w_in, col0, lower_bounds_param, norm_w, layer, name, bm, bn):
    t, k = h.shape
    base = col0 // bn
    nk = HG_WIDTH // bn
    hpt = bn // HG_KEY
    n_layers = lower_bounds_param.shape[0]
    head_shape = jax.ShapeDtypeStruct((HG_HEADS, t, HG_KEY), BF16)
    first = lambda j: jnp.minimum(j, nk - 1)
    second = lambda j: jnp.maximum(j - nk, 0)
    head_block = lambda sel: pl.BlockSpec((hpt, bm, HG_KEY), lambda i, j: (sel(j), i, 0))
    return pl.pallas_call(
        functools.partial(_hg_prep_kernel, layer),
        out_shape=[head_shape, head_shape, head_shape, head_shape,
                   jax.ShapeDtypeStruct((3, t // HG_CHUNK, HG_WIDTH), F32)],
        grid=(t // bm, 2 * nk),
        in_specs=[
            pl.BlockSpec((bm, k), lambda i, j: (i, 0)),
            pl.BlockSpec((k, bn), lambda i, j: (0, base + j + nk * (j // nk))),
            pl.BlockSpec((k, bn), lambda i, j: (0, base + nk + j + nk * (j // nk))),
            pl.BlockSpec((n_layers, bn), lambda i, j: (0, first(j))),
            pl.BlockSpec((1, bn), lambda i, j: (0, 0)),
        ],
        out_specs=[head_block(first), head_block(first), head_block(second), head_block(second),
                   pl.BlockSpec((3, bm // HG_CHUNK, bn), lambda i, j: (0, i, first(j)))],
        scratch_shapes=[pltpu.VMEM((k, bn), BF16), pltpu.VMEM((k, bn), BF16)],
        compiler_params=_params(2),
        name=name,
    )(h, w_in, w_in, lower_bounds_param, jnp.tile(norm_w, hpt).reshape(1, bn))


def _hgrn_scan_program(qa_ref, ka_ref, v_ref, gs_ref, rv_ref, o_ref, state):
    ts = qa_ref.shape[1]
    c = HG_CHUNK
    causal = lax.broadcasted_iota(jnp.int32, (c, c), 0) >= lax.broadcasted_iota(jnp.int32, (c, c), 1)
    heads = [pl.ds(h * HG_KEY, HG_KEY) for h in range(HG_HEADS)]

    def chunk(n):
        def run():
            rows = pl.ds(n * c, c)
            both = []
            for h, cols in enumerate(heads):
                rhs = jnp.concatenate([(state[h] * rv_ref[0, n:n + 1, cols]).astype(BF16), ka_ref[h, rows, :]], axis=0)
                both.append(lax.dot_general(qa_ref[h, rows, :], rhs, (((1,), (1,)), ((), ())), preferred_element_type=F32))
            kv_t = [lax.dot_general(v_ref[h, rows, :], ka_ref[h, rows, :], (((0,), (0,)), ((), ())),
                                    preferred_element_type=F32) for h in range(HG_HEADS)]
            for h, cols in enumerate(heads):
                att = jnp.where(causal, both[h][:, HG_VAL:], 0.0).astype(BF16)
                o = both[h][:, :HG_VAL] + jnp.dot(att, v_ref[h, rows, :], preferred_element_type=F32)
                o = o * lax.rsqrt(jnp.mean(o * o, axis=-1, keepdims=True) + RMS_EPS) * gs_ref[h, rows, :].astype(F32)
                o_ref[rows, cols] = o.astype(o_ref.dtype)
            for h, cols in enumerate(heads):
                state[h] = rv_ref[2, n:n + 1, cols] * state[h] + kv_t[h] * rv_ref[1, n:n + 1, cols]
        return run

    return [chunk(n) for n in range(ts // c)]


def _attn_scan_kernel(*refs):
    attn_in, scan_in = refs[:9], refs[9:14]
    o_att_ref, o_hg_ref, st, og, ls, state = refs[14:]

    @pl.when(pl.program_id(1) == 0)
    def _():
        state[...] = jnp.zeros_like(state)

    attn_stages = _attn_program(*attn_in, o_att_ref, st, og, ls)
    scan_stages = _hgrn_scan_program(*scan_in, o_hg_ref, state)
    n_a, n_s = len(attn_stages), len(scan_stages)
    done = 0
    for idx, stage in enumerate(attn_stages):
        due = (idx + 1) * n_s // n_a
        for k in range(done, due):
            scan_stages[k]()
        done = due
        stage()


def _attention_and_scan(qkv, qa, ka, v, gs, rv, b, name):
    s = qkv.shape[1] // b
    hpg = ATT_HEADS_PER_GROUP
    n_cols = ATT_WIDTH // HEAD_DIM
    ts = s // hpg

    def attn_spec(which, g):
        return pl.BlockSpec((None, s, HEAD_DIM), lambda bi, k: (which * n_cols + g * hpg + k, bi, 0))

    head_block = pl.BlockSpec((HG_HEADS, ts, HG_KEY), lambda bi, k: (0, bi * hpg + k, 0))
    return pl.pallas_call(
        _attn_scan_kernel,
        out_shape=(jax.ShapeDtypeStruct((b, s, ATT_OUT_WIDTH), BF16), jax.ShapeDtypeStruct((b, s, HG_WIDTH), BF16)),
        grid=(b, hpg),
        in_specs=[attn_spec(which, g) for g in range(N_ATT_GROUPS) for which in range(3)]
        + [head_block, head_block, head_block, head_block,
           pl.BlockSpec((3, ts // HG_CHUNK, HG_WIDTH), lambda bi, k: (0, bi * hpg + k, 0))],
        out_specs=(pl.BlockSpec((None, s, HEAD_DIM), lambda bi, k: (bi, 0, k)),
                   pl.BlockSpec((None, ts, HG_WIDTH), lambda bi, k: (bi, k, 0))),
        scratch_shapes=[
            pltpu.VMEM((6, s, HEAD_DIM), F32),
            pltpu.VMEM((3, s, HEAD_DIM), F32),
            pltpu.VMEM((3, s, HEAD_DIM), F32),
            pltpu.VMEM((HG_HEADS, HG_VAL, HG_KEY), F32),
        ],
        compiler_params=_params(2),
        name=name,
    )(*([qkv] * 9), qa, ka, v, gs, rv)


def kernel(x, mem, ln_mix_w, w_in, hg_norm_w, hg_lower_bounds, w_branch_a, w_branch_b, w_out, ln_cross_w, ln_mem_w, wq_cross, wkv_cross, wo_cross, ln_ffn_w, w1, w3, w2, ln_final_w):
    b, s, d = x.shape
    t = b * s
    depth = w_in.shape[0]
    qkv_w = 3 * ATT_WIDTH
    hg_w = 4 * HG_WIDTH
    assert w_in.shape[2] == qkv_w + hg_w + 2 * d, w_in.shape
    assert s % MID_ROWS == 0 and s % (ATT_BLOCK * ATT_GROUPS[-1][1]) == 0 and s % (ATT_HEADS_PER_GROUP * HG_CHUNK) == 0, s
    for rows in (RMS_ROWS, PROJ_QKV_TILE[0], HG_PREP_TILE[0], GATE_MERGE_TILE[0], FFN_UP_TILE[0], FFN_DOWN_TILE[0]):
        assert t % rows == 0, (t, rows)

    xf = x.reshape(t, d)
    for l in range(depth):
        h = _rmsnorm(xf, ln_mix_w[l], BF16, "rms_mix", bm=RMS_ROWS)
        qkv = _matmul(h, w_in[l], 0, qkv_w, BF16, "proj_qkv", *PROJ_QKV_TILE, head_major=True)
        qa, ka, v, gate, rv = _hg_prep(h, w_in[l], qkv_w, hg_lower_bounds, hg_norm_w[l], l, "proj_hg_prep", *HG_PREP_TILE)
        o_att, o_hg = _attention_and_scan(qkv, qa, ka, v, gate, rv, b, "attn_and_hgrn2_scan")
        merged, (w_out_b, wq_b, wkv_b, wo_b) = _merge(
            h, o_att.reshape(t, ATT_OUT_WIDTH), o_hg.reshape(t, HG_WIDTH), w_in[l], qkv_w + hg_w,
            w_branch_a[l], w_branch_b[l], "gate_merge", *GATE_MERGE_TILE,
            riders=(w_out[l], wq_cross[l], wkv_cross[l], wo_cross[l]))
        x2, hf = _mid_block(merged.reshape(b, s, d), xf.reshape(b, s, d), mem, w_out_b, wq_b, wkv_b, wo_b,
                            ln_cross_w[l], ln_mem_w[l], ln_ffn_w[l], "out_proj_cross", bm=MID_ROWS)
        xf, hf = x2.reshape(t, d), hf.reshape(t, d)
        act, (w2_b,) = _swiglu(hf, w1[l], w3[l], "ffn_up", *FFN_UP_TILE, riders=(w2[l],))
        if l + 1 < depth:
            xf = _matmul_residual(act, w2_b, xf, "ffn_down", *FFN_DOWN_TILE)
        else:
            out = _matmul_residual_norm(act, w2_b, xf, ln_final_w, F32, False, "ffn_down_final", bm=FFN_DOWN_TILE[0],
                                        single_buffer_w=True)
    return out.reshape(b, s, d)
```
